```python
import math
import jax
import jax.numpy as jnp
from jax import lax
import numpy as np

D_MODEL = 1024
BATCH = 4
SEQ = 4096
DEPTH = 2

GRID_W = 64
CTX_LEN = 256
N_MIXERS = 2
N_SSM_LAYERS = (DEPTH + 1) // 2
N_CONV_LAYERS = DEPTH // 2

D_INNER = 2 * D_MODEL
SSM_HEADDIM = 64
SSM_HEADS = D_INNER // SSM_HEADDIM
SSM_GROUPS = 8
HEADS_PER_GROUP = SSM_HEADS // SSM_GROUPS
D_STATE = 128
SSM_CONV = 5
CHUNK = 128
CONV_DIM = D_INNER + 2 * SSM_GROUPS * D_STATE
D_IN_PROJ = D_INNER + CONV_DIM + 2 * SSM_HEADS

CONF_WIDTH = 31

N_EXPERTS = 64
TOP_K = 8
N_EXPERT_GROUPS = 8
TOPK_GROUPS = 4
D_EXPERT = 256
D_SHARED = 256
ROUTED_SCALE = 2.5
MOE_BLOCK = 128

EPS = 1e-6

kernel_name = 'hybrid_ssd_conformer_moe_dit'


def rms_norm(x, g):
    x32 = x.astype(jnp.float32)
    y = x32 * lax.rsqrt(jnp.mean(x32 * x32, axis=-1, keepdims=True) + EPS)
    return (y * g.astype(jnp.float32)).astype(x.dtype)


def layer_norm(x, g, b):
    x32 = x.astype(jnp.float32)
    mu = jnp.mean(x32, axis=-1, keepdims=True)
    xc = x32 - mu
    y = xc * lax.rsqrt(jnp.mean(xc * xc, axis=-1, keepdims=True) + EPS)
    return (y * g.astype(jnp.float32) + b.astype(jnp.float32)).astype(x.dtype)


def dwconv(x, w, b):
    k, ch = w.shape
    y = lax.conv_general_dilated(x.astype(w.dtype), w[:, None, :], window_strides=(1,),
                                 padding=[(k // 2, k // 2)],
                                 dimension_numbers=('NWC', 'WIO', 'NWC'),
                                 feature_group_count=ch)
    return y + b


def ssd_scan(x, dt, a, b, c, h0):
    bsz, L, G, R, P = x.shape
    N = b.shape[-1]
    nc = L // CHUNK
    xdt = (x * dt[..., None]).reshape(bsz, nc, CHUNK, G, R, P)
    bq = b.reshape(bsz, nc, CHUNK, G, N)
    cq = c.reshape(bsz, nc, CHUNK, G, N)
    cs = jnp.cumsum((dt * a).reshape(bsz, nc, CHUNK, G, R), axis=2)
    tri = jnp.tril(jnp.ones((CHUNK, CHUNK), dtype=bool))[None, None, :, :, None, None]
    decay = jnp.exp(jnp.where(tri, cs[:, :, :, None] - cs[:, :, None, :], -jnp.inf))
    cb = jnp.einsum('bclgn,bcsgn->bclsg', cq, bq)
    y_diag = jnp.einsum('bclsgr,bcsgrp->bclgrp', cb[..., None] * decay, xdt)
    w_end = jnp.exp(cs[:, :, -1:] - cs)
    states = jnp.einsum('bcsgn,bcsgrp->bcgrpn', bq, xdt * w_end[..., None])
    chunk_decay = jnp.exp(cs[:, :, -1])

    def step(h, inp):
        st, dec = inp
        return h * dec[..., None, None] + st, h

    h_last, h_start = lax.scan(step, h0, (jnp.moveaxis(states, 1, 0), jnp.moveaxis(chunk_decay, 1, 0)))
    h_start = jnp.moveaxis(h_start, 0, 1)
    y_off = jnp.einsum('bclgn,bcgrpn->bclgrp', cq, h_start) * jnp.exp(cs)[..., None]
    return (y_diag + y_off).reshape(bsz, L, G, R, P), h_last


def mamba_mixer(h_lat, h_ctx, w_in, conv_w, conv_b, dt_bias, a_log, d_skip, norm_w, w_out, ctx_out):
    G, R, P, N = SSM_GROUPS, HEADS_PER_GROUP, SSM_HEADDIM, D_STATE
    a = -jnp.exp(a_log.astype(jnp.float32)).reshape(2, G, R)
    dtb = dt_bias.astype(jnp.float32).reshape(2, G, R)

    def project(h):
        bsz, L, _ = h.shape
        z, xbc, dtr = jnp.split(h @ w_in, [D_INNER, D_INNER + CONV_DIM], axis=-1)
        xbc = jax.nn.silu(dwconv(xbc, conv_w, conv_b))
        xs, bs, cs = jnp.split(xbc, [D_INNER, D_INNER + G * N], axis=-1)
        dt = jax.nn.softplus(dtr.astype(jnp.float32).reshape(bsz, L, 2, G, R) + dtb)
        return (z, xs.reshape(bsz, L, G, R, P), bs.reshape(bsz, L, G, N),
                cs.reshape(bsz, L, G, N), dt)

    def flip(t):
        return jnp.flip(t, axis=1)

    def output(z, xs, y_fwd, y_bwd_rev):
        bsz, L = z.shape[:2]
        y = y_fwd + flip(y_bwd_rev) + xs * d_skip.reshape(G, R, 1)
        y = y.reshape(bsz, L, D_INNER) * jax.nn.silu(z.astype(jnp.float32))
        yg = y.reshape(bsz, L, G, D_INNER // G)
        yg = yg * lax.rsqrt(jnp.mean(yg * yg, axis=-1, keepdims=True) + EPS)
        y = yg.reshape(bsz, L, D_INNER) * norm_w
        return y.astype(w_out.dtype) @ w_out

    zc, xc, bc, cc, dtc = project(h_ctx)
    zl, xl, bl, cl, dtl = project(h_lat)
    h0 = jnp.zeros((h_lat.shape[0], G, R, P, N), jnp.float32)
    yc_f, st_f = ssd_scan(xc, dtc[:, :, 0], a[0], bc, cc, h0)
    yc_b, st_b = ssd_scan(flip(xc), flip(dtc[:, :, 1]), a[1], flip(bc), flip(cc), h0)
    yl_f, _ = ssd_scan(xl, dtl[:, :, 0], a[0], bl, cl, st_f)
    yl_b, _ = ssd_scan(flip(xl), flip(dtl[:, :, 1]), a[1], flip(bl), flip(cl), st_b)
    y_lat = output(zl, xl, yl_f, yl_b)
    y_ctx = output(zc, xc, yc_f, yc_b) if ctx_out else None
    return y_lat, y_ctx


def conformer_mixer(h, n_rows, row_len, w_in, b_in, dw_w, dw_b, ln_g, ln_b, w_out, b_out):
    bsz, L, d = h.shape
    u = h @ w_in + b_in
    u = u[..., :d] * jax.nn.sigmoid(u[..., d:])
    u = dwconv(u.reshape(bsz * n_rows, row_len, d), dw_w, dw_b).reshape(bsz, L, d)
    u = jax.nn.silu(layer_norm(u, ln_g, ln_b))
    return u @ w_out + b_out


def moe_ffn(h, router_w, router_b, w_gate, w_up, w_down, s_gate, s_up, s_down):
    T, d = h.shape
    scores = jax.nn.sigmoid(h.astype(jnp.float32) @ router_w.astype(jnp.float32))
    sel = scores + router_b.astype(jnp.float32)
    grp_score = lax.top_k(sel.reshape(T, N_EXPERT_GROUPS, -1), 2)[0].sum(-1)
    _, gidx = lax.top_k(grp_score, TOPK_GROUPS)
    gmask = jnp.zeros((T, N_EXPERT_GROUPS), bool).at[jnp.arange(T)[:, None], gidx].set(True)
    emask = jnp.repeat(gmask, N_EXPERTS // N_EXPERT_GROUPS, axis=1)
    _, eidx = lax.top_k(jnp.where(emask, sel, -jnp.inf), TOP_K)
    gate = jnp.take_along_axis(scores, eidx, axis=1)
    gate = gate / jnp.sum(gate, axis=-1, keepdims=True) * ROUTED_SCALE

    A = T * TOP_K
    flat_e = eidx.reshape(A)
    order = jnp.argsort(flat_e)
    sorted_e = flat_e[order]
    tok = (order // TOP_K).astype(jnp.int32)
    wts = gate.reshape(A)[order]
    counts = jnp.bincount(flat_e, length=N_EXPERTS)
    start = jnp.cumsum(counts) - counts
    padded = (counts + MOE_BLOCK - 1) // MOE_BLOCK * MOE_BLOCK
    pend = jnp.cumsum(padded)
    dest = (pend - padded)[sorted_e] + jnp.arange(A) - start[sorted_e]
    n_blocks = -(-(A + N_EXPERTS * (MOE_BLOCK - 1)) // MOE_BLOCK)
    P = n_blocks * MOE_BLOCK
    row_tok = jnp.zeros((P,), jnp.int32).at[dest].set(tok)
    row_w = jnp.zeros((P,), jnp.float32).at[dest].set(wts)
    block_e = jnp.minimum(jnp.searchsorted(pend, jnp.arange(n_blocks) * MOE_BLOCK, side='right'),
                          N_EXPERTS - 1)

    def expert_block(args):
        rows, e = args
        xb = h[rows]
        return (jax.nn.silu(xb @ w_gate[e]) * (xb @ w_up[e])) @ w_down[e]

    yb = lax.map(expert_block, (row_tok.reshape(n_blocks, MOE_BLOCK), block_e))
    routed = jnp.zeros((T, d), jnp.float32).at[row_tok].add(yb.reshape(P, d).astype(jnp.float32) * row_w[:, None])
    shared = (jax.nn.silu(h @ s_gate) * (h @ s_up)) @ s_down
    return (routed + shared).astype(h.dtype)


def setup_inputs(seed: int = 0) -> dict:
    key = jax.random.key(seed)
    k = jax.random.split(key, 40)
    f32 = jnp.float32

    def nrm(i, shape, scale):
        return jax.random.normal(k[i], shape, f32) * scale

    def gain(i, shape):
        return 1.0 + nrm(i, shape, 0.05)

    D, NS, NC, E = D_MODEL, N_SSM_LAYERS, N_CONV_LAYERS, N_EXPERTS
    dt0 = jnp.exp(jax.random.uniform(k[14], (NS, 2, SSM_HEADS), f32, math.log(1e-3), math.log(1e-1)))
    return {
        'x': nrm(0, (BATCH, SEQ, D), 1.0),
        'c': nrm(1, (BATCH, D), 1.0),
        'ctx': nrm(2, (BATCH, CTX_LEN, D), 1.0),
        'c_ctx': nrm(3, (D,), 1.0),
        'ada_w': nrm(4, (DEPTH, D, 6 * D), 0.5 * D ** -0.5),
        'ada_b': nrm(5, (DEPTH, 6 * D), 0.02),
        'norm_mix_pre': gain(6, (DEPTH, D)),
        'norm_mix_post': gain(7, (DEPTH, D)),
        'norm_ffn_pre': gain(8, (DEPTH, D)),
        'norm_ffn_post': gain(9, (DEPTH, D)),
        'ssm_w_in': nrm(10, (NS, D, D_IN_PROJ), D ** -0.5),
        'ssm_conv_w': nrm(11, (NS, SSM_CONV, CONV_DIM), SSM_CONV ** -0.5),
        'ssm_conv_b': nrm(12, (NS, CONV_DIM), 0.02),
        'ssm_dt_bias': dt0 + jnp.log(-jnp.expm1(-dt0)),
        'ssm_a_log': jnp.log(jax.random.uniform(k[15], (NS, 2, SSM_HEADS), f32, 1.0, 16.0)),
        'ssm_d': gain(16, (NS, SSM_HEADS)),
        'ssm_norm': gain(17, (NS, D_INNER)),
        'ssm_w_out': nrm(18, (NS, D_INNER, D), D_INNER ** -0.5),
        'cv_w_in': nrm(19, (NC, D, 2 * D), D ** -0.5),
        'cv_b_in': nrm(20, (NC, 2 * D), 0.02),
        'cv_dw_w': nrm(21, (NC, CONF_WIDTH, D), CONF_WIDTH ** -0.5),
        'cv_dw_b': nrm(22, (NC, D), 0.02),
        'cv_ln_g': gain(23, (NC, D)),
        'cv_ln_b': nrm(24, (NC, D), 0.02),
        'cv_w_out': nrm(25, (NC, D, D), D ** -0.5),
        'cv_b_out': nrm(26, (NC, D), 0.02),
        'router_w': nrm(27, (DEPTH, D, E), D ** -0.5),
        'router_b': nrm(28, (DEPTH, E), 0.01),
        'exp_w_gate': nrm(29, (DEPTH, E, D, D_EXPERT), D ** -0.5),
        'exp_w_up': nrm(30, (DEPTH, E, D, D_EXPERT), D ** -0.5),
        'exp_w_down': nrm(31, (DEPTH, E, D_EXPERT, D), D_EXPERT ** -0.5),
        'sh_w_gate': nrm(32, (DEPTH, D, D_SHARED), D ** -0.5),
        'sh_w_up': nrm(33, (DEPTH, D, D_SHARED), D ** -0.5),
        'sh_w_down': nrm(34, (DEPTH, D_SHARED, D), D_SHARED ** -0.5),
    }


def reference(x, c, ctx, c_ctx, ada_w, ada_b, norm_mix_pre, norm_mix_post, norm_ffn_pre, norm_ffn_post,
              ssm_w_in, ssm_conv_w, ssm_conv_b, ssm_dt_bias, ssm_a_log, ssm_d, ssm_norm, ssm_w_out,
              cv_w_in, cv_b_in, cv_dw_w, cv_dw_b, cv_ln_g, cv_ln_b, cv_w_out, cv_b_out,
              router_w, router_b, exp_w_gate, exp_w_up, exp_w_down, sh_w_gate, sh_w_up, sh_w_down):
    bsz, seq, d = x.shape
    rows = seq // GRID_W
    silu_c = jax.nn.silu(c)
    silu_cc = jax.nn.silu(c_ctx)
    xc = ctx
    for i in range(DEPTH):
        is_ssm = i % N_MIXERS == 0
        j = i // N_MIXERS
        ctx_later = any(m % N_MIXERS == 0 for m in range(i + 1, DEPTH))
        ctx_in = is_ssm or ctx_later
        sh1, sc1, g1, sh2, sc2, g2 = jnp.split((silu_c @ ada_w[i] + ada_b[i])[:, None, :], 6, axis=-1)
        hl = rms_norm(x, norm_mix_pre[i]) * (1 + sc1) + sh1
        if ctx_in:
            csh1, csc1, cg1, csh2, csc2, cg2 = jnp.split(silu_cc @ ada_w[i] + ada_b[i], 6)
            hc = rms_norm(xc, norm_mix_pre[i]) * (1 + csc1) + csh1
        if is_ssm:
            yl, yc = mamba_mixer(hl, hc, ssm_w_in[j], ssm_conv_w[j], ssm_conv_b[j], ssm_dt_bias[j],
                                 ssm_a_log[j], ssm_d[j], ssm_norm[j], ssm_w_out[j], ctx_later)
        else:
            cv = (cv_w_in[j], cv_b_in[j], cv_dw_w[j], cv_dw_b[j], cv_ln_g[j], cv_ln_b[j], cv_w_out[j], cv_b_out[j])
            yl = conformer_mixer(hl, rows, GRID_W, *cv)
            yc = conformer_mixer(hc, 1, hc.shape[1], *cv) if ctx_later else None
        x = x + g1 * rms_norm(yl, norm_mix_post[i])
        hl = rms_norm(x, norm_ffn_pre[i]) * (1 + sc2) + sh2
        moe_w = (router_w[i], router_b[i], exp_w_gate[i], exp_w_up[i], exp_w_down[i],
                 sh_w_gate[i], sh_w_up[i], sh_w_down[i])
        if ctx_later:
            xc = xc + cg1 * rms_norm(yc, norm_mix_post[i])
            hc = rms_norm(xc, norm_ffn_pre[i]) * (1 + csc2) + csh2
            f = moe_ffn(jnp.concatenate([hl.reshape(-1, d), hc.reshape(-1, d)], axis=0), *moe_w)
            fl = f[: bsz * seq].reshape(x.shape)
            fc = f[bsz * seq:].reshape(xc.shape)
            xc = xc + cg2 * rms_norm(fc, norm_ffn_post[i])
        else:
            fl = moe_ffn(hl.reshape(-1, d), *moe_w).reshape(x.shape)
        x = x + g2 * rms_norm(fl, norm_ffn_post[i])
    return x
```

```python
import functools

import jax
import jax.numpy as jnp
from jax import lax
from jax.experimental import pallas as pl
from jax.experimental.pallas import tpu as pltpu

F32 = jnp.float32
BF16 = jnp.bfloat16
I32 = jnp.int32

D = 1024
BATCH = 4
SEQ = 4096
CTX = 256
LTOT = CTX + SEQ
GRID_W = 64

D_INNER = 2048
HEADS = 32
GROUPS = 8
HPG = 4
HEADDIM = 64
NSTATE = 128
Q = 128
NCHUNK = LTOT // Q
CTX_CHUNKS = CTX // Q
CONV_DIM = D_INNER + 2 * GROUPS * NSTATE
SSM_K = 5
CONF_K = 31

E = 64
TOPK = 8
NGRP = 8
TOPG = 4
DE = 256
ROUTED_SCALE = 2.5
EPS = 1e-6

T = BATCH * SEQ
TM = 256
BLK = 256
NBLK = -(-(T * TOPK + E * (BLK - 1)) // BLK)
PROWS = (NBLK + 1) * BLK + 8

VMEM_LIMIT = 56 * 1024 * 1024
NEG = -1e30


def _cp(sem):
    return pltpu.CompilerParams(dimension_semantics=sem, vmem_limit_bytes=VMEM_LIMIT)


def _silu(v):
    return v * jax.nn.sigmoid(v)


def _rms(v, g):
    return v * lax.rsqrt(jnp.mean(v * v, axis=-1, keepdims=True) + EPS) * g


def _split3(v):
    a = v.astype(BF16)
    r = v - a.astype(F32)
    b = r.astype(BF16)
    c = (r - b.astype(F32)).astype(BF16)
    return a, b, c


def _dot(a, b):
    return jnp.dot(a, b, preferred_element_type=F32)


def _dot_nt(a, b):
    return lax.dot_general(a, b, (((1,), (1,)), ((), ())), preferred_element_type=F32)


def _ada_kernel(c_ref, w_ref, b_ref, o_ref):
    s = _silu(c_ref[...])
    o_ref[...] = jnp.dot(s, w_ref[...], preferred_element_type=F32,
                         precision=lax.Precision.HIGHEST) + b_ref[...]


def _ada(cvec, ada_w, ada_b):
    depth = ada_w.shape[0]
    tn = 1536
    return pl.pallas_call(
        _ada_kernel,
        grid=(depth, 6 * D // tn),
        in_specs=[pl.BlockSpec((8, D), lambda l, j: (0, 0)),
                  pl.BlockSpec((None, D, tn), lambda l, j: (l, 0, j)),
                  pl.BlockSpec((None, 1, tn), lambda l, j: (l, 0, j))],
        out_specs=pl.BlockSpec((None, 8, tn), lambda l, j: (l, 0, j)),
        out_shape=jax.ShapeDtypeStruct((depth, 8, 6 * D), F32),
        compiler_params=_cp(("arbitrary", "arbitrary")),
        name="ada",
    )(cvec, ada_w, ada_b.reshape(depth, 1, 6 * D))


def _ssm_in_kernel(x_ref, mod_ref, g_ref, wz_ref, wx_ref, wdt_ref, wdtT_ref, dtb_ref, dtbT_ref,
                   z_ref, xbc_ref, dt_ref, dtT_ref):
    h = _rms(x_ref[...], g_ref[...]) * (1.0 + mod_ref[1:2, :]) + mod_ref[0:1, :]
    hb = h.astype(BF16)
    z_ref[...] = _dot(hb, wz_ref[...]).astype(BF16)
    xbc_ref[...] = _dot(hb, wx_ref[...]).astype(BF16)
    dt_ref[...] = jax.nn.softplus(_dot(hb, wdt_ref[...]) + dtb_ref[...])
    dtT_ref[...] = jax.nn.softplus(_dot_nt(wdtT_ref[...], hb) + dtbT_ref[...])


def _ssm_in(hcat, mod, gain, wz, wx, wdt, wdtT, dtb, dtbT):
    nt = LTOT // TM
    full = lambda b, i: (0, 0)
    return pl.pallas_call(
        _ssm_in_kernel,
        grid=(BATCH, nt),
        in_specs=[pl.BlockSpec((None, TM, D), lambda b, i: (b, i, 0)),
                  pl.BlockSpec((None, 6, D), lambda b, i: (jnp.where(i == 0, BATCH, b), 0, 0)),
                  pl.BlockSpec((1, D), full),
                  pl.BlockSpec((D, D_INNER), full),
                  pl.BlockSpec((D, CONV_DIM), full),
                  pl.BlockSpec((D, 2 * HEADS), full),
                  pl.BlockSpec((2 * HEADS, D), full),
                  pl.BlockSpec((1, 2 * HEADS), full),
                  pl.BlockSpec((2 * HEADS, 1), full)],
        out_specs=[pl.BlockSpec((None, TM, D_INNER), lambda b, i: (b, i, 0)),
                   pl.BlockSpec((None, TM, CONV_DIM), lambda b, i: (b, i, 0)),
                   pl.BlockSpec((None, TM, 2 * HEADS), lambda b, i: (b, i, 0)),
                   pl.BlockSpec((None, 2 * HEADS, TM), lambda b, i: (b, 0, i))],
        out_shape=[jax.ShapeDtypeStruct((BATCH, LTOT, D_INNER), BF16),
                   jax.ShapeDtypeStruct((BATCH, LTOT, CONV_DIM), BF16),
                   jax.ShapeDtypeStruct((BATCH, LTOT, 2 * HEADS), F32),
                   jax.ShapeDtypeStruct((BATCH, 2 * HEADS, LTOT), F32)],
        compiler_params=_cp(("arbitrary", "arbitrary")),
        name="ssm_in",
    )(hcat, mod, gain, wz, wx, wdt, wdtT, dtb, dtbT)


_HALO = 16
_CC = 512


def _ssm_conv_kernel(x_ref, xp_ref, xn_ref, w_ref, b_ref, xs_ref, bt_ref, cm_ref, ext_ref):
    i = pl.program_id(1)
    first_lat = CTX // Q
    pv = jnp.where((i == 0) | (i == first_lat), 0.0, 1.0)
    nv = jnp.where((i == first_lat - 1) | (i == NCHUNK - 1), 0.0, 1.0)
    for c in range(CONV_DIM // _CC):
        cs = slice(c * _CC, (c + 1) * _CC)
        ext_ref[0:8, :] = xp_ref[:, cs].astype(F32)[8:16, :] * pv
        ext_ref[8:8 + Q, :] = x_ref[:, cs].astype(F32)
        ext_ref[8 + Q:16 + Q, :] = xn_ref[:, cs].astype(F32)[0:8, :] * nv
        acc = jnp.broadcast_to(b_ref[:, cs], (Q, _CC))
        for k in range(SSM_K):
            acc = acc + w_ref[k:k + 1, cs] * ext_ref[6 + k:6 + k + Q, :]
        y = _silu(acc)
        lo = c * _CC
        if lo < D_INNER:
            xs_ref[:, cs] = y.astype(BF16)
        elif lo < D_INNER + GROUPS * NSTATE:
            o = lo - D_INNER
            bt_ref[o:o + _CC, :] = y.T.astype(BF16)
        else:
            o = lo - D_INNER - GROUPS * NSTATE
            cm_ref[:, o:o + _CC] = y.astype(BF16)


def _ssm_conv(xbc, w, b):
    nh = Q // _HALO
    last = LTOT // _HALO - 1
    gn = GROUPS * NSTATE
    return pl.pallas_call(
        _ssm_conv_kernel,
        grid=(BATCH, NCHUNK),
        in_specs=[pl.BlockSpec((None, Q, CONV_DIM), lambda b, i: (b, i, 0)),
                  pl.BlockSpec((None, _HALO, CONV_DIM), lambda b, i: (b, jnp.maximum(i * nh - 1, 0), 0)),
                  pl.BlockSpec((None, _HALO, CONV_DIM), lambda b, i: (b, jnp.minimum(i * nh + nh, last), 0)),
                  pl.BlockSpec((SSM_K, CONV_DIM), lambda b, i: (0, 0)),
                  pl.BlockSpec((1, CONV_DIM), lambda b, i: (0, 0))],
        out_specs=[pl.BlockSpec((None, Q, D_INNER), lambda b, i: (b, i, 0)),
                   pl.BlockSpec((None, gn, Q), lambda b, i: (b, 0, i)),
                   pl.BlockSpec((None, Q, gn), lambda b, i: (b, i, 0))],
        out_shape=[jax.ShapeDtypeStruct((BATCH, LTOT, D_INNER), BF16),
                   jax.ShapeDtypeStruct((BATCH, gn, LTOT), BF16),
                   jax.ShapeDtypeStruct((BATCH, LTOT, gn), BF16)],
        scratch_shapes=[pltpu.VMEM((Q + 16, _CC), F32)],
        compiler_params=_cp(("arbitrary", "arbitrary")),
        name="ssm_conv",
    )(xbc, xbc, xbc, w, b)


def _ssd_chunk(direction, xs_ref, cm_ref, bt_ref, dt_ref, dtT_ref, arow_ref, acol_ref, rexp_ref, s_ref):
    d0 = direction * HEADS
    dtc = dt_ref[:, d0:d0 + HEADS]
    dtr = dtT_ref[d0:d0 + HEADS, :]
    da_c = dtc * arow_ref[:, d0:d0 + HEADS]
    da_r = dtr * acol_ref[d0:d0 + HEADS, :]
    ii = lax.broadcasted_iota(I32, (Q, Q), 0)
    jj = lax.broadcasted_iota(I32, (Q, Q), 1)
    if direction == 0:
        lower = jj <= ii
        tot_idx = Q - 1
    else:
        lower = jj >= ii
        tot_idx = 0
    tri_c = jnp.where(lower, 1.0, 0.0).astype(BF16)
    upper = (ii <= jj) if direction == 0 else (ii >= jj)
    tri_r = jnp.where(upper, 1.0, 0.0).astype(BF16)
    c1, c2, c3 = _split3(da_c)
    cum_c = _dot(tri_c, c1) + _dot(tri_c, c2) + _dot(tri_c, c3)
    r1, r2, r3 = _split3(da_r)
    cum_r = _dot(r1, tri_r) + _dot(r2, tri_r) + _dot(r3, tri_r)
    tot_c = cum_c[tot_idx:tot_idx + 1, :]
    tot_r = cum_r[:, tot_idx:tot_idx + 1]
    rfac = dtr * jnp.exp(tot_r - cum_r)
    dec = jnp.exp(tot_c)
    dh = dec.astype(BF16)
    dl = (dec - dh.astype(F32)).astype(BF16)
    dec_x = (_dot(jnp.broadcast_to(dh, (8, HEADS)), rexp_ref[...])
             + _dot(jnp.broadcast_to(dl, (8, HEADS)), rexp_ref[...]))[0:1, :]
    lane = lax.broadcasted_iota(I32, (Q, HPG * HEADDIM), 1)
    ys = []
    for g in range(GROUPS):
        cg = cm_ref[:, g * NSTATE:(g + 1) * NSTATE]
        btg = bt_ref[g * NSTATE:(g + 1) * NSTATE, :]
        xg = xs_ref[:, g * 256:(g + 1) * 256]
        sg = s_ref[g]
        cb = _dot(cg, btg)
        rhs = jnp.concatenate([xg, sg.astype(BF16)], axis=0)
        cg32 = cg.astype(F32)
        btg32 = btg.astype(F32)
        yg = None
        ug = None
        for r in range(HPG):
            h = g * HPG + r
            colb = cum_c[:, h:h + 1]
            rowb = cum_r[h:h + 1, :]
            decay = jnp.exp(jnp.where(lower, colb - rowb, NEG))
            m = (cb * decay * dtr[h:h + 1, :]).astype(BF16)
            cs = (cg32 * jnp.exp(colb)).astype(BF16)
            res = _dot(jnp.concatenate([m, cs], axis=1), rhs)
            bw = (btg32 * rfac[h:h + 1, :]).astype(BF16)
            upd = _dot(bw, xg)
            if r == 0:
                yg, ug = res, upd
            else:
                sel = lane >= r * HEADDIM
                yg = jnp.where(sel, res, yg)
                ug = jnp.where(sel, upd, ug)
        s_ref[g] = sg * dec_x[:, g * 256:(g + 1) * 256] + ug
        ys.append(yg)
    return ys


def _ssd_bwd_kernel(xs_ref, cm_ref, bt_ref, dt_ref, dtT_ref, arow_ref, acol_ref, rexp_ref,
                    y_ref, s_ref):
    @pl.when(pl.program_id(1) == 0)
    def _():
        s_ref[...] = jnp.zeros_like(s_ref)
    ys = _ssd_chunk(1, xs_ref, cm_ref, bt_ref, dt_ref, dtT_ref, arow_ref, acol_ref, rexp_ref, s_ref)
    for g in range(GROUPS):
        y_ref[:, g * 256:(g + 1) * 256] = ys[g].astype(BF16)


def _ssd_fwd_kernel(xs_ref, cm_ref, bt_ref, dt_ref, dtT_ref, arow_ref, acol_ref, rexp_ref,
                    z_ref, yb_ref, dsk_ref, nw_ref, y_ref, s_ref):
    @pl.when(pl.program_id(1) == 0)
    def _():
        s_ref[...] = jnp.zeros_like(s_ref)
    ys = _ssd_chunk(0, xs_ref, cm_ref, bt_ref, dt_ref, dtT_ref, arow_ref, acol_ref, rexp_ref, s_ref)
    for g in range(GROUPS):
        gs = slice(g * 256, (g + 1) * 256)
        y = ys[g] + yb_ref[:, gs].astype(F32) + xs_ref[:, gs].astype(F32) * dsk_ref[:, gs]
        y = y * _silu(z_ref[:, gs].astype(F32))
        y = y * lax.rsqrt(jnp.mean(y * y, axis=-1, keepdims=True) + EPS) * nw_ref[:, gs]
        y_ref[:, gs] = y.astype(BF16)


def _ssd_specs(cmap):
    gn = GROUPS * NSTATE
    full = lambda b, j: (0, 0)
    return [pl.BlockSpec((None, Q, D_INNER), lambda b, j: (b, cmap(j), 0)),
            pl.BlockSpec((None, Q, gn), lambda b, j: (b, cmap(j), 0)),
            pl.BlockSpec((None, gn, Q), lambda b, j: (b, 0, cmap(j))),
            pl.BlockSpec((None, Q, 2 * HEADS), lambda b, j: (b, cmap(j), 0)),
            pl.BlockSpec((None, 2 * HEADS, Q), lambda b, j: (b, 0, cmap(j))),
            pl.BlockSpec((1, 2 * HEADS), full),
            pl.BlockSpec((2 * HEADS, 1), full),
            pl.BlockSpec((HEADS, D_INNER), full)]


def _ssd_bwd(xs, cm, bt, dt, dtT, arow, acol, rexp):
    cmap = lambda j: jnp.where(j < CTX_CHUNKS, CTX_CHUNKS - 1 - j, NCHUNK + CTX_CHUNKS - 1 - j)
    omap = lambda b, j: (b, NCHUNK - 1 - jnp.maximum(j, CTX_CHUNKS), 0)
    return pl.pallas_call(
        _ssd_bwd_kernel,
        grid=(BATCH, NCHUNK),
        in_specs=_ssd_specs(cmap),
        out_specs=pl.BlockSpec((None, Q, D_INNER), omap),
        out_shape=jax.ShapeDtypeStruct((BATCH, SEQ, D_INNER), BF16),
        scratch_shapes=[pltpu.VMEM((GROUPS, NSTATE, HPG * HEADDIM), F32)],
        compiler_params=_cp(("arbitrary", "arbitrary")),
        name="ssd_bwd",
    )(xs, cm, bt, dt, dtT, arow, acol, rexp)


def _ssd_fwd(xs, cm, bt, dt, dtT, arow, acol, rexp, z, yb, dsk, nw):
    cmap = lambda j: j
    lat = lambda b, j: (b, jnp.maximum(j - CTX_CHUNKS, 0), 0)
    full = lambda b, j: (0, 0)
    return pl.pallas_call(
        _ssd_fwd_kernel,
        grid=(BATCH, NCHUNK),
        in_specs=_ssd_specs(cmap) + [
            pl.BlockSpec((None, Q, D_INNER), lambda b, j: (b, j, 0)),
            pl.BlockSpec((None, Q, D_INNER), lat),
            pl.BlockSpec((1, D_INNER), full),
            pl.BlockSpec((1, D_INNER), full)],
        out_specs=pl.BlockSpec((None, Q, D_INNER), lat),
        out_shape=jax.ShapeDtypeStruct((BATCH, SEQ, D_INNER), BF16),
        scratch_shapes=[pltpu.VMEM((GROUPS, NSTATE, HPG * HEADDIM), F32)],
        compiler_params=_cp(("arbitrary", "arbitrary")),
        name="ssd_fwd",
    )(xs, cm, bt, dt, dtT, arow, acol, rexp, z, yb, dsk, nw)


def _route(h, rwT_ref, rb_ref, cnt_ref, eidx_ref, pos_ref, gate_ref):
    hh = h.astype(BF16)
    hl = (h - hh.astype(F32)).astype(BF16)
    w = rwT_ref[...]
    wh = w.astype(BF16)
    wl = (w - wh.astype(F32)).astype(BF16)
    logits = _dot_nt(wh, hh) + _dot_nt(wh, hl) + _dot_nt(wl, hh)
    scores = jax.nn.sigmoid(logits)
    sel = scores + rb_ref[...]
    per = E // NGRP
    sub = lax.broadcasted_iota(I32, (per, TM), 0)
    gscore = []
    for g in range(NGRP):
        blk = sel[g * per:(g + 1) * per, :]
        m1 = jnp.max(blk, axis=0, keepdims=True)
        first = jnp.min(jnp.where(blk == m1, sub, per), axis=0, keepdims=True)
        m2 = jnp.max(jnp.where(sub == first, -jnp.inf, blk), axis=0, keepdims=True)
        gscore.append(m1 + m2)
    masked = []
    for g in range(NGRP):
        rank = jnp.zeros((1, TM), F32)
        for o in range(NGRP):
            if o == g:
                continue
            ahead = (gscore[o] >= gscore[g]) if o < g else (gscore[o] > gscore[g])
            rank = rank + jnp.where(ahead, 1.0, 0.0)
        blk = sel[g * per:(g + 1) * per, :]
        masked.append(jnp.where(rank < TOPG, blk, -jnp.inf))
    v = jnp.concatenate(masked, axis=0)
    eio = lax.broadcasted_iota(I32, (E, TM), 0)
    kio = lax.broadcasted_iota(I32, (TOPK, TM), 0)
    eidx = jnp.zeros((TOPK, TM), I32)
    gsc = jnp.zeros((TOPK, TM), F32)
    hot = jnp.zeros((E, TM), F32)
    picks = []
    for k in range(TOPK):
        m = jnp.max(v, axis=0, keepdims=True)
        first = jnp.min(jnp.where(v == m, eio, E), axis=0, keepdims=True)
        pick = eio == first
        sc = jnp.sum(jnp.where(pick, scores, 0.0), axis=0, keepdims=True)
        eidx = jnp.where(kio == k, first, eidx)
        gsc = jnp.where(kio == k, sc, gsc)
        hot = jnp.where(pick, 1.0, hot)
        v = jnp.where(pick, -jnp.inf, v)
        picks.append(pick)
    gate = gsc / jnp.sum(gsc, axis=0, keepdims=True) * ROUTED_SCALE
    ti = lax.broadcasted_iota(I32, (TM, TM), 0)
    tj = lax.broadcasted_iota(I32, (TM, TM), 1)
    before = jnp.where(ti < tj, 1.0, 0.0).astype(BF16)
    posfull = _dot(hot.astype(BF16), before) + cnt_ref[...]
    pos = jnp.zeros((TOPK, TM), F32)
    for k in range(TOPK):
        pk = jnp.sum(jnp.where(picks[k], posfull, 0.0), axis=0, keepdims=True)
        pos = jnp.where(kio == k, pk, pos)
    cnt_ref[...] = cnt_ref[...] + jnp.sum(hot, axis=1, keepdims=True)
    eidx_ref[...] = eidx
    pos_ref[...] = pos.astype(I32)
    eye = jnp.where(ti == tj, 1.0, 0.0).astype(BF16)
    g1, g2, g3 = _split3(gate)
    gate_ref[...] = _dot_nt(eye, g1) + _dot_nt(eye, g2) + _dot_nt(eye, g3)


def _mix_epilogue(y, x_ref, mod_ref, gpost_ref, gpre_ref, rwT_ref, rb_ref,
                  x1_ref, h2_ref, eidx_ref, pos_ref, gate_ref, cnt_out_ref, cnt_ref):
    first = (pl.program_id(0) == 0) & (pl.program_id(1) == 0)

    @pl.when(first)
    def _():
        cnt_ref[...] = jnp.zeros_like(cnt_ref)
    x1 = x_ref[...] + mod_ref[2:3, :] * _rms(y, gpost_ref[...])
    x1_ref[...] = x1
    h2 = _rms(x1, gpre_ref[...]) * (1.0 + mod_ref[4:5, :]) + mod_ref[3:4, :]
    h2_ref[...] = h2
    _route(h2, rwT_ref, rb_ref, cnt_ref, eidx_ref, pos_ref, gate_ref)
    cnt_out_ref[...] = cnt_ref[...]


def _ssm_out_kernel(y_ref, w_ref, x_ref, mod_ref, gpost_ref, gpre_ref, rwT_ref, rb_ref,
                    x1_ref, h2_ref, eidx_ref, pos_ref, gate_ref, cnt_out_ref, cnt_ref):
    y = _dot(y_ref[...], w_ref[...])
    _mix_epilogue(y, x_ref, mod_ref, gpost_ref, gpre_ref, rwT_ref, rb_ref,
                  x1_ref, h2_ref, eidx_ref, pos_ref, gate_ref, cnt_out_ref, cnt_ref)


_CW = 256
_PAD = 16


def _conf_out_kernel(u_ref, dww_ref, dwb_ref, lng_ref, lnb_ref, w_ref, b_ref,
                     x_ref, mod_ref, gpost_ref, gpre_ref, rwT_ref, rb_ref,
                     x1_ref, h2_ref, eidx_ref, pos_ref, gate_ref, cnt_out_ref,
                     cnt_ref, ext_ref, v_ref):
    nrow = TM // GRID_W
    zpad = jnp.zeros((_PAD, D), F32)
    for r in range(nrow):
        base = r * (GRID_W + 2 * _PAD)
        ext_ref[base:base + _PAD, :] = zpad
        ext_ref[base + _PAD:base + _PAD + GRID_W, :] = u_ref[r * GRID_W:(r + 1) * GRID_W, :].astype(F32)
        ext_ref[base + _PAD + GRID_W:base + 2 * _PAD + GRID_W, :] = zpad
    for r in range(nrow):
        base = r * (GRID_W + 2 * _PAD)
        for c in range(D // _CW):
            cs = slice(c * _CW, (c + 1) * _CW)
            acc = jnp.broadcast_to(dwb_ref[:, cs], (GRID_W, _CW))
            for k in range(CONF_K):
                o = base + _PAD - CONF_K // 2 + k
                acc = acc + dww_ref[k:k + 1, cs] * ext_ref[o:o + GRID_W, cs]
            v_ref[r * GRID_W:(r + 1) * GRID_W, cs] = acc
    v = v_ref[...]
    mu = jnp.mean(v, axis=-1, keepdims=True)
    vc = v - mu
    ln = vc * lax.rsqrt(jnp.mean(vc * vc, axis=-1, keepdims=True) + EPS) * lng_ref[...] + lnb_ref[...]
    y = _dot(_silu(ln).astype(BF16), w_ref[...]) + b_ref[...]
    _mix_epilogue(y, x_ref, mod_ref, gpost_ref, gpre_ref, rwT_ref, rb_ref,
                  x1_ref, h2_ref, eidx_ref, pos_ref, gate_ref, cnt_out_ref, cnt_ref)


def _mix_out_common_specs():
    full = lambda b, i: (0, 0)
    nt = SEQ // TM
    in_specs = [pl.BlockSpec((None, TM, D), lambda b, i: (b, i, 0)),
                pl.BlockSpec((None, 6, D), lambda b, i: (b, 0, 0)),
                pl.BlockSpec((1, D), full), pl.BlockSpec((1, D), full),
                pl.BlockSpec((E, D), full), pl.BlockSpec((E, 1), full)]
    out_specs = [pl.BlockSpec((None, TM, D), lambda b, i: (b, i, 0)),
                 pl.BlockSpec((None, TM, D), lambda b, i: (b, i, 0)),
                 pl.BlockSpec((TOPK, TM), lambda b, i: (0, b * nt + i)),
                 pl.BlockSpec((TOPK, TM), lambda b, i: (0, b * nt + i)),
                 pl.BlockSpec((TM, TOPK), lambda b, i: (b * nt + i, 0)),
                 pl.BlockSpec((E, 1), full)]
    out_shape = [jax.ShapeDtypeStruct((BATCH, SEQ, D), F32),
                 jax.ShapeDtypeStruct((BATCH, SEQ, D), F32),
                 jax.ShapeDtypeStruct((TOPK, T), I32),
                 jax.ShapeDtypeStruct((TOPK, T), I32),
                 jax.ShapeDtypeStruct((T, TOPK), F32),
                 jax.ShapeDtypeStruct((E, 1), F32)]
    return in_specs, out_specs, out_shape


def _ssm_out(y, w, x, mod, gpost, gpre, rwT, rb):
    common_in, out_specs, out_shape = _mix_out_common_specs()
    return pl.pallas_call(
        _ssm_out_kernel,
        grid=(BATCH, SEQ // TM),
        in_specs=[pl.BlockSpec((None, TM, D_INNER), lambda b, i: (b, i, 0)),
                  pl.BlockSpec((D_INNER, D), lambda b, i: (0, 0))] + common_in,
        out_specs=out_specs, out_shape=out_shape,
        scratch_shapes=[pltpu.VMEM((E, 1), F32)],
        compiler_params=_cp(("arbitrary", "arbitrary")),
        name="ssm_out",
    )(y, w, x, mod, gpost, gpre, rwT, rb)


def _conf_out(u, dww, dwb, lng, lnb, w, b, x, mod, gpost, gpre, rwT, rb):
    common_in, out_specs, out_shape = _mix_out_common_specs()
    full = lambda b_, i: (0, 0)
    nrow = TM // GRID_W
    return pl.pallas_call(
        _conf_out_kernel,
        grid=(BATCH, SEQ // TM),
        in_specs=[pl.BlockSpec((None, TM, D), lambda b_, i: (b_, i, 0)),
                  pl.BlockSpec((CONF_K, D), full), pl.BlockSpec((1, D), full),
                  pl.BlockSpec((1, D), full), pl.BlockSpec((1, D), full),
                  pl.BlockSpec((D, D), full), pl.BlockSpec((1, D), full)] + common_in,
        out_specs=out_specs, out_shape=out_shape,
        scratch_shapes=[pltpu.VMEM((E, 1), F32),
                        pltpu.VMEM((nrow * (GRID_W + 2 * _PAD), D), F32),
                        pltpu.VMEM((TM, D), F32)],
        compiler_params=_cp(("arbitrary", "arbitrary")),
        name="conf_out",
    )(u, dww, dwb, lng, lnb, w, b, x, mod, gpost, gpre, rwT, rb)


def _conf_in_kernel(x_ref, mod_ref, g_ref, wa_ref, wg_ref, ba_ref, bg_ref, u_ref):
    h = _rms(x_ref[...], g_ref[...]) * (1.0 + mod_ref[1:2, :]) + mod_ref[0:1, :]
    hb = h.astype(BF16)
    a = _dot(hb, wa_ref[...]) + ba_ref[...]
    g = _dot(hb, wg_ref[...]) + bg_ref[...]
    u_ref[...] = (a * jax.nn.sigmoid(g)).astype(BF16)


def _conf_in(x, mod, gain, wa, wg, ba, bg):
    full = lambda b, i: (0, 0)
    return pl.pallas_call(
        _conf_in_kernel,
        grid=(BATCH, SEQ // TM),
        in_specs=[pl.BlockSpec((None, TM, D), lambda b, i: (b, i, 0)),
                  pl.BlockSpec((None, 6, D), lambda b, i: (b, 0, 0)),
                  pl.BlockSpec((1, D), full),
                  pl.BlockSpec((D, D), full), pl.BlockSpec((D, D), full),
                  pl.BlockSpec((1, D), full), pl.BlockSpec((1, D), full)],
        out_specs=pl.BlockSpec((None, TM, D), lambda b, i: (b, i, 0)),
        out_shape=jax.ShapeDtypeStruct((BATCH, SEQ, D), BF16),
        compiler_params=_cp(("arbitrary", "arbitrary")),
        name="conf_in",
    )(x, mod, gain, wa, wg, ba, bg)


def _row_copy(src, si, dst, di, sem):
    return pltpu.make_async_copy(src.at[pl.ds(si, 1), :], dst.at[pl.ds(di, 1), :], sem)


def _dispatch_kernel(zs_ref, h_ref, dest_ref, xs_ref, zero_ref, sem):
    @pl.when(pl.program_id(0) == 0)
    def _():
        zero_ref[...] = jnp.zeros_like(zero_ref)

        def zcopy(e):
            start = pl.multiple_of(zs_ref[e], 8)
            return pltpu.make_async_copy(zero_ref, xs_ref.at[pl.ds(start, BLK + 8), :], sem)

        def zstart(e, c):
            zcopy(e).start()
            return c

        def zwait(e, c):
            zcopy(e).wait()
            return c
        lax.fori_loop(0, E, zstart, 0)
        lax.fori_loop(0, E, zwait, 0)

    def issue(t, c):
        for k in range(TOPK):
            _row_copy(h_ref, t, xs_ref, dest_ref[k, t], sem).start()
        return c

    def drain(t, c):
        for k in range(TOPK):
            _row_copy(h_ref, t, xs_ref, dest_ref[k, t], sem).wait()
        return c
    lax.fori_loop(0, TM, issue, 0)
    lax.fori_loop(0, TM, drain, 0)


def _dispatch(zstart, h2, dest):
    grid_spec = pltpu.PrefetchScalarGridSpec(
        num_scalar_prefetch=1,
        grid=(T // TM,),
        in_specs=[pl.BlockSpec((TM, D), lambda i, zs: (i, 0)),
                  pl.BlockSpec((TOPK, TM), lambda i, zs: (0, i), memory_space=pltpu.SMEM)],
        out_specs=pl.BlockSpec(memory_space=pl.ANY),
        scratch_shapes=[pltpu.VMEM((BLK + 8, D), F32), pltpu.SemaphoreType.DMA(())])
    return pl.pallas_call(
        _dispatch_kernel,
        grid_spec=grid_spec,
        out_shape=jax.ShapeDtypeStruct((PROWS, D), F32),
        compiler_params=_cp(("arbitrary",)),
        name="moe_dispatch",
    )(zstart, h2, dest)


def _experts_kernel(be_ref, na_ref, x_ref, wg_ref, wu_ref, wd_ref, y_ref):
    i = pl.program_id(0)

    @pl.when(i < na_ref[0])
    def _():
        xb = x_ref[...].astype(BF16)
        g = _dot(xb, wg_ref[...].astype(BF16))
        u = _dot(xb, wu_ref[...].astype(BF16))
        a = (_silu(g) * u).astype(BF16)
        y_ref[...] = _dot(a, wd_ref[...].astype(BF16))

    @pl.when(i >= na_ref[0])
    def _():
        y_ref[...] = jnp.zeros_like(y_ref)


def _experts(block_e, nact, xs, wg, wu, wd):
    xmap = lambda i, be, na: (jnp.minimum(i, na[0] - 1), 0)
    wmap = lambda i, be, na: (be[i], 0, 0)
    grid_spec = pltpu.PrefetchScalarGridSpec(
        num_scalar_prefetch=2,
        grid=(NBLK,),
        in_specs=[pl.BlockSpec((BLK, D), xmap),
                  pl.BlockSpec((None, D, DE), wmap),
                  pl.BlockSpec((None, D, DE), wmap),
                  pl.BlockSpec((None, DE, D), wmap)],
        out_specs=pl.BlockSpec((BLK, D), lambda i, be, na: (i, 0)))
    return pl.pallas_call(
        _experts_kernel,
        grid_spec=grid_spec,
        out_shape=jax.ShapeDtypeStruct((NBLK * BLK, D), F32),
        compiler_params=_cp(("arbitrary",)),
        name="moe_experts",
    )(block_e, nact, xs, wg, wu, wd)


def _combine_kernel(dest_ref, gate_ref, h_ref, x1_ref, mod_ref, gpost_ref, sg_ref, su_ref, sd_ref,
                    ys_ref, o_ref, buf_ref, sem):
    def issue(t, c):
        for k in range(TOPK):
            _row_copy(ys_ref, dest_ref[k, t], buf_ref.at[k], t, sem).start()
        return c

    def drain(t, c):
        for k in range(TOPK):
            _row_copy(ys_ref, dest_ref[k, t], buf_ref.at[k], t, sem).wait()
        return c
    lax.fori_loop(0, TM, issue, 0)
    hb = h_ref[...].astype(BF16)
    a = (_silu(_dot(hb, sg_ref[...])) * _dot(hb, su_ref[...])).astype(BF16)
    f = _dot(a, sd_ref[...])
    lax.fori_loop(0, TM, drain, 0)
    gate = gate_ref[...]
    for k in range(TOPK):
        f = f + gate[:, k:k + 1] * buf_ref[k]
    o_ref[...] = x1_ref[...] + mod_ref[5:6, :] * _rms(f, gpost_ref[...])


def _combine(dest, gate, h2, x1, mod, gpost, sg, su, sd, ys):
    nt = SEQ // TM
    full = lambda b, i: (0, 0)
    return pl.pallas_call(
        _combine_kernel,
        grid=(BATCH, nt),
        in_specs=[pl.BlockSpec((TOPK, TM), lambda b, i: (0, b * nt + i), memory_space=pltpu.SMEM),
                  pl.BlockSpec((TM, TOPK), lambda b, i: (b * nt + i, 0)),
                  pl.BlockSpec((None, TM, D), lambda b, i: (b, i, 0)),
                  pl.BlockSpec((None, TM, D), lambda b, i: (b, i, 0)),
                  pl.BlockSpec((None, 6, D), lambda b, i: (b, 0, 0)),
                  pl.BlockSpec((1, D), full),
                  pl.BlockSpec((D, DE), full), pl.BlockSpec((D, DE), full), pl.BlockSpec((DE, D), full),
                  pl.BlockSpec(memory_space=pl.ANY)],
        out_specs=pl.BlockSpec((None, TM, D), lambda b, i: (b, i, 0)),
        out_shape=jax.ShapeDtypeStruct((BATCH, SEQ, D), F32),
        scratch_shapes=[pltpu.VMEM((TOPK, TM, D), F32), pltpu.SemaphoreType.DMA(())],
        compiler_params=_cp(("arbitrary", "arbitrary")),
        name="moe_combine",
    )(dest, gate, h2, x1, mod, gpost, sg, su, sd, ys)


def _moe(x1, h2, eidx, pos, gate, counts, mod, gpost, wg, wu, wd, sg, su, sd):
    cnt = counts.reshape(E).astype(I32)
    padded = (cnt + BLK - 1) // BLK * BLK
    pend = jnp.cumsum(padded)
    pstart = pend - padded
    dest = jnp.take(pstart, eidx) + pos
    nact = (pend[-1] // BLK).reshape(1).astype(I32)
    block_e = jnp.searchsorted(pend, jnp.arange(NBLK, dtype=I32) * BLK, side="right")
    block_e = jnp.minimum(block_e, E - 1).astype(I32)
    xs = _dispatch(((pstart + cnt) // 8 * 8).astype(I32), h2.reshape(T, D), dest)
    ys = _experts(block_e, nact, xs, wg, wu, wd)
    return _combine(dest, gate, h2, x1, mod, gpost, sg.astype(BF16), su.astype(BF16),
                    sd.astype(BF16), ys)


def kernel(x, c, ctx, c_ctx, ada_w, ada_b, norm_mix_pre, norm_mix_post, norm_ffn_pre, norm_ffn_post, ssm_w_in, ssm_conv_w, ssm_conv_b, ssm_dt_bias, ssm_a_log, ssm_d, ssm_norm, ssm_w_out, cv_w_in, cv_b_in, cv_dw_w, cv_dw_b, cv_ln_g, cv_ln_b, cv_w_out, cv_b_out, router_w, router_b, exp_w_gate, exp_w_up, exp_w_down, sh_w_gate, sh_w_up, sh_w_down):
    row = lambda v: v.reshape(1, -1)
    cvec = jnp.concatenate([c, c_ctx[None, :], jnp.zeros((3, D), F32)], axis=0)
    mod = _ada(cvec, ada_w, ada_b).reshape(2, 8, 6, D)

    hcat = jnp.concatenate([ctx, x], axis=1)
    w_in = ssm_w_in[0].astype(BF16)
    wz = w_in[:, :D_INNER]
    wx = w_in[:, D_INNER:D_INNER + CONV_DIM]
    wdt = w_in[:, D_INNER + CONV_DIM:]
    dtb = ssm_dt_bias[0].reshape(1, 2 * HEADS)
    z, xbc, dt, dtT = _ssm_in(hcat, mod[0], row(norm_mix_pre[0]), wz, wx, wdt, wdt.T,
                              dtb, dtb.reshape(2 * HEADS, 1))
    xs, bt, cm = _ssm_conv(xbc, ssm_conv_w[0], row(ssm_conv_b[0]))
    a = -jnp.exp(ssm_a_log[0].astype(F32)).reshape(1, 2 * HEADS)
    rexp = (jnp.arange(D_INNER)[None, :] // HEADDIM == jnp.arange(HEADS)[:, None]).astype(BF16)
    yb = _ssd_bwd(xs, cm, bt, dt, dtT, a, a.reshape(2 * HEADS, 1), rexp)
    dsk = jnp.repeat(ssm_d[0], HEADDIM).reshape(1, D_INNER)
    ygn = _ssd_fwd(xs, cm, bt, dt, dtT, a, a.reshape(2 * HEADS, 1), rexp, z, yb, dsk,
                   row(ssm_norm[0]))
    x1, h2, eidx, pos, gate, counts = _ssm_out(
        ygn, ssm_w_out[0].astype(BF16), x, mod[0], row(norm_mix_post[0]), row(norm_ffn_pre[0]),
        router_w[0].T, router_b[0].reshape(E, 1))
    x2 = _moe(x1, h2, eidx, pos, gate, counts, mod[0], row(norm_ffn_post[0]),
              exp_w_gate[0], exp_w_up[0], exp_w_down[0], sh_w_gate[0], sh_w_up[0], sh_w_down[0])

    cw = cv_w_in[0].astype(BF16)
    u = _conf_in(x2, mod[1], row(norm_mix_pre[1]), cw[:, :D], cw[:, D:],
                 row(cv_b_in[0, :D]), row(cv_b_in[0, D:]))
    x3, h4, eidx, pos, gate, counts = _conf_out(
        u, cv_dw_w[0], row(cv_dw_b[0]), row(cv_ln_g[0]), row(cv_ln_b[0]),
        cv_w_out[0].astype(BF16), row(cv_b_out[0]),
        x2, mod[1], row(norm_mix_post[1]), row(norm_ffn_pre[1]),
        router_w[1].T, router_b[1].reshape(E, 1))
    return _moe(x3, h4, eidx, pos, gate, counts, mod[1], row(norm_ffn_post[1]),
                exp_w_gate[1], exp_w_up[1], exp_w_down[1], sh_w_gate[1], sh_w_up[1], sh_w_down[1])
```

```python
import functools

import jax
import jax.numpy as jnp
from jax import lax
from jax.experimental import pallas as pl
from jax.experimental.pallas import tpu as pltpu

F32 = jnp.float32
BF16 = jnp.bfloat16
I32 = jnp.int32

D = 1024
BATCH = 4
SEQ = 4096
CTX = 256
LTOT = CTX + SEQ
GRID_W = 64

D_INNER = 2048
HEADS = 32
GROUPS = 8
HPG = 4
HEADDIM = 64
NSTATE = 128
Q = 128
NCHUNK = LTOT // Q
CTX_CHUNKS = CTX // Q
CONV_DIM = D_INNER + 2 * GROUPS * NSTATE
SSM_K = 5
CONF_K = 31

E = 64
TOPK = 8
NGRP = 8
TOPG = 4
DE = 256
ROUTED_SCALE = 2.5
EPS = 1e-6

T = BATCH * SEQ
TM = 256
BLK = 256
NBLK = -(-(T * TOPK + E * (BLK - 1)) // BLK)
PROWS = (NBLK + 1) * BLK + 8

VMEM_LIMIT = 56 * 1024 * 1024
NEG = -1e30


def _cp(sem):
    return pltpu.CompilerParams(dimension_semantics=sem, vmem_limit_bytes=VMEM_LIMIT)


def _silu(v):
    return v * jax.nn.sigmoid(v)


def _rms(v, g):
    return v * lax.rsqrt(jnp.mean(v * v, axis=-1, keepdims=True) + EPS) * g


def _split3(v):
    a = v.astype(BF16)
    r = v - a.astype(F32)
    b = r.astype(BF16)
    c = (r - b.astype(F32)).astype(BF16)
    return a, b, c


def _dot(a, b):
    return jnp.dot(a, b, preferred_element_type=F32)


def _dot_nt(a, b):
    return lax.dot_general(a, b, (((1,), (1,)), ((), ())), preferred_element_type=F32)


U32 = jnp.uint32
RW = D // 2 // 128
_HI = 0xFFFF0000


def _pack2(lo, hi):
    ul = pltpu.bitcast(lo.astype(BF16).astype(F32), U32)
    uh = pltpu.bitcast(hi.astype(BF16).astype(F32), U32)
    return (ul >> 16) | (uh & U32(_HI))


def _unpack2(w):
    return pltpu.bitcast(w << 16, F32), pltpu.bitcast(w & U32(_HI), F32)


def _store_packed(ref, v):
    m = v.shape[0]
    for s in range(RW):
        lo = v[:, s * 128:(s + 1) * 128]
        hi = v[:, D // 2 + s * 128:D // 2 + (s + 1) * 128]
        ref[pl.ds(s, m, stride=RW), :] = _pack2(lo, hi)


def _load_packed(ref, m, base=0):
    los, his = [], []
    for s in range(RW):
        lo, hi = _unpack2(ref[pl.ds(base + s, m, stride=RW), :])
        los.append(lo.astype(BF16))
        his.append(hi.astype(BF16))
    return jnp.concatenate(los + his, axis=1)


def _ada_kernel(c_ref, w_ref, b_ref, o_ref):
    s = _silu(c_ref[...])
    o_ref[...] = jnp.dot(s, w_ref[...], preferred_element_type=F32,
                         precision=lax.Precision.HIGHEST) + b_ref[...]


def _ada(cvec, ada_w, ada_b):
    depth = ada_w.shape[0]
    tn = 1536
    return pl.pallas_call(
        _ada_kernel,
        grid=(depth, 6 * D // tn),
        in_specs=[pl.BlockSpec((8, D), lambda l, j: (0, 0)),
                  pl.BlockSpec((None, D, tn), lambda l, j: (l, 0, j)),
                  pl.BlockSpec((None, 1, tn), lambda l, j: (l, 0, j))],
        out_specs=pl.BlockSpec((None, 8, tn), lambda l, j: (l, 0, j)),
        out_shape=jax.ShapeDtypeStruct((depth, 8, 6 * D), F32),
        compiler_params=_cp(("arbitrary", "arbitrary")),
        name="ada",
    )(cvec, ada_w, ada_b.reshape(depth, 1, 6 * D))


def _ssm_in_kernel(x_ref, mod_ref, g_ref, wz_ref, wx0_ref, wx1_ref, wdt_ref, wdtT_ref, dtb_ref, dtbT_ref,
                   z_ref, xbc_ref, dt_ref, dtT_ref):
    h = _rms(x_ref[...], g_ref[...]) * (1.0 + mod_ref[1:2, :]) + mod_ref[0:1, :]
    hb = h.astype(BF16)
    z_ref[...] = _dot(hb, wz_ref[...]).astype(BF16)
    xbc_ref[:, :D_INNER] = _dot(hb, wx0_ref[...]).astype(BF16)
    xbc_ref[:, D_INNER:] = _dot(hb, wx1_ref[...]).astype(BF16)
    dt_ref[...] = jax.nn.softplus(_dot(hb, wdt_ref[...]) + dtb_ref[...])
    dtT_ref[...] = jax.nn.softplus(_dot_nt(wdtT_ref[...], hb) + dtbT_ref[...])


def _ssm_in(hcat, mod, gain, w_in, wdt, wdtT, dtb, dtbT):
    nt = LTOT // TM
    full = lambda b, i: (0, 0)
    assert CONV_DIM == 2 * D_INNER
    return pl.pallas_call(
        _ssm_in_kernel,
        grid=(BATCH, nt),
        in_specs=[pl.BlockSpec((None, TM, D), lambda b, i: (b, i, 0)),
                  pl.BlockSpec((None, 6, D), lambda b, i: (jnp.where(i == 0, BATCH, b), 0, 0)),
                  pl.BlockSpec((1, D), full),
                  pl.BlockSpec((D, D_INNER), lambda b, i: (0, 0)),
                  pl.BlockSpec((D, D_INNER), lambda b, i: (0, 1)),
                  pl.BlockSpec((D, D_INNER), lambda b, i: (0, 2)),
                  pl.BlockSpec((D, 2 * HEADS), full),
                  pl.BlockSpec((2 * HEADS, D), full),
                  pl.BlockSpec((1, 2 * HEADS), full),
                  pl.BlockSpec((2 * HEADS, 1), full)],
        out_specs=[pl.BlockSpec((None, TM, D_INNER), lambda b, i: (b, i, 0)),
                   pl.BlockSpec((None, TM, CONV_DIM), lambda b, i: (b, i, 0)),
                   pl.BlockSpec((None, TM, 2 * HEADS), lambda b, i: (b, i, 0)),
                   pl.BlockSpec((None, 2 * HEADS, TM), lambda b, i: (b, 0, i))],
        out_shape=[jax.ShapeDtypeStruct((BATCH, LTOT, D_INNER), BF16),
                   jax.ShapeDtypeStruct((BATCH, LTOT, CONV_DIM), BF16),
                   jax.ShapeDtypeStruct((BATCH, LTOT, 2 * HEADS), F32),
                   jax.ShapeDtypeStruct((BATCH, 2 * HEADS, LTOT), F32)],
        compiler_params=_cp(("arbitrary", "arbitrary")),
        name="ssm_in",
    )(hcat, mod, gain, w_in, w_in, w_in, wdt, wdtT, dtb, dtbT)


_HALO = 16
_CC = 512


def _ssm_conv_kernel(x_ref, xp_ref, xn_ref, w_ref, b_ref, xs_ref, bt_ref, cm_ref, ext_ref):
    i = pl.program_id(1)
    first_lat = CTX // Q
    pv = jnp.where((i == 0) | (i == first_lat), 0.0, 1.0)
    nv = jnp.where((i == first_lat - 1) | (i == NCHUNK - 1), 0.0, 1.0)
    for c in range(CONV_DIM // _CC):
        cs = slice(c * _CC, (c + 1) * _CC)
        ext_ref[0:8, :] = xp_ref[:, cs].astype(F32)[8:16, :] * pv
        ext_ref[8:8 + Q, :] = x_ref[:, cs].astype(F32)
        ext_ref[8 + Q:16 + Q, :] = xn_ref[:, cs].astype(F32)[0:8, :] * nv
        acc = jnp.broadcast_to(b_ref[:, cs], (Q, _CC))
        for k in range(SSM_K):
            acc = acc + w_ref[k:k + 1, cs] * ext_ref[6 + k:6 + k + Q, :]
        y = _silu(acc)
        lo = c * _CC
        if lo < D_INNER:
            xs_ref[:, cs] = y.astype(BF16)
        elif lo < D_INNER + GROUPS * NSTATE:
            o = lo - D_INNER
            bt_ref[o:o + _CC, :] = y.T.astype(BF16)
        else:
            o = lo - D_INNER - GROUPS * NSTATE
            cm_ref[:, o:o + _CC] = y.astype(BF16)


def _ssm_conv(xbc, w, b):
    nh = Q // _HALO
    last = LTOT // _HALO - 1
    gn = GROUPS * NSTATE
    return pl.pallas_call(
        _ssm_conv_kernel,
        grid=(BATCH, NCHUNK),
        in_specs=[pl.BlockSpec((None, Q, CONV_DIM), lambda b, i: (b, i, 0)),
                  pl.BlockSpec((None, _HALO, CONV_DIM), lambda b, i: (b, jnp.maximum(i * nh - 1, 0), 0)),
                  pl.BlockSpec((None, _HALO, CONV_DIM), lambda b, i: (b, jnp.minimum(i * nh + nh, last), 0)),
                  pl.BlockSpec((SSM_K, CONV_DIM), lambda b, i: (0, 0)),
                  pl.BlockSpec((1, CONV_DIM), lambda b, i: (0, 0))],
        out_specs=[pl.BlockSpec((None, Q, D_INNER), lambda b, i: (b, i, 0)),
                   pl.BlockSpec((None, gn, Q), lambda b, i: (b, 0, i)),
                   pl.BlockSpec((None, Q, gn), lambda b, i: (b, i, 0))],
        out_shape=[jax.ShapeDtypeStruct((BATCH, LTOT, D_INNER), BF16),
                   jax.ShapeDtypeStruct((BATCH, gn, LTOT), BF16),
                   jax.ShapeDtypeStruct((BATCH, LTOT, gn), BF16)],
        scratch_shapes=[pltpu.VMEM((Q + 16, _CC), F32)],
        compiler_params=_cp(("arbitrary", "arbitrary")),
        name="ssm_conv",
    )(xbc, xbc, xbc, w, b)


def _ssd_chunk(direction, xs_ref, cm_ref, bt_ref, dt_ref, dtT_ref, arow_ref, acol_ref, rexp_ref, s_ref):
    d0 = direction * HEADS
    dtc = dt_ref[:, d0:d0 + HEADS]
    dtr = dtT_ref[d0:d0 + HEADS, :]
    da_c = dtc * arow_ref[:, d0:d0 + HEADS]
    da_r = dtr * acol_ref[d0:d0 + HEADS, :]
    ii = lax.broadcasted_iota(I32, (Q, Q), 0)
    jj = lax.broadcasted_iota(I32, (Q, Q), 1)
    if direction == 0:
        lower = jj <= ii
        tot_idx = Q - 1
    else:
        lower = jj >= ii
        tot_idx = 0
    tri_c = jnp.where(lower, 1.0, 0.0).astype(BF16)
    upper = (ii <= jj) if direction == 0 else (ii >= jj)
    tri_r = jnp.where(upper, 1.0, 0.0).astype(BF16)
    c1, c2, c3 = _split3(da_c)
    cum_c = _dot(tri_c, c1) + _dot(tri_c, c2) + _dot(tri_c, c3)
    r1, r2, r3 = _split3(da_r)
    cum_r = _dot(r1, tri_r) + _dot(r2, tri_r) + _dot(r3, tri_r)
    tot_c = cum_c[tot_idx:tot_idx + 1, :]
    tot_r = cum_r[:, tot_idx:tot_idx + 1]
    rfac = dtr * jnp.exp(tot_r - cum_r)
    dec = jnp.exp(tot_c)
    dh = dec.astype(BF16)
    dl = (dec - dh.astype(F32)).astype(BF16)
    dec_x = (_dot(jnp.broadcast_to(dh, (8, HEADS)), rexp_ref[...])
             + _dot(jnp.broadcast_to(dl, (8, HEADS)), rexp_ref[...]))[0:1, :]
    lane = lax.broadcasted_iota(I32, (Q, HPG * HEADDIM), 1)
    ys = []
    for g in range(GROUPS):
        cg = cm_ref[:, g * NSTATE:(g + 1) * NSTATE]
        btg = bt_ref[g * NSTATE:(g + 1) * NSTATE, :]
        xg = xs_ref[:, g * 256:(g + 1) * 256]
        sg = s_ref[g]
        cb = _dot(cg, btg)
        rhs = jnp.concatenate([xg, sg.astype(BF16)], axis=0)
        cg32 = cg.astype(F32)
        btg32 = btg.astype(F32)
        yg = None
        ug = None
        for r in range(HPG):
            h = g * HPG + r
            colb = cum_c[:, h:h + 1]
            rowb = cum_r[h:h + 1, :]
            decay = jnp.exp(jnp.where(lower, colb - rowb, NEG))
            m = (cb * decay * dtr[h:h + 1, :]).astype(BF16)
            cs = (cg32 * jnp.exp(colb)).astype(BF16)
            res = _dot(jnp.concatenate([m, cs], axis=1), rhs)
            bw = (btg32 * rfac[h:h + 1, :]).astype(BF16)
            upd = _dot(bw, xg)
            if r == 0:
                yg, ug = res, upd
            else:
                sel = lane >= r * HEADDIM
                yg = jnp.where(sel, res, yg)
                ug = jnp.where(sel, upd, ug)
        s_ref[g] = sg * dec_x[:, g * 256:(g + 1) * 256] + ug
        ys.append(yg)
    return ys


def _ssd_bwd_kernel(xs_ref, cm_ref, bt_ref, dt_ref, dtT_ref, arow_ref, acol_ref, rexp_ref,
                    y_ref, s_ref):
    @pl.when(pl.program_id(1) == 0)
    def _():
        s_ref[...] = jnp.zeros_like(s_ref)
    ys = _ssd_chunk(1, xs_ref, cm_ref, bt_ref, dt_ref, dtT_ref, arow_ref, acol_ref, rexp_ref, s_ref)
    for g in range(GROUPS):
        y_ref[:, g * 256:(g + 1) * 256] = ys[g].astype(BF16)


def _ssd_fwd_kernel(xs_ref, cm_ref, bt_ref, dt_ref, dtT_ref, arow_ref, acol_ref, rexp_ref,
                    z_ref, yb_ref, dsk_ref, nw_ref, y_ref, s_ref):
    @pl.when(pl.program_id(1) == 0)
    def _():
        s_ref[...] = jnp.zeros_like(s_ref)
    ys = _ssd_chunk(0, xs_ref, cm_ref, bt_ref, dt_ref, dtT_ref, arow_ref, acol_ref, rexp_ref, s_ref)
    for g in range(GROUPS):
        gs = slice(g * 256, (g + 1) * 256)
        y = ys[g] + yb_ref[:, gs].astype(F32) + xs_ref[:, gs].astype(F32) * dsk_ref[:, gs]
        y = y * _silu(z_ref[:, gs].astype(F32))
        y = y * lax.rsqrt(jnp.mean(y * y, axis=-1, keepdims=True) + EPS) * nw_ref[:, gs]
        y_ref[:, gs] = y.astype(BF16)


def _ssd_specs(cmap):
    gn = GROUPS * NSTATE
    full = lambda b, j: (0, 0)
    return [pl.BlockSpec((None, Q, D_INNER), lambda b, j: (b, cmap(j), 0)),
            pl.BlockSpec((None, Q, gn), lambda b, j: (b, cmap(j), 0)),
            pl.BlockSpec((None, gn, Q), lambda b, j: (b, 0, cmap(j))),
            pl.BlockSpec((None, Q, 2 * HEADS), lambda b, j: (b, cmap(j), 0)),
            pl.BlockSpec((None, 2 * HEADS, Q), lambda b, j: (b, 0, cmap(j))),
            pl.BlockSpec((1, 2 * HEADS), full),
            pl.BlockSpec((2 * HEADS, 1), full),
            pl.BlockSpec((HEADS, D_INNER), full)]


def _ssd_bwd(xs, cm, bt, dt, dtT, arow, acol, rexp):
    cmap = lambda j: jnp.where(j < CTX_CHUNKS, CTX_CHUNKS - 1 - j, NCHUNK + CTX_CHUNKS - 1 - j)
    omap = lambda b, j: (b, NCHUNK - 1 - jnp.maximum(j, CTX_CHUNKS), 0)
    return pl.pallas_call(
        _ssd_bwd_kernel,
        grid=(BATCH, NCHUNK),
        in_specs=_ssd_specs(cmap),
        out_specs=pl.BlockSpec((None, Q, D_INNER), omap),
        out_shape=jax.ShapeDtypeStruct((BATCH, SEQ, D_INNER), BF16),
        scratch_shapes=[pltpu.VMEM((GROUPS, NSTATE, HPG * HEADDIM), F32)],
        compiler_params=_cp(("arbitrary", "arbitrary")),
        name="ssd_bwd",
    )(xs, cm, bt, dt, dtT, arow, acol, rexp)


def _ssd_fwd(xs, cm, bt, dt, dtT, arow, acol, rexp, z, yb, dsk, nw):
    cmap = lambda j: j
    lat = lambda b, j: (b, jnp.maximum(j - CTX_CHUNKS, 0), 0)
    full = lambda b, j: (0, 0)
    return pl.pallas_call(
        _ssd_fwd_kernel,
        grid=(BATCH, NCHUNK),
        in_specs=_ssd_specs(cmap) + [
            pl.BlockSpec((None, Q, D_INNER), lambda b, j: (b, j, 0)),
            pl.BlockSpec((None, Q, D_INNER), lat),
            pl.BlockSpec((1, D_INNER), full),
            pl.BlockSpec((1, D_INNER), full)],
        out_specs=pl.BlockSpec((None, Q, D_INNER), lat),
        out_shape=jax.ShapeDtypeStruct((BATCH, SEQ, D_INNER), BF16),
        scratch_shapes=[pltpu.VMEM((GROUPS, NSTATE, HPG * HEADDIM), F32)],
        compiler_params=_cp(("arbitrary", "arbitrary")),
        name="ssd_fwd",
    )(xs, cm, bt, dt, dtT, arow, acol, rexp, z, yb, dsk, nw)


def _route(h, rwT_ref, rb_ref, cnt_ref, eidx_ref, pos_ref, gate_ref):
    hh = h.astype(BF16)
    hl = (h - hh.astype(F32)).astype(BF16)
    w = rwT_ref[...]
    wh = w.astype(BF16)
    wl = (w - wh.astype(F32)).astype(BF16)
    logits = _dot_nt(wh, hh) + _dot_nt(wh, hl) + _dot_nt(wl, hh)
    scores = jax.nn.sigmoid(logits)
    sel = scores + rb_ref[...]
    per = E // NGRP
    sub = lax.broadcasted_iota(I32, (per, TM), 0)
    gscore = []
    for g in range(NGRP):
        blk = sel[g * per:(g + 1) * per, :]
        m1 = jnp.max(blk, axis=0, keepdims=True)
        first = jnp.min(jnp.where(blk == m1, sub, per), axis=0, keepdims=True)
        m2 = jnp.max(jnp.where(sub == first, -jnp.inf, blk), axis=0, keepdims=True)
        gscore.append(m1 + m2)
    masked = []
    for g in range(NGRP):
        rank = jnp.zeros((1, TM), F32)
        for o in range(NGRP):
            if o == g:
                continue
            ahead = (gscore[o] >= gscore[g]) if o < g else (gscore[o] > gscore[g])
            rank = rank + jnp.where(ahead, 1.0, 0.0)
        blk = sel[g * per:(g + 1) * per, :]
        masked.append(jnp.where(rank < TOPG, blk, -jnp.inf))
    v = jnp.concatenate(masked, axis=0)
    eio = lax.broadcasted_iota(I32, (E, TM), 0)
    kio = lax.broadcasted_iota(I32, (TOPK, TM), 0)
    eidx = jnp.zeros((TOPK, TM), I32)
    gsc = jnp.zeros((TOPK, TM), F32)
    hot = jnp.zeros((E, TM), F32)
    picks = []
    for k in range(TOPK):
        m = jnp.max(v, axis=0, keepdims=True)
        first = jnp.min(jnp.where(v == m, eio, E), axis=0, keepdims=True)
        pick = eio == first
        sc = jnp.sum(jnp.where(pick, scores, 0.0), axis=0, keepdims=True)
        eidx = jnp.where(kio == k, first, eidx)
        gsc = jnp.where(kio == k, sc, gsc)
        hot = jnp.where(pick, 1.0, hot)
        v = jnp.where(pick, -jnp.inf, v)
        picks.append(pick)
    gate = gsc / jnp.sum(gsc, axis=0, keepdims=True) * ROUTED_SCALE
    ti = lax.broadcasted_iota(I32, (TM, TM), 0)
    tj = lax.broadcasted_iota(I32, (TM, TM), 1)
    before = jnp.where(ti < tj, 1.0, 0.0).astype(BF16)
    posfull = _dot(hot.astype(BF16), before) + cnt_ref[...]
    pos = jnp.zeros((TOPK, TM), F32)
    for k in range(TOPK):
        pk = jnp.sum(jnp.where(picks[k], posfull, 0.0), axis=0, keepdims=True)
        pos = jnp.where(kio == k, pk, pos)
    cnt_ref[...] = cnt_ref[...] + jnp.sum(hot, axis=1, keepdims=True)
    eidx_ref[...] = eidx
    pos_ref[...] = pos.astype(I32)
    eye = jnp.where(ti == tj, 1.0, 0.0).astype(BF16)
    g1, g2, g3 = _split3(gate)
    gate_ref[...] = _dot_nt(eye, g1) + _dot_nt(eye, g2) + _dot_nt(eye, g3)


def _mix_epilogue(y, x_ref, mod_ref, gpost_ref, gpre_ref, rwT_ref, rb_ref,
                  x1_ref, h2_ref, eidx_ref, pos_ref, gate_ref, cnt_out_ref, cnt_ref):
    first = (pl.program_id(0) == 0) & (pl.program_id(1) == 0)

    @pl.when(first)
    def _():
        cnt_ref[...] = jnp.zeros_like(cnt_ref)
    x1 = x_ref[...] + mod_ref[2:3, :] * _rms(y, gpost_ref[...])
    x1_ref[...] = x1
    h2 = _rms(x1, gpre_ref[...]) * (1.0 + mod_ref[4:5, :]) + mod_ref[3:4, :]
    _store_packed(h2_ref, h2)
    _route(h2, rwT_ref, rb_ref, cnt_ref, eidx_ref, pos_ref, gate_ref)
    cnt_out_ref[...] = cnt_ref[...]


def _ssm_out_kernel(y_ref, w_ref, x_ref, mod_ref, gpost_ref, gpre_ref, rwT_ref, rb_ref,
                    x1_ref, h2_ref, eidx_ref, pos_ref, gate_ref, cnt_out_ref, cnt_ref):
    y = _dot(y_ref[...], w_ref[...])
    _mix_epilogue(y, x_ref, mod_ref, gpost_ref, gpre_ref, rwT_ref, rb_ref,
                  x1_ref, h2_ref, eidx_ref, pos_ref, gate_ref, cnt_out_ref, cnt_ref)


_CW = 256
_PAD = 16


def _conf_out_kernel(u_ref, dww_ref, dwb_ref, lng_ref, lnb_ref, w_ref, b_ref,
                     x_ref, mod_ref, gpost_ref, gpre_ref, rwT_ref, rb_ref,
                     x1_ref, h2_ref, eidx_ref, pos_ref, gate_ref, cnt_out_ref,
                     cnt_ref, ext_ref, v_ref):
    nrow = TM // GRID_W
    zpad = jnp.zeros((_PAD, D), F32)
    for r in range(nrow):
        base = r * (GRID_W + 2 * _PAD)
        ext_ref[base:base + _PAD, :] = zpad
        ext_ref[base + _PAD:base + _PAD + GRID_W, :] = u_ref[r * GRID_W:(r + 1) * GRID_W, :].astype(F32)
        ext_ref[base + _PAD + GRID_W:base + 2 * _PAD + GRID_W, :] = zpad
    for r in range(nrow):
        base = r * (GRID_W + 2 * _PAD)
        for c in range(D // _CW):
            cs = slice(c * _CW, (c + 1) * _CW)
            acc = jnp.broadcast_to(dwb_ref[:, cs], (GRID_W, _CW))
            for k in range(CONF_K):
                o = base + _PAD - CONF_K // 2 + k
                acc = acc + dww_ref[k:k + 1, cs] * ext_ref[o:o + GRID_W, cs]
            v_ref[r * GRID_W:(r + 1) * GRID_W, cs] = acc
    v = v_ref[...]
    mu = jnp.mean(v, axis=-1, keepdims=True)
    vc = v - mu
    ln = vc * lax.rsqrt(jnp.mean(vc * vc, axis=-1, keepdims=True) + EPS) * lng_ref[...] + lnb_ref[...]
    y = _dot(_silu(ln).astype(BF16), w_ref[...]) + b_ref[...]
    _mix_epilogue(y, x_ref, mod_ref, gpost_ref, gpre_ref, rwT_ref, rb_ref,
                  x1_ref, h2_ref, eidx_ref, pos_ref, gate_ref, cnt_out_ref, cnt_ref)


def _mix_out_common_specs():
    full = lambda b, i: (0, 0)
    nt = SEQ // TM
    in_specs = [pl.BlockSpec((None, TM, D), lambda b, i: (b, i, 0)),
                pl.BlockSpec((None, 6, D), lambda b, i: (b, 0, 0)),
                pl.BlockSpec((1, D), full), pl.BlockSpec((1, D), full),
                pl.BlockSpec((E, D), full), pl.BlockSpec((E, 1), full)]
    out_specs = [pl.BlockSpec((None, TM, D), lambda b, i: (b, i, 0)),
                 pl.BlockSpec((TM * RW, 128), lambda b, i: (b * nt + i, 0)),
                 pl.BlockSpec((TOPK, TM), lambda b, i: (0, b * nt + i)),
                 pl.BlockSpec((TOPK, TM), lambda b, i: (0, b * nt + i)),
                 pl.BlockSpec((TM, TOPK), lambda b, i: (b * nt + i, 0)),
                 pl.BlockSpec((E, 1), full)]
    out_shape = [jax.ShapeDtypeStruct((BATCH, SEQ, D), F32),
                 jax.ShapeDtypeStruct((T * RW, 128), U32),
                 jax.ShapeDtypeStruct((TOPK, T), I32),
                 jax.ShapeDtypeStruct((TOPK, T), I32),
                 jax.ShapeDtypeStruct((T, TOPK), F32),
                 jax.ShapeDtypeStruct((E, 1), F32)]
    return in_specs, out_specs, out_shape


def _ssm_out(y, w, x, mod, gpost, gpre, rwT, rb):
    common_in, out_specs, out_shape = _mix_out_common_specs()
    return pl.pallas_call(
        _ssm_out_kernel,
        grid=(BATCH, SEQ // TM),
        in_specs=[pl.BlockSpec((None, TM, D_INNER), lambda b, i: (b, i, 0)),
                  pl.BlockSpec((D_INNER, D), lambda b, i: (0, 0))] + common_in,
        out_specs=out_specs, out_shape=out_shape,
        scratch_shapes=[pltpu.VMEM((E, 1), F32)],
        compiler_params=_cp(("arbitrary", "arbitrary")),
        name="ssm_out",
    )(y, w, x, mod, gpost, gpre, rwT, rb)


def _conf_out(u, dww, dwb, lng, lnb, w, b, x, mod, gpost, gpre, rwT, rb):
    common_in, out_specs, out_shape = _mix_out_common_specs()
    full = lambda b_, i: (0, 0)
    nrow = TM // GRID_W
    return pl.pallas_call(
        _conf_out_kernel,
        grid=(BATCH, SEQ // TM),
        in_specs=[pl.BlockSpec((None, TM, D), lambda b_, i: (b_, i, 0)),
                  pl.BlockSpec((CONF_K, D), full), pl.BlockSpec((1, D), full),
                  pl.BlockSpec((1, D), full), pl.BlockSpec((1, D), full),
                  pl.BlockSpec((D, D), full), pl.BlockSpec((1, D), full)] + common_in,
        out_specs=out_specs, out_shape=out_shape,
        scratch_shapes=[pltpu.VMEM((E, 1), F32),
                        pltpu.VMEM((nrow * (GRID_W + 2 * _PAD), D), F32),
                        pltpu.VMEM((TM, D), F32)],
        compiler_params=_cp(("arbitrary", "arbitrary")),
        name="conf_out",
    )(u, dww, dwb, lng, lnb, w, b, x, mod, gpost, gpre, rwT, rb)


def _conf_in_kernel(x_ref, mod_ref, g_ref, wa_ref, wg_ref, ba_ref, bg_ref, u_ref):
    h = _rms(x_ref[...], g_ref[...]) * (1.0 + mod_ref[1:2, :]) + mod_ref[0:1, :]
    hb = h.astype(BF16)
    a = _dot(hb, wa_ref[...]) + ba_ref[...]
    g = _dot(hb, wg_ref[...]) + bg_ref[...]
    u_ref[...] = (a * jax.nn.sigmoid(g)).astype(BF16)


def _conf_in(x, mod, gain, w, bias):
    full = lambda b, i: (0, 0)
    return pl.pallas_call(
        _conf_in_kernel,
        grid=(BATCH, SEQ // TM),
        in_specs=[pl.BlockSpec((None, TM, D), lambda b, i: (b, i, 0)),
                  pl.BlockSpec((None, 6, D), lambda b, i: (b, 0, 0)),
                  pl.BlockSpec((1, D), full),
                  pl.BlockSpec((D, D), full), pl.BlockSpec((D, D), lambda b, i: (0, 1)),
                  pl.BlockSpec((1, D), full), pl.BlockSpec((1, D), lambda b, i: (0, 1))],
        out_specs=pl.BlockSpec((None, TM, D), lambda b, i: (b, i, 0)),
        out_shape=jax.ShapeDtypeStruct((BATCH, SEQ, D), BF16),
        compiler_params=_cp(("arbitrary", "arbitrary")),
        name="conf_in",
    )(x, mod, gain, w, w, bias, bias)


def _row_copy(src, si, dst, di, sem):
    return pltpu.make_async_copy(src.at[pl.ds(pl.multiple_of(si * RW, RW), RW), :],
                                 dst.at[pl.ds(pl.multiple_of(di * RW, RW), RW), :], sem)


ZROWS = BLK + 8


def _dispatch_kernel(zs_ref, dest_ref, h_ref, xs_ref, zero_ref, zsem, sems):
    i = pl.program_id(0)
    n = pl.num_programs(0)

    @pl.when(i == 0)
    def _():
        zero_ref[...] = jnp.zeros_like(zero_ref)

        def zcopy(e):
            start = pl.multiple_of(zs_ref[e] * RW, 8 * RW)
            return pltpu.make_async_copy(zero_ref, xs_ref.at[pl.ds(start, ZROWS * RW), :], zsem)

        def zstart(e, c):
            zcopy(e).start()
            return c

        def zwait(e, c):
            zcopy(e).wait()
            return c
        lax.fori_loop(0, E, zstart, 0)
        lax.fori_loop(0, E, zwait, 0)

    def issue(t, c):
        for k in range(TOPK):
            _row_copy(h_ref, i * TM + t, xs_ref, dest_ref[t * TOPK + k],
                      sems.at[i % 2]).start(priority=k % 2)
        return c

    def drain(slot):
        def body(t, c):
            for k in range(TOPK):
                _row_copy(h_ref, 0, xs_ref, 0, sems.at[slot]).wait()
            return c
        lax.fori_loop(0, TM, body, 0)

    lax.fori_loop(0, TM, issue, 0)

    @pl.when(i > 0)
    def _():
        drain((i + 1) % 2)

    @pl.when(i == n - 1)
    def _():
        drain(i % 2)


def _dispatch(zstart, dest, hp):
    grid_spec = pltpu.PrefetchScalarGridSpec(
        num_scalar_prefetch=1,
        grid=(T // TM,),
        in_specs=[pl.BlockSpec((TM * TOPK,), lambda i, zs: (i,), memory_space=pltpu.SMEM),
                  pl.BlockSpec(memory_space=pl.ANY)],
        out_specs=pl.BlockSpec(memory_space=pl.ANY),
        scratch_shapes=[pltpu.VMEM((ZROWS * RW, 128), U32), pltpu.SemaphoreType.DMA(()),
                        pltpu.SemaphoreType.DMA((2,))])
    return pl.pallas_call(
        _dispatch_kernel,
        grid_spec=grid_spec,
        out_shape=jax.ShapeDtypeStruct((PROWS * RW, 128), U32),
        compiler_params=_cp(("arbitrary",)),
        name="moe_dispatch",
    )(zstart, dest, hp)


def _experts_kernel(be_ref, na_ref, x_ref, wg_ref, wu_ref, wd_ref, y_ref):
    i = pl.program_id(0)

    @pl.when(i < na_ref[0])
    def _():
        xb = _load_packed(x_ref, BLK)
        g = _dot(xb, wg_ref[...].astype(BF16))
        u = _dot(xb, wu_ref[...].astype(BF16))
        a = (_silu(g) * u).astype(BF16)
        _store_packed(y_ref, _dot(a, wd_ref[...].astype(BF16)))

    @pl.when(i >= na_ref[0])
    def _():
        y_ref[...] = jnp.zeros_like(y_ref)


def _experts(block_e, nact, xs, wg, wu, wd):
    xmap = lambda i, be, na: (jnp.minimum(i, na[0] - 1), 0)
    wmap = lambda i, be, na: (be[i], 0, 0)
    grid_spec = pltpu.PrefetchScalarGridSpec(
        num_scalar_prefetch=2,
        grid=(NBLK,),
        in_specs=[pl.BlockSpec((BLK * RW, 128), xmap),
                  pl.BlockSpec((None, D, DE), wmap),
                  pl.BlockSpec((None, D, DE), wmap),
                  pl.BlockSpec((None, DE, D), wmap)],
        out_specs=pl.BlockSpec((BLK * RW, 128), lambda i, be, na: (i, 0)))
    return pl.pallas_call(
        _experts_kernel,
        grid_spec=grid_spec,
        out_shape=jax.ShapeDtypeStruct((NBLK * BLK * RW, 128), U32),
        compiler_params=_cp(("arbitrary",)),
        name="moe_experts",
    )(block_e, nact, xs, wg, wu, wd)


def _combine_kernel(dcur_ref, dnxt_ref, gate_ref, h_ref, x1_ref, mod_ref, gpost_ref,
                    sg_ref, su_ref, sd_ref, ys_ref, o_ref, buf_ref, sems):
    i = pl.program_id(0)
    n = pl.num_programs(0)

    def gather(dest_ref, slot):
        def body(t, c):
            for k in range(TOPK):
                _row_copy(ys_ref, dest_ref[t * TOPK + k], buf_ref.at[slot, k], t,
                          sems.at[slot]).start(priority=k % 2)
            return c
        lax.fori_loop(0, TM, body, 0)

    @pl.when(i == 0)
    def _():
        gather(dcur_ref, 0)

    @pl.when(i + 1 < n)
    def _():
        gather(dnxt_ref, (i + 1) % 2)

    hb = _load_packed(h_ref, TM)
    a = (_silu(_dot(hb, sg_ref[...])) * _dot(hb, su_ref[...])).astype(BF16)
    f = _dot(a, sd_ref[...])

    slot = i % 2

    def drain(t, c):
        for k in range(TOPK):
            _row_copy(ys_ref, 0, buf_ref.at[slot, k], 0, sems.at[slot]).wait()
        return c
    lax.fori_loop(0, TM, drain, 0)

    gate = gate_ref[...]
    gb = [jnp.broadcast_to(gate[:, k:k + 1], (TM, 128)) for k in range(TOPK)]
    los, his = [], []
    for s in range(RW):
        lo = f[:, s * 128:(s + 1) * 128]
        hi = f[:, D // 2 + s * 128:D // 2 + (s + 1) * 128]
        for k in range(TOPK):
            wl, wh = _unpack2(buf_ref[slot, k, pl.ds(s, TM, stride=RW), :])
            lo = lo + gb[k] * wl
            hi = hi + gb[k] * wh
        los.append(lo)
        his.append(hi)
    f = jnp.concatenate(los + his, axis=1)
    o_ref[...] = x1_ref[...] + mod_ref[5:6, :] * _rms(f, gpost_ref[...])


def _combine(dest, gate, hp, x1, mod, gpost, sg, su, sd, ys):
    nt = SEQ // TM
    n = T // TM
    full = lambda i: (0, 0)
    return pl.pallas_call(
        _combine_kernel,
        grid=(n,),
        in_specs=[pl.BlockSpec((TM * TOPK,), lambda i: (i,), memory_space=pltpu.SMEM),
                  pl.BlockSpec((TM * TOPK,), lambda i: (jnp.minimum(i + 1, n - 1),),
                               memory_space=pltpu.SMEM),
                  pl.BlockSpec((TM, TOPK), lambda i: (i, 0)),
                  pl.BlockSpec((TM * RW, 128), lambda i: (i, 0)),
                  pl.BlockSpec((TM, D), lambda i: (i, 0)),
                  pl.BlockSpec((None, 6, D), lambda i: (i // nt, 0, 0)),
                  pl.BlockSpec((1, D), full),
                  pl.BlockSpec((D, DE), full), pl.BlockSpec((D, DE), full), pl.BlockSpec((DE, D), full),
                  pl.BlockSpec(memory_space=pl.ANY)],
        out_specs=pl.BlockSpec((TM, D), lambda i: (i, 0)),
        out_shape=jax.ShapeDtypeStruct((T, D), F32),
        scratch_shapes=[pltpu.VMEM((2, TOPK, TM * RW, 128), U32), pltpu.SemaphoreType.DMA((2,))],
        compiler_params=_cp(("arbitrary",)),
        name="moe_combine",
    )(dest, dest, gate, hp, x1.reshape(T, D), mod, gpost, sg, su, sd, ys).reshape(BATCH, SEQ, D)


def _moe(x1, hp, eidx, pos, gate, counts, mod, gpost, wg, wu, wd, sg, su, sd):
    cnt = counts.reshape(E).astype(I32)
    padded = (cnt + BLK - 1) // BLK * BLK
    pend = jnp.cumsum(padded)
    pstart = pend - padded
    ids = jnp.arange(E, dtype=I32)
    dest = pos + jnp.sum(jnp.where(eidx[:, :, None] == ids, pstart, 0), axis=-1)
    dest = dest.T.reshape(T * TOPK)
    nact = (pend[-1] // BLK).reshape(1).astype(I32)
    blk0 = jnp.arange(NBLK, dtype=I32) * BLK
    block_e = jnp.sum((pend[None, :] <= blk0[:, None]).astype(I32), axis=1)
    block_e = jnp.minimum(block_e, E - 1).astype(I32)
    xs = _dispatch(((pstart + cnt) // 8 * 8).astype(I32), dest, hp)
    ys = _experts(block_e, nact, xs, wg, wu, wd)
    return _combine(dest, gate, hp, x1, mod, gpost, sg.astype(BF16), su.astype(BF16),
                    sd.astype(BF16), ys)


def kernel(x, c, ctx, c_ctx, ada_w, ada_b, norm_mix_pre, norm_mix_post, norm_ffn_pre, norm_ffn_post, ssm_w_in, ssm_conv_w, ssm_conv_b, ssm_dt_bias, ssm_a_log, ssm_d, ssm_norm, ssm_w_out, cv_w_in, cv_b_in, cv_dw_w, cv_dw_b, cv_ln_g, cv_ln_b, cv_w_out, cv_b_out, router_w, router_b, exp_w_gate, exp_w_up, exp_w_down, sh_w_gate, sh_w_up, sh_w_down):
    row = lambda v: v.reshape(1, -1)
    cvec = jnp.concatenate([c, c_ctx[None, :], jnp.zeros((3, D), F32)], axis=0)
    mod = _ada(cvec, ada_w, ada_b).reshape(2, 8, 6, D)

    hcat = jnp.concatenate([ctx, x], axis=1)
    w_in = ssm_w_in[0].astype(BF16)
    dtb = ssm_dt_bias[0].reshape(1, 2 * HEADS)
    wdt = w_in[:, D_INNER + CONV_DIM:]
    z, xbc, dt, dtT = _ssm_in(hcat, mod[0], row(norm_mix_pre[0]), w_in, wdt, wdt.T,
                              dtb, dtb.reshape(2 * HEADS, 1))
    xs, bt, cm = _ssm_conv(xbc, ssm_conv_w[0], row(ssm_conv_b[0]))
    a = -jnp.exp(ssm_a_log[0].astype(F32)).reshape(1, 2 * HEADS)
    rexp = (jnp.arange(D_INNER)[None, :] // HEADDIM == jnp.arange(HEADS)[:, None]).astype(BF16)
    yb = _ssd_bwd(xs, cm, bt, dt, dtT, a, a.reshape(2 * HEADS, 1), rexp)
    dsk = jnp.repeat(ssm_d[0], HEADDIM).reshape(1, D_INNER)
    ygn = _ssd_fwd(xs, cm, bt, dt, dtT, a, a.reshape(2 * HEADS, 1), rexp, z, yb, dsk,
                   row(ssm_norm[0]))
    x1, h2, eidx, pos, gate, counts = _ssm_out(
        ygn, ssm_w_out[0].astype(BF16), x, mod[0], row(norm_mix_post[0]), row(norm_ffn_pre[0]),
        router_w[0].T, router_b[0].reshape(E, 1))
    x2 = _moe(x1, h2, eidx, pos, gate, counts, mod[0], row(norm_ffn_post[0]),
              exp_w_gate[0], exp_w_up[0], exp_w_down[0], sh_w_gate[0], sh_w_up[0], sh_w_down[0])

    u = _conf_in(x2, mod[1], row(norm_mix_pre[1]), cv_w_in[0].astype(BF16), row(cv_b_in[0]))
    x3, h4, eidx, pos, gate, counts = _conf_out(
        u, cv_dw_w[0], row(cv_dw_b[0]), row(cv_ln_g[0]), row(cv_ln_b[0]),
        cv_w_out[0].astype(BF16), row(cv_b_out[0]),
        x2, mod[1], row(norm_mix_post[1]), row(norm_ffn_pre[1]),
        router_w[1].T, router_b[1].reshape(E, 1))
    return _moe(x3, h4, eidx, pos, gate, counts, mod[1], row(norm_ffn_post[1]),
                exp_w_gate[1], exp_w_up[1], exp_w_down[1], sh_w_gate[1], sh_w_up[1], sh_w_down[1])
```

```python
import functools

import jax
import jax.numpy as jnp
from jax import lax
from jax.experimental import pallas as pl
from jax.experimental.pallas import tpu as pltpu

F32 = jnp.float32
BF16 = jnp.bfloat16
I32 = jnp.int32

D = 1024
BATCH = 4
SEQ = 4096
CTX = 256
LTOT = CTX + SEQ
GRID_W = 64

D_INNER = 2048
HEADS = 32
GROUPS = 8
HPG = 4
HEADDIM = 64
NSTATE = 128
Q = 128
NCHUNK = LTOT // Q
CTX_CHUNKS = CTX // Q
CONV_DIM = D_INNER + 2 * GROUPS * NSTATE
SSM_K = 5
CONF_K = 31

E = 64
TOPK = 8
NGRP = 8
TOPG = 4
DE = 256
ROUTED_SCALE = 2.5
EPS = 1e-6

T = BATCH * SEQ
TM = 256
BLK = 256
NBLK = -(-(T * TOPK + E * (BLK - 1)) // BLK)
PROWS = (NBLK + 1) * BLK + 8

VMEM_LIMIT = 56 * 1024 * 1024
NEG = -1e30


def _cp(sem):
    return pltpu.CompilerParams(dimension_semantics=sem, vmem_limit_bytes=VMEM_LIMIT)


def _silu(v):
    return v * jax.nn.sigmoid(v)


def _rms(v, g):
    return v * lax.rsqrt(jnp.mean(v * v, axis=-1, keepdims=True) + EPS) * g


def _split3(v):
    a = v.astype(BF16)
    r = v - a.astype(F32)
    b = r.astype(BF16)
    c = (r - b.astype(F32)).astype(BF16)
    return a, b, c


def _dot(a, b):
    return jnp.dot(a, b, preferred_element_type=F32)


def _dot_nt(a, b):
    return lax.dot_general(a, b, (((1,), (1,)), ((), ())), preferred_element_type=F32)


U32 = jnp.uint32
RW = D // 2 // 128
_HI = 0xFFFF0000


def _pack2(lo, hi):
    ul = pltpu.bitcast(lo.astype(BF16).astype(F32), U32)
    uh = pltpu.bitcast(hi.astype(BF16).astype(F32), U32)
    return (ul >> 16) | (uh & U32(_HI))


def _unpack2(w):
    return pltpu.bitcast(w << 16, F32), pltpu.bitcast(w & U32(_HI), F32)


def _store_packed(ref, v):
    m = v.shape[0]
    for s in range(RW):
        lo = v[:, s * 128:(s + 1) * 128]
        hi = v[:, D // 2 + s * 128:D // 2 + (s + 1) * 128]
        ref[pl.ds(s, m, stride=RW), :] = _pack2(lo, hi)


def _load_packed(ref, m, base=0):
    los, his = [], []
    for s in range(RW):
        lo, hi = _unpack2(ref[pl.ds(base + s, m, stride=RW), :])
        los.append(lo.astype(BF16))
        his.append(hi.astype(BF16))
    return jnp.concatenate(los + his, axis=1)


def _ada_kernel(c_ref, w_ref, b_ref, o_ref):
    s = _silu(c_ref[...])
    o_ref[...] = jnp.dot(s, w_ref[...], preferred_element_type=F32,
                         precision=lax.Precision.HIGHEST) + b_ref[...]


def _ada(cvec, ada_w, ada_b):
    depth = ada_w.shape[0]
    tn = 1536
    return pl.pallas_call(
        _ada_kernel,
        grid=(depth, 6 * D // tn),
        in_specs=[pl.BlockSpec((8, D), lambda l, j: (0, 0)),
                  pl.BlockSpec((None, D, tn), lambda l, j: (l, 0, j)),
                  pl.BlockSpec((None, 1, tn), lambda l, j: (l, 0, j))],
        out_specs=pl.BlockSpec((None, 8, tn), lambda l, j: (l, 0, j)),
        out_shape=jax.ShapeDtypeStruct((depth, 8, 6 * D), F32),
        compiler_params=_cp(("arbitrary", "arbitrary")),
        name="ada",
    )(cvec, ada_w, ada_b.reshape(depth, 1, 6 * D))


def _ssm_in_kernel(x_ref, mod_ref, g_ref, wz_ref, wx0_ref, wx1_ref, wdt_ref, wdtT_ref, dtb_ref, dtbT_ref,
                   z_ref, xbc_ref, dt_ref, dtT_ref):
    h = _rms(x_ref[...], g_ref[...]) * (1.0 + mod_ref[1:2, :]) + mod_ref[0:1, :]
    hb = h.astype(BF16)
    z_ref[...] = _dot(hb, wz_ref[...]).astype(BF16)
    xbc_ref[:, :D_INNER] = _dot(hb, wx0_ref[...]).astype(BF16)
    xbc_ref[:, D_INNER:] = _dot(hb, wx1_ref[...]).astype(BF16)
    dt_ref[...] = jax.nn.softplus(_dot(hb, wdt_ref[...]) + dtb_ref[...])
    dtT_ref[...] = jax.nn.softplus(_dot_nt(wdtT_ref[...], hb) + dtbT_ref[...])


def _ssm_in(hcat, mod, gain, w_in, wdt, wdtT, dtb, dtbT):
    nt = LTOT // TM
    full = lambda b, i: (0, 0)
    assert CONV_DIM == 2 * D_INNER
    return pl.pallas_call(
        _ssm_in_kernel,
        grid=(BATCH, nt),
        in_specs=[pl.BlockSpec((None, TM, D), lambda b, i: (b, i, 0)),
                  pl.BlockSpec((None, 6, D), lambda b, i: (jnp.where(i == 0, BATCH, b), 0, 0)),
                  pl.BlockSpec((1, D), full),
                  pl.BlockSpec((D, D_INNER), lambda b, i: (0, 0)),
                  pl.BlockSpec((D, D_INNER), lambda b, i: (0, 1)),
                  pl.BlockSpec((D, D_INNER), lambda b, i: (0, 2)),
                  pl.BlockSpec((D, 2 * HEADS), full),
                  pl.BlockSpec((2 * HEADS, D), full),
                  pl.BlockSpec((1, 2 * HEADS), full),
                  pl.BlockSpec((2 * HEADS, 1), full)],
        out_specs=[pl.BlockSpec((None, TM, D_INNER), lambda b, i: (b, i, 0)),
                   pl.BlockSpec((None, TM, CONV_DIM), lambda b, i: (b, i, 0)),
                   pl.BlockSpec((None, TM, 2 * HEADS), lambda b, i: (b, i, 0)),
                   pl.BlockSpec((None, 2 * HEADS, TM), lambda b, i: (b, 0, i))],
        out_shape=[jax.ShapeDtypeStruct((BATCH, LTOT, D_INNER), BF16),
                   jax.ShapeDtypeStruct((BATCH, LTOT, CONV_DIM), BF16),
                   jax.ShapeDtypeStruct((BATCH, LTOT, 2 * HEADS), F32),
                   jax.ShapeDtypeStruct((BATCH, 2 * HEADS, LTOT), F32)],
        compiler_params=_cp(("arbitrary", "arbitrary")),
        name="ssm_in",
    )(hcat, mod, gain, w_in, w_in, w_in, wdt, wdtT, dtb, dtbT)


_HALO = 16
_CC = 512


def _ssm_conv_kernel(x_ref, xp_ref, xn_ref, w_ref, b_ref, xs_ref, bt_ref, cm_ref, ext_ref):
    i = pl.program_id(1)
    first_lat = CTX // Q
    pv = jnp.where((i == 0) | (i == first_lat), 0.0, 1.0)
    nv = jnp.where((i == first_lat - 1) | (i == NCHUNK - 1), 0.0, 1.0)
    for c in range(CONV_DIM // _CC):
        cs = slice(c * _CC, (c + 1) * _CC)
        ext_ref[0:8, :] = xp_ref[:, cs].astype(F32)[8:16, :] * pv
        ext_ref[8:8 + Q, :] = x_ref[:, cs].astype(F32)
        ext_ref[8 + Q:16 + Q, :] = xn_ref[:, cs].astype(F32)[0:8, :] * nv
        acc = jnp.broadcast_to(b_ref[:, cs], (Q, _CC))
        for k in range(SSM_K):
            acc = acc + w_ref[k:k + 1, cs] * ext_ref[6 + k:6 + k + Q, :]
        y = _silu(acc)
        lo = c * _CC
        if lo < D_INNER:
            xs_ref[:, cs] = y.astype(BF16)
        elif lo < D_INNER + GROUPS * NSTATE:
            o = lo - D_INNER
            bt_ref[o:o + _CC, :] = y.T.astype(BF16)
        else:
            o = lo - D_INNER - GROUPS * NSTATE
            cm_ref[:, o:o + _CC] = y.astype(BF16)


def _ssm_conv(xbc, w, b):
    nh = Q // _HALO
    last = LTOT // _HALO - 1
    gn = GROUPS * NSTATE
    return pl.pallas_call(
        _ssm_conv_kernel,
        grid=(BATCH, NCHUNK),
        in_specs=[pl.BlockSpec((None, Q, CONV_DIM), lambda b, i: (b, i, 0)),
                  pl.BlockSpec((None, _HALO, CONV_DIM), lambda b, i: (b, jnp.maximum(i * nh - 1, 0), 0)),
                  pl.BlockSpec((None, _HALO, CONV_DIM), lambda b, i: (b, jnp.minimum(i * nh + nh, last), 0)),
                  pl.BlockSpec((SSM_K, CONV_DIM), lambda b, i: (0, 0)),
                  pl.BlockSpec((1, CONV_DIM), lambda b, i: (0, 0))],
        out_specs=[pl.BlockSpec((None, Q, D_INNER), lambda b, i: (b, i, 0)),
                   pl.BlockSpec((None, gn, Q), lambda b, i: (b, 0, i)),
                   pl.BlockSpec((None, Q, gn), lambda b, i: (b, i, 0))],
        out_shape=[jax.ShapeDtypeStruct((BATCH, LTOT, D_INNER), BF16),
                   jax.ShapeDtypeStruct((BATCH, gn, LTOT), BF16),
                   jax.ShapeDtypeStruct((BATCH, LTOT, gn), BF16)],
        scratch_shapes=[pltpu.VMEM((Q + 16, _CC), F32)],
        compiler_params=_cp(("arbitrary", "arbitrary")),
        name="ssm_conv",
    )(xbc, xbc, xbc, w, b)


def _ssd_chunk(direction, xs_ref, cm_ref, bt_ref, dt_ref, dtT_ref, arow_ref, acol_ref, rexp_ref, s_ref):
    d0 = direction * HEADS
    dtc = dt_ref[:, d0:d0 + HEADS]
    dtr = dtT_ref[d0:d0 + HEADS, :]
    da_c = dtc * arow_ref[:, d0:d0 + HEADS]
    da_r = dtr * acol_ref[d0:d0 + HEADS, :]
    ii = lax.broadcasted_iota(I32, (Q, Q), 0)
    jj = lax.broadcasted_iota(I32, (Q, Q), 1)
    if direction == 0:
        lower = jj <= ii
        tot_idx = Q - 1
    else:
        lower = jj >= ii
        tot_idx = 0
    tri_c = jnp.where(lower, 1.0, 0.0).astype(BF16)
    upper = (ii <= jj) if direction == 0 else (ii >= jj)
    tri_r = jnp.where(upper, 1.0, 0.0).astype(BF16)
    c1, c2, c3 = _split3(da_c)
    cum_c = _dot(tri_c, c1) + _dot(tri_c, c2) + _dot(tri_c, c3)
    r1, r2, r3 = _split3(da_r)
    cum_r = _dot(r1, tri_r) + _dot(r2, tri_r) + _dot(r3, tri_r)
    tot_c = cum_c[tot_idx:tot_idx + 1, :]
    tot_r = cum_r[:, tot_idx:tot_idx + 1]
    rfac = dtr * jnp.exp(tot_r - cum_r)
    dec = jnp.exp(tot_c)
    dh = dec.astype(BF16)
    dl = (dec - dh.astype(F32)).astype(BF16)
    dec_x = (_dot(jnp.broadcast_to(dh, (8, HEADS)), rexp_ref[...])
             + _dot(jnp.broadcast_to(dl, (8, HEADS)), rexp_ref[...]))[0:1, :]
    lane = lax.broadcasted_iota(I32, (Q, HPG * HEADDIM), 1)
    ys = []
    for g in range(GROUPS):
        cg = cm_ref[:, g * NSTATE:(g + 1) * NSTATE]
        btg = bt_ref[g * NSTATE:(g + 1) * NSTATE, :]
        xg = xs_ref[:, g * 256:(g + 1) * 256]
        sg = s_ref[g]
        cb = _dot(cg, btg)
        rhs = jnp.concatenate([xg, sg.astype(BF16)], axis=0)
        cg32 = cg.astype(F32)
        btg32 = btg.astype(F32)
        yg = None
        ug = None
        for r in range(HPG):
            h = g * HPG + r
            colb = cum_c[:, h:h + 1]
            rowb = cum_r[h:h + 1, :]
            decay = jnp.exp(jnp.where(lower, colb - rowb, NEG))
            m = (cb * decay * dtr[h:h + 1, :]).astype(BF16)
            cs = (cg32 * jnp.exp(colb)).astype(BF16)
            res = _dot(jnp.concatenate([m, cs], axis=1), rhs)
            bw = (btg32 * rfac[h:h + 1, :]).astype(BF16)
            upd = _dot(bw, xg)
            if r == 0:
                yg, ug = res, upd
            else:
                sel = lane >= r * HEADDIM
                yg = jnp.where(sel, res, yg)
                ug = jnp.where(sel, upd, ug)
        s_ref[g] = sg * dec_x[:, g * 256:(g + 1) * 256] + ug
        ys.append(yg)
    return ys


def _ssd_bwd_kernel(xs_ref, cm_ref, bt_ref, dt_ref, dtT_ref, arow_ref, acol_ref, rexp_ref,
                    y_ref, s_ref):
    @pl.when(pl.program_id(1) == 0)
    def _():
        s_ref[...] = jnp.zeros_like(s_ref)
    ys = _ssd_chunk(1, xs_ref, cm_ref, bt_ref, dt_ref, dtT_ref, arow_ref, acol_ref, rexp_ref, s_ref)
    for g in range(GROUPS):
        y_ref[:, g * 256:(g + 1) * 256] = ys[g].astype(BF16)


def _ssd_fwd_kernel(xs_ref, cm_ref, bt_ref, dt_ref, dtT_ref, arow_ref, acol_ref, rexp_ref,
                    z_ref, yb_ref, dsk_ref, nw_ref, y_ref, s_ref):
    @pl.when(pl.program_id(1) == 0)
    def _():
        s_ref[...] = jnp.zeros_like(s_ref)
    ys = _ssd_chunk(0, xs_ref, cm_ref, bt_ref, dt_ref, dtT_ref, arow_ref, acol_ref, rexp_ref, s_ref)
    for g in range(GROUPS):
        gs = slice(g * 256, (g + 1) * 256)
        y = ys[g] + yb_ref[:, gs].astype(F32) + xs_ref[:, gs].astype(F32) * dsk_ref[:, gs]
        y = y * _silu(z_ref[:, gs].astype(F32))
        y = y * lax.rsqrt(jnp.mean(y * y, axis=-1, keepdims=True) + EPS) * nw_ref[:, gs]
        y_ref[:, gs] = y.astype(BF16)


def _ssd_specs(cmap):
    gn = GROUPS * NSTATE
    full = lambda b, j: (0, 0)
    return [pl.BlockSpec((None, Q, D_INNER), lambda b, j: (b, cmap(j), 0)),
            pl.BlockSpec((None, Q, gn), lambda b, j: (b, cmap(j), 0)),
            pl.BlockSpec((None, gn, Q), lambda b, j: (b, 0, cmap(j))),
            pl.BlockSpec((None, Q, 2 * HEADS), lambda b, j: (b, cmap(j), 0)),
            pl.BlockSpec((None, 2 * HEADS, Q), lambda b, j: (b, 0, cmap(j))),
            pl.BlockSpec((1, 2 * HEADS), full),
            pl.BlockSpec((2 * HEADS, 1), full),
            pl.BlockSpec((HEADS, D_INNER), full)]


def _ssd_bwd(xs, cm, bt, dt, dtT, arow, acol, rexp):
    cmap = lambda j: jnp.where(j < CTX_CHUNKS, CTX_CHUNKS - 1 - j, NCHUNK + CTX_CHUNKS - 1 - j)
    omap = lambda b, j: (b, NCHUNK - 1 - jnp.maximum(j, CTX_CHUNKS), 0)
    return pl.pallas_call(
        _ssd_bwd_kernel,
        grid=(BATCH, NCHUNK),
        in_specs=_ssd_specs(cmap),
        out_specs=pl.BlockSpec((None, Q, D_INNER), omap),
        out_shape=jax.ShapeDtypeStruct((BATCH, SEQ, D_INNER), BF16),
        scratch_shapes=[pltpu.VMEM((GROUPS, NSTATE, HPG * HEADDIM), F32)],
        compiler_params=_cp(("arbitrary", "arbitrary")),
        name="ssd_bwd",
    )(xs, cm, bt, dt, dtT, arow, acol, rexp)


def _ssd_fwd(xs, cm, bt, dt, dtT, arow, acol, rexp, z, yb, dsk, nw):
    cmap = lambda j: j
    lat = lambda b, j: (b, jnp.maximum(j - CTX_CHUNKS, 0), 0)
    full = lambda b, j: (0, 0)
    return pl.pallas_call(
        _ssd_fwd_kernel,
        grid=(BATCH, NCHUNK),
        in_specs=_ssd_specs(cmap) + [
            pl.BlockSpec((None, Q, D_INNER), lambda b, j: (b, j, 0)),
            pl.BlockSpec((None, Q, D_INNER), lat),
            pl.BlockSpec((1, D_INNER), full),
            pl.BlockSpec((1, D_INNER), full)],
        out_specs=pl.BlockSpec((None, Q, D_INNER), lat),
        out_shape=jax.ShapeDtypeStruct((BATCH, SEQ, D_INNER), BF16),
        scratch_shapes=[pltpu.VMEM((GROUPS, NSTATE, HPG * HEADDIM), F32)],
        compiler_params=_cp(("arbitrary", "arbitrary")),
        name="ssd_fwd",
    )(xs, cm, bt, dt, dtT, arow, acol, rexp, z, yb, dsk, nw)


def _route(h, rwT_ref, rb_ref, cnt_ref, eidx_ref, pos_ref, gate_ref):
    hh = h.astype(BF16)
    hl = (h - hh.astype(F32)).astype(BF16)
    w = rwT_ref[...]
    wh = w.astype(BF16)
    wl = (w - wh.astype(F32)).astype(BF16)
    logits = _dot_nt(wh, hh) + _dot_nt(wh, hl) + _dot_nt(wl, hh)
    scores = jax.nn.sigmoid(logits)
    sel = scores + rb_ref[...]
    per = E // NGRP
    sub = lax.broadcasted_iota(I32, (per, TM), 0)
    gscore = []
    for g in range(NGRP):
        blk = sel[g * per:(g + 1) * per, :]
        m1 = jnp.max(blk, axis=0, keepdims=True)
        first = jnp.min(jnp.where(blk == m1, sub, per), axis=0, keepdims=True)
        m2 = jnp.max(jnp.where(sub == first, -jnp.inf, blk), axis=0, keepdims=True)
        gscore.append(m1 + m2)
    masked = []
    for g in range(NGRP):
        rank = jnp.zeros((1, TM), F32)
        for o in range(NGRP):
            if o == g:
                continue
            ahead = (gscore[o] >= gscore[g]) if o < g else (gscore[o] > gscore[g])
            rank = rank + jnp.where(ahead, 1.0, 0.0)
        blk = sel[g * per:(g + 1) * per, :]
        masked.append(jnp.where(rank < TOPG, blk, -jnp.inf))
    v = jnp.concatenate(masked, axis=0)
    eio = lax.broadcasted_iota(I32, (E, TM), 0)
    kio = lax.broadcasted_iota(I32, (TOPK, TM), 0)
    eidx = jnp.zeros((TOPK, TM), I32)
    gsc = jnp.zeros((TOPK, TM), F32)
    hot = jnp.zeros((E, TM), F32)
    picks = []
    for k in range(TOPK):
        m = jnp.max(v, axis=0, keepdims=True)
        first = jnp.min(jnp.where(v == m, eio, E), axis=0, keepdims=True)
        pick = eio == first
        sc = jnp.sum(jnp.where(pick, scores, 0.0), axis=0, keepdims=True)
        eidx = jnp.where(kio == k, first, eidx)
        gsc = jnp.where(kio == k, sc, gsc)
        hot = jnp.where(pick, 1.0, hot)
        v = jnp.where(pick, -jnp.inf, v)
        picks.append(pick)
    gate = gsc / jnp.sum(gsc, axis=0, keepdims=True) * ROUTED_SCALE
    ti = lax.broadcasted_iota(I32, (TM, TM), 0)
    tj = lax.broadcasted_iota(I32, (TM, TM), 1)
    before = jnp.where(ti < tj, 1.0, 0.0).astype(BF16)
    posfull = _dot(hot.astype(BF16), before) + cnt_ref[...]
    pos = jnp.zeros((TOPK, TM), F32)
    for k in range(TOPK):
        pk = jnp.sum(jnp.where(picks[k], posfull, 0.0), axis=0, keepdims=True)
        pos = jnp.where(kio == k, pk, pos)
    cnt_ref[...] = cnt_ref[...] + jnp.sum(hot, axis=1, keepdims=True)
    eidx_ref[...] = eidx
    pos_ref[...] = pos.astype(I32)
    eye = jnp.where(ti == tj, 1.0, 0.0).astype(BF16)
    g1, g2, g3 = _split3(gate)
    gate_ref[...] = _dot_nt(eye, g1) + _dot_nt(eye, g2) + _dot_nt(eye, g3)


def _mix_epilogue(y, x_ref, mod_ref, gpost_ref, gpre_ref, rwT_ref, rb_ref,
                  x1_ref, h2_ref, eidx_ref, pos_ref, gate_ref, cnt_out_ref, cnt_ref):
    first = (pl.program_id(0) == 0) & (pl.program_id(1) == 0)

    @pl.when(first)
    def _():
        cnt_ref[...] = jnp.zeros_like(cnt_ref)
    x1 = x_ref[...] + mod_ref[2:3, :] * _rms(y, gpost_ref[...])
    x1_ref[...] = x1
    h2 = _rms(x1, gpre_ref[...]) * (1.0 + mod_ref[4:5, :]) + mod_ref[3:4, :]
    _store_packed(h2_ref, h2)
    _route(h2, rwT_ref, rb_ref, cnt_ref, eidx_ref, pos_ref, gate_ref)
    cnt_out_ref[...] = cnt_ref[...]


def _ssm_out_kernel(y_ref, w_ref, x_ref, mod_ref, gpost_ref, gpre_ref, rwT_ref, rb_ref,
                    x1_ref, h2_ref, eidx_ref, pos_ref, gate_ref, cnt_out_ref, cnt_ref):
    y = _dot(y_ref[...], w_ref[...])
    _mix_epilogue(y, x_ref, mod_ref, gpost_ref, gpre_ref, rwT_ref, rb_ref,
                  x1_ref, h2_ref, eidx_ref, pos_ref, gate_ref, cnt_out_ref, cnt_ref)


_CW = 256
_PAD = 16


def _conf_out_kernel(u_ref, dww_ref, dwb_ref, lng_ref, lnb_ref, w_ref, b_ref,
                     x_ref, mod_ref, gpost_ref, gpre_ref, rwT_ref, rb_ref,
                     x1_ref, h2_ref, eidx_ref, pos_ref, gate_ref, cnt_out_ref,
                     cnt_ref, ext_ref, v_ref):
    nrow = TM // GRID_W
    zpad = jnp.zeros((_PAD, D), F32)
    for r in range(nrow):
        base = r * (GRID_W + 2 * _PAD)
        ext_ref[base:base + _PAD, :] = zpad
        ext_ref[base + _PAD:base + _PAD + GRID_W, :] = u_ref[r * GRID_W:(r + 1) * GRID_W, :].astype(F32)
        ext_ref[base + _PAD + GRID_W:base + 2 * _PAD + GRID_W, :] = zpad
    for r in range(nrow):
        base = r * (GRID_W + 2 * _PAD)
        for c in range(D // _CW):
            cs = slice(c * _CW, (c + 1) * _CW)
            acc = jnp.broadcast_to(dwb_ref[:, cs], (GRID_W, _CW))
            for k in range(CONF_K):
                o = base + _PAD - CONF_K // 2 + k
                acc = acc + dww_ref[k:k + 1, cs] * ext_ref[o:o + GRID_W, cs]
            v_ref[r * GRID_W:(r + 1) * GRID_W, cs] = acc
    v = v_ref[...]
    mu = jnp.mean(v, axis=-1, keepdims=True)
    vc = v - mu
    ln = vc * lax.rsqrt(jnp.mean(vc * vc, axis=-1, keepdims=True) + EPS) * lng_ref[...] + lnb_ref[...]
    y = _dot(_silu(ln).astype(BF16), w_ref[...]) + b_ref[...]
    _mix_epilogue(y, x_ref, mod_ref, gpost_ref, gpre_ref, rwT_ref, rb_ref,
                  x1_ref, h2_ref, eidx_ref, pos_ref, gate_ref, cnt_out_ref, cnt_ref)


def _mix_out_common_specs():
    full = lambda b, i: (0, 0)
    nt = SEQ // TM
    in_specs = [pl.BlockSpec((None, TM, D), lambda b, i: (b, i, 0)),
                pl.BlockSpec((None, 6, D), lambda b, i: (b, 0, 0)),
                pl.BlockSpec((1, D), full), pl.BlockSpec((1, D), full),
                pl.BlockSpec((E, D), full), pl.BlockSpec((E, 1), full)]
    out_specs = [pl.BlockSpec((None, TM, D), lambda b, i: (b, i, 0)),
                 pl.BlockSpec((TM * RW, 128), lambda b, i: (b * nt + i, 0)),
                 pl.BlockSpec((TOPK, TM), lambda b, i: (0, b * nt + i)),
                 pl.BlockSpec((TOPK, TM), lambda b, i: (0, b * nt + i)),
                 pl.BlockSpec((TM, TOPK), lambda b, i: (b * nt + i, 0)),
                 pl.BlockSpec((E, 1), full)]
    out_shape = [jax.ShapeDtypeStruct((BATCH, SEQ, D), F32),
                 jax.ShapeDtypeStruct((T * RW, 128), U32),
                 jax.ShapeDtypeStruct((TOPK, T), I32),
                 jax.ShapeDtypeStruct((TOPK, T), I32),
                 jax.ShapeDtypeStruct((T, TOPK), F32),
                 jax.ShapeDtypeStruct((E, 1), F32)]
    return in_specs, out_specs, out_shape


def _ssm_out(y, w, x, mod, gpost, gpre, rwT, rb):
    common_in, out_specs, out_shape = _mix_out_common_specs()
    return pl.pallas_call(
        _ssm_out_kernel,
        grid=(BATCH, SEQ // TM),
        in_specs=[pl.BlockSpec((None, TM, D_INNER), lambda b, i: (b, i, 0)),
                  pl.BlockSpec((D_INNER, D), lambda b, i: (0, 0))] + common_in,
        out_specs=out_specs, out_shape=out_shape,
        scratch_shapes=[pltpu.VMEM((E, 1), F32)],
        compiler_params=_cp(("arbitrary", "arbitrary")),
        name="ssm_out",
    )(y, w, x, mod, gpost, gpre, rwT, rb)


def _conf_out(u, dww, dwb, lng, lnb, w, b, x, mod, gpost, gpre, rwT, rb):
    common_in, out_specs, out_shape = _mix_out_common_specs()
    full = lambda b_, i: (0, 0)
    nrow = TM // GRID_W
    return pl.pallas_call(
        _conf_out_kernel,
        grid=(BATCH, SEQ // TM),
        in_specs=[pl.BlockSpec((None, TM, D), lambda b_, i: (b_, i, 0)),
                  pl.BlockSpec((CONF_K, D), full), pl.BlockSpec((1, D), full),
                  pl.BlockSpec((1, D), full), pl.BlockSpec((1, D), full),
                  pl.BlockSpec((D, D), full), pl.BlockSpec((1, D), full)] + common_in,
        out_specs=out_specs, out_shape=out_shape,
        scratch_shapes=[pltpu.VMEM((E, 1), F32),
                        pltpu.VMEM((nrow * (GRID_W + 2 * _PAD), D), F32),
                        pltpu.VMEM((TM, D), F32)],
        compiler_params=_cp(("arbitrary", "arbitrary")),
        name="conf_out",
    )(u, dww, dwb, lng, lnb, w, b, x, mod, gpost, gpre, rwT, rb)


def _conf_in_kernel(x_ref, mod_ref, g_ref, wa_ref, wg_ref, ba_ref, bg_ref, u_ref):
    h = _rms(x_ref[...], g_ref[...]) * (1.0 + mod_ref[1:2, :]) + mod_ref[0:1, :]
    hb = h.astype(BF16)
    a = _dot(hb, wa_ref[...]) + ba_ref[...]
    g = _dot(hb, wg_ref[...]) + bg_ref[...]
    u_ref[...] = (a * jax.nn.sigmoid(g)).astype(BF16)


def _conf_in(x, mod, gain, w, bias):
    full = lambda b, i: (0, 0)
    return pl.pallas_call(
        _conf_in_kernel,
        grid=(BATCH, SEQ // TM),
        in_specs=[pl.BlockSpec((None, TM, D), lambda b, i: (b, i, 0)),
                  pl.BlockSpec((None, 6, D), lambda b, i: (b, 0, 0)),
                  pl.BlockSpec((1, D), full),
                  pl.BlockSpec((D, D), full), pl.BlockSpec((D, D), lambda b, i: (0, 1)),
                  pl.BlockSpec((1, D), full), pl.BlockSpec((1, D), lambda b, i: (0, 1))],
        out_specs=pl.BlockSpec((None, TM, D), lambda b, i: (b, i, 0)),
        out_shape=jax.ShapeDtypeStruct((BATCH, SEQ, D), BF16),
        compiler_params=_cp(("arbitrary", "arbitrary")),
        name="conf_in",
    )(x, mod, gain, w, w, bias, bias)


def _row_copy(src, si, dst, di, sem):
    return pltpu.make_async_copy(src.at[pl.ds(pl.multiple_of(si * RW, RW), RW), :],
                                 dst.at[pl.ds(pl.multiple_of(di * RW, RW), RW), :], sem)


ZROWS = BLK + 8


def _dispatch_kernel(zs_ref, dest_ref, h_ref, xs_ref, zero_ref, zsem, sem):
    @pl.when(pl.program_id(0) == 0)
    def _():
        zero_ref[...] = jnp.zeros_like(zero_ref)

        def zcopy(e):
            start = pl.multiple_of(zs_ref[e] * RW, 8 * RW)
            return pltpu.make_async_copy(zero_ref, xs_ref.at[pl.ds(start, ZROWS * RW), :], zsem)

        def zstart(e, c):
            zcopy(e).start()
            return c

        def zwait(e, c):
            zcopy(e).wait()
            return c
        lax.fori_loop(0, E, zstart, 0)
        lax.fori_loop(0, E, zwait, 0)

    def issue(t, c):
        for k in range(TOPK):
            _row_copy(h_ref, t, xs_ref, dest_ref[t * TOPK + k], sem).start(priority=k % 2)
        return c

    def drain(t, c):
        for k in range(TOPK):
            _row_copy(h_ref, t, xs_ref, 0, sem).wait()
        return c
    lax.fori_loop(0, TM, issue, 0)
    lax.fori_loop(0, TM, drain, 0)


def _dispatch(zstart, dest, hp):
    grid_spec = pltpu.PrefetchScalarGridSpec(
        num_scalar_prefetch=1,
        grid=(T // TM,),
        in_specs=[pl.BlockSpec((TM * TOPK,), lambda i, zs: (i,), memory_space=pltpu.SMEM),
                  pl.BlockSpec((TM * RW, 128), lambda i, zs: (i, 0))],
        out_specs=pl.BlockSpec(memory_space=pl.ANY),
        scratch_shapes=[pltpu.VMEM((ZROWS * RW, 128), U32), pltpu.SemaphoreType.DMA(()),
                        pltpu.SemaphoreType.DMA(())])
    return pl.pallas_call(
        _dispatch_kernel,
        grid_spec=grid_spec,
        out_shape=jax.ShapeDtypeStruct((PROWS * RW, 128), U32),
        compiler_params=_cp(("arbitrary",)),
        name="moe_dispatch",
    )(zstart, dest, hp)


def _experts_kernel(be_ref, na_ref, x_ref, wg_ref, wu_ref, wd_ref, y_ref, wgb_ref, wub_ref, wdb_ref):
    i = pl.program_id(0)

    @pl.when((i == 0) | (be_ref[i] != be_ref[jnp.maximum(i - 1, 0)]))
    def _():
        wgb_ref[...] = wg_ref[...].astype(BF16)
        wub_ref[...] = wu_ref[...].astype(BF16)
        wdb_ref[...] = wd_ref[...].astype(BF16)

    @pl.when(i < na_ref[0])
    def _():
        xb = _load_packed(x_ref, BLK)
        g = _dot(xb, wgb_ref[...])
        u = _dot(xb, wub_ref[...])
        a = (_silu(g) * u).astype(BF16)
        _store_packed(y_ref, _dot(a, wdb_ref[...]))

    @pl.when(i >= na_ref[0])
    def _():
        y_ref[...] = jnp.zeros_like(y_ref)


def _experts(layer, block_e, nact, xs, wg, wu, wd):
    xmap = lambda i, be, na: (jnp.minimum(i, na[0] - 1), 0)
    wmap = lambda i, be, na: (layer, be[i], 0, 0)
    grid_spec = pltpu.PrefetchScalarGridSpec(
        num_scalar_prefetch=2,
        grid=(NBLK,),
        in_specs=[pl.BlockSpec((BLK * RW, 128), xmap),
                  pl.BlockSpec((None, None, D, DE), wmap),
                  pl.BlockSpec((None, None, D, DE), wmap),
                  pl.BlockSpec((None, None, DE, D), wmap)],
        out_specs=pl.BlockSpec((BLK * RW, 128), lambda i, be, na: (i, 0)),
        scratch_shapes=[pltpu.VMEM((D, DE), BF16), pltpu.VMEM((D, DE), BF16),
                        pltpu.VMEM((DE, D), BF16)])
    return pl.pallas_call(
        _experts_kernel,
        grid_spec=grid_spec,
        out_shape=jax.ShapeDtypeStruct((NBLK * BLK * RW, 128), U32),
        compiler_params=_cp(("arbitrary",)),
        name="moe_experts",
    )(block_e, nact, xs, wg, wu, wd)


def _combine_kernel(dcur_ref, dnxt_ref, gate_ref, h_ref, x1_ref, mod_ref, gpost_ref,
                    sg_ref, su_ref, sd_ref, ys_ref, o_ref, buf_ref, sems):
    i = pl.program_id(0)
    n = pl.num_programs(0)

    def gather(dest_ref, slot):
        def body(t, c):
            for k in range(TOPK):
                _row_copy(ys_ref, dest_ref[t * TOPK + k], buf_ref.at[slot, k], t,
                          sems.at[slot]).start(priority=k % 2)
            return c
        lax.fori_loop(0, TM, body, 0)

    @pl.when(i == 0)
    def _():
        gather(dcur_ref, 0)

    @pl.when(i + 1 < n)
    def _():
        gather(dnxt_ref, (i + 1) % 2)

    hb = _load_packed(h_ref, TM)
    a = (_silu(_dot(hb, sg_ref[...])) * _dot(hb, su_ref[...])).astype(BF16)
    f = _dot(a, sd_ref[...])

    slot = i % 2

    def drain(t, c):
        for k in range(TOPK):
            _row_copy(ys_ref, 0, buf_ref.at[slot, k], 0, sems.at[slot]).wait()
        return c
    lax.fori_loop(0, TM, drain, 0)

    gate = gate_ref[...]
    gb = [jnp.broadcast_to(gate[:, k:k + 1], (TM, 128)) for k in range(TOPK)]
    los, his = [], []
    for s in range(RW):
        lo = f[:, s * 128:(s + 1) * 128]
        hi = f[:, D // 2 + s * 128:D // 2 + (s + 1) * 128]
        for k in range(TOPK):
            wl, wh = _unpack2(buf_ref[slot, k, pl.ds(s, TM, stride=RW), :])
            lo = lo + gb[k] * wl
            hi = hi + gb[k] * wh
        los.append(lo)
        his.append(hi)
    f = jnp.concatenate(los + his, axis=1)
    o_ref[...] = x1_ref[...] + mod_ref[5:6, :] * _rms(f, gpost_ref[...])


def _combine(dest, gate, hp, x1, mod, gpost, sg, su, sd, ys):
    nt = SEQ // TM
    n = T // TM
    full = lambda i: (0, 0)
    return pl.pallas_call(
        _combine_kernel,
        grid=(n,),
        in_specs=[pl.BlockSpec((TM * TOPK,), lambda i: (i,), memory_space=pltpu.SMEM),
                  pl.BlockSpec((TM * TOPK,), lambda i: (jnp.minimum(i + 1, n - 1),),
                               memory_space=pltpu.SMEM),
                  pl.BlockSpec((TM, TOPK), lambda i: (i, 0)),
                  pl.BlockSpec((TM * RW, 128), lambda i: (i, 0)),
                  pl.BlockSpec((TM, D), lambda i: (i, 0)),
                  pl.BlockSpec((None, 6, D), lambda i: (i // nt, 0, 0)),
                  pl.BlockSpec((1, D), full),
                  pl.BlockSpec((D, DE), full), pl.BlockSpec((D, DE), full), pl.BlockSpec((DE, D), full),
                  pl.BlockSpec(memory_space=pl.ANY)],
        out_specs=pl.BlockSpec((TM, D), lambda i: (i, 0)),
        out_shape=jax.ShapeDtypeStruct((T, D), F32),
        scratch_shapes=[pltpu.VMEM((2, TOPK, TM * RW, 128), U32), pltpu.SemaphoreType.DMA((2,))],
        compiler_params=_cp(("arbitrary",)),
        name="moe_combine",
    )(dest, dest, gate, hp, x1.reshape(T, D), mod, gpost, sg, su, sd, ys).reshape(BATCH, SEQ, D)


def _moe(layer, x1, hp, eidx, pos, gate, counts, mod, gpost, wg, wu, wd, sg, su, sd):
    cnt = counts.reshape(E).astype(I32)
    padded = (cnt + BLK - 1) // BLK * BLK
    pend = jnp.cumsum(padded)
    pstart = pend - padded
    ids = jnp.arange(E, dtype=I32)
    dest = pos + jnp.sum(jnp.where(eidx[:, :, None] == ids, pstart, 0), axis=-1)
    dest = dest.T.reshape(T * TOPK)
    nact = (pend[-1] // BLK).reshape(1).astype(I32)
    blk0 = jnp.arange(NBLK, dtype=I32) * BLK
    block_e = jnp.sum((pend[None, :] <= blk0[:, None]).astype(I32), axis=1)
    block_e = jnp.minimum(block_e, E - 1).astype(I32)
    xs = _dispatch(((pstart + cnt) // 8 * 8).astype(I32), dest, hp)
    ys = _experts(layer, block_e, nact, xs, wg, wu, wd)
    return _combine(dest, gate, hp, x1, mod, gpost, sg.astype(BF16), su.astype(BF16),
                    sd.astype(BF16), ys)


def kernel(x, c, ctx, c_ctx, ada_w, ada_b, norm_mix_pre, norm_mix_post, norm_ffn_pre, norm_ffn_post, ssm_w_in, ssm_conv_w, ssm_conv_b, ssm_dt_bias, ssm_a_log, ssm_d, ssm_norm, ssm_w_out, cv_w_in, cv_b_in, cv_dw_w, cv_dw_b, cv_ln_g, cv_ln_b, cv_w_out, cv_b_out, router_w, router_b, exp_w_gate, exp_w_up, exp_w_down, sh_w_gate, sh_w_up, sh_w_down):
    row = lambda v: v.reshape(1, -1)
    cvec = jnp.concatenate([c, c_ctx[None, :], jnp.zeros((3, D), F32)], axis=0)
    mod = _ada(cvec, ada_w, ada_b).reshape(2, 8, 6, D)

    hcat = jnp.concatenate([ctx, x], axis=1)
    w_in = ssm_w_in[0].astype(BF16)
    dtb = ssm_dt_bias[0].reshape(1, 2 * HEADS)
    wdt = w_in[:, D_INNER + CONV_DIM:]
    z, xbc, dt, dtT = _ssm_in(hcat, mod[0], row(norm_mix_pre[0]), w_in, wdt, wdt.T,
                              dtb, dtb.reshape(2 * HEADS, 1))
    xs, bt, cm = _ssm_conv(xbc, ssm_conv_w[0], row(ssm_conv_b[0]))
    a = -jnp.exp(ssm_a_log[0].astype(F32)).reshape(1, 2 * HEADS)
    rexp = (jnp.arange(D_INNER)[None, :] // HEADDIM == jnp.arange(HEADS)[:, None]).astype(BF16)
    yb = _ssd_bwd(xs, cm, bt, dt, dtT, a, a.reshape(2 * HEADS, 1), rexp)
    dsk = jnp.repeat(ssm_d[0], HEADDIM).reshape(1, D_INNER)
    ygn = _ssd_fwd(xs, cm, bt, dt, dtT, a, a.reshape(2 * HEADS, 1), rexp, z, yb, dsk,
                   row(ssm_norm[0]))
    x1, h2, eidx, pos, gate, counts = _ssm_out(
        ygn, ssm_w_out[0].astype(BF16), x, mod[0], row(norm_mix_post[0]), row(norm_ffn_pre[0]),
        router_w[0].T, router_b[0].reshape(E, 1))
    x2 = _moe(0, x1, h2, eidx, pos, gate, counts, mod[0], row(norm_ffn_post[0]),
              exp_w_gate, exp_w_up, exp_w_down, sh_w_gate[0], sh_w_up[0], sh_w_down[0])

    u = _conf_in(x2, mod[1], row(norm_mix_pre[1]), cv_w_in[0].astype(BF16), row(cv_b_in[0]))
    x3, h4, eidx, pos, gate, counts = _conf_out(
        u, cv_dw_w[0], row(cv_dw_b[0]), row(cv_ln_g[0]), row(cv_ln_b[0]),
        cv_w_out[0].astype(BF16), row(cv_b_out[0]),
        x2, mod[1], row(norm_mix_post[1]), row(norm_ffn_pre[1]),
        router_w[1].T, router_b[1].reshape(E, 1))
    return _moe(1, x3, h4, eidx, pos, gate, counts, mod[1], row(norm_ffn_post[1]),
                exp_w_gate, exp_w_up, exp_w_down, sh_w_gate[1], sh_w_up[1], sh_w_down[1])
```

```python
import functools

import jax
import jax.numpy as jnp
from jax import lax
from jax.experimental import pallas as pl
from jax.experimental.pallas import tpu as pltpu

F32 = jnp.float32
BF16 = jnp.bfloat16
I32 = jnp.int32

D = 1024
BATCH = 4
SEQ = 4096
CTX = 256
LTOT = CTX + SEQ
GRID_W = 64

D_INNER = 2048
HEADS = 32
GROUPS = 8
HPG = 4
HEADDIM = 64
NSTATE = 128
Q = 128
NCHUNK = LTOT // Q
CTX_CHUNKS = CTX // Q
CONV_DIM = D_INNER + 2 * GROUPS * NSTATE
SSM_K = 5
CONF_K = 31

E = 64
TOPK = 8
NGRP = 8
TOPG = 4
DE = 256
ROUTED_SCALE = 2.5
EPS = 1e-6

T = BATCH * SEQ
TM = 256
BLK = 512
NBLK = -(-(T * TOPK + E * (BLK - 1)) // BLK)
PROWS = (NBLK + 1) * BLK + 8

VMEM_LIMIT = 56 * 1024 * 1024
NEG = -1e30


def _cp(sem):
    return pltpu.CompilerParams(dimension_semantics=sem, vmem_limit_bytes=VMEM_LIMIT)


def _silu(v):
    return v * jax.nn.sigmoid(v)


def _rms(v, g):
    return v * lax.rsqrt(jnp.mean(v * v, axis=-1, keepdims=True) + EPS) * g


def _split3(v):
    a = v.astype(BF16)
    r = v - a.astype(F32)
    b = r.astype(BF16)
    c = (r - b.astype(F32)).astype(BF16)
    return a, b, c


def _dot(a, b):
    return jnp.dot(a, b, preferred_element_type=F32)


def _dot_nt(a, b):
    return lax.dot_general(a, b, (((1,), (1,)), ((), ())), preferred_element_type=F32)


U32 = jnp.uint32
RW = D // 2 // 128
_HI = 0xFFFF0000


def _pack2(lo, hi):
    ul = pltpu.bitcast(lo.astype(BF16).astype(F32), U32)
    uh = pltpu.bitcast(hi.astype(BF16).astype(F32), U32)
    return (ul >> 16) | (uh & U32(_HI))


def _unpack2(w):
    return pltpu.bitcast(w << 16, F32), pltpu.bitcast(w & U32(_HI), F32)


def _store_packed(ref, v):
    m = v.shape[0]
    for s in range(RW):
        lo = v[:, s * 128:(s + 1) * 128]
        hi = v[:, D // 2 + s * 128:D // 2 + (s + 1) * 128]
        ref[pl.ds(s, m, stride=RW), :] = _pack2(lo, hi)


def _load_packed(ref, m, base=0):
    los, his = [], []
    for s in range(RW):
        lo, hi = _unpack2(ref[pl.ds(base + s, m, stride=RW), :])
        los.append(lo.astype(BF16))
        his.append(hi.astype(BF16))
    return jnp.concatenate(los + his, axis=1)


def _ada_kernel(c_ref, w_ref, b_ref, o_ref):
    s = _silu(c_ref[...])
    o_ref[...] = jnp.dot(s, w_ref[...], preferred_element_type=F32,
                         precision=lax.Precision.HIGHEST) + b_ref[...]


def _ada(cvec, ada_w, ada_b):
    depth = ada_w.shape[0]
    tn = 1536
    return pl.pallas_call(
        _ada_kernel,
        grid=(depth, 6 * D // tn),
        in_specs=[pl.BlockSpec((8, D), lambda l, j: (0, 0)),
                  pl.BlockSpec((None, D, tn), lambda l, j: (l, 0, j)),
                  pl.BlockSpec((None, 1, tn), lambda l, j: (l, 0, j))],
        out_specs=pl.BlockSpec((None, 8, tn), lambda l, j: (l, 0, j)),
        out_shape=jax.ShapeDtypeStruct((depth, 8, 6 * D), F32),
        compiler_params=_cp(("arbitrary", "arbitrary")),
        name="ada",
    )(cvec, ada_w, ada_b.reshape(depth, 1, 6 * D))


def _ssm_in_kernel(x_ref, mod_ref, g_ref, wz_ref, wx0_ref, wx1_ref, wdt_ref, wdtT_ref, dtb_ref, dtbT_ref,
                   z_ref, xbc_ref, dt_ref, dtT_ref):
    h = _rms(x_ref[...], g_ref[...]) * (1.0 + mod_ref[1:2, :]) + mod_ref[0:1, :]
    hb = h.astype(BF16)
    z_ref[...] = _dot(hb, wz_ref[...]).astype(BF16)
    xbc_ref[:, :D_INNER] = _dot(hb, wx0_ref[...]).astype(BF16)
    xbc_ref[:, D_INNER:] = _dot(hb, wx1_ref[...]).astype(BF16)
    dt_ref[...] = jax.nn.softplus(_dot(hb, wdt_ref[...]) + dtb_ref[...])
    dtT_ref[...] = jax.nn.softplus(_dot_nt(wdtT_ref[...], hb) + dtbT_ref[...])


def _ssm_in(hcat, mod, gain, w_in, wdt, wdtT, dtb, dtbT):
    nt = LTOT // TM
    full = lambda b, i: (0, 0)
    assert CONV_DIM == 2 * D_INNER
    return pl.pallas_call(
        _ssm_in_kernel,
        grid=(BATCH, nt),
        in_specs=[pl.BlockSpec((None, TM, D), lambda b, i: (b, i, 0)),
                  pl.BlockSpec((None, 6, D), lambda b, i: (jnp.where(i == 0, BATCH, b), 0, 0)),
                  pl.BlockSpec((1, D), full),
                  pl.BlockSpec((D, D_INNER), lambda b, i: (0, 0)),
                  pl.BlockSpec((D, D_INNER), lambda b, i: (0, 1)),
                  pl.BlockSpec((D, D_INNER), lambda b, i: (0, 2)),
                  pl.BlockSpec((D, 2 * HEADS), full),
                  pl.BlockSpec((2 * HEADS, D), full),
                  pl.BlockSpec((1, 2 * HEADS), full),
                  pl.BlockSpec((2 * HEADS, 1), full)],
        out_specs=[pl.BlockSpec((None, TM, D_INNER), lambda b, i: (b, i, 0)),
                   pl.BlockSpec((None, TM, CONV_DIM), lambda b, i: (b, i, 0)),
                   pl.BlockSpec((None, TM, 2 * HEADS), lambda b, i: (b, i, 0)),
                   pl.BlockSpec((None, 2 * HEADS, TM), lambda b, i: (b, 0, i))],
        out_shape=[jax.ShapeDtypeStruct((BATCH, LTOT, D_INNER), BF16),
                   jax.ShapeDtypeStruct((BATCH, LTOT, CONV_DIM), BF16),
                   jax.ShapeDtypeStruct((BATCH, LTOT, 2 * HEADS), F32),
                   jax.ShapeDtypeStruct((BATCH, 2 * HEADS, LTOT), F32)],
        compiler_params=_cp(("arbitrary", "arbitrary")),
        name="ssm_in",
    )(hcat, mod, gain, w_in, w_in, w_in, wdt, wdtT, dtb, dtbT)


_HALO = 16
_CC = 512


def _ssm_conv_kernel(x_ref, xp_ref, xn_ref, w_ref, b_ref, xs_ref, bt_ref, cm_ref, ext_ref):
    i = pl.program_id(1)
    first_lat = CTX // Q
    pv = jnp.where((i == 0) | (i == first_lat), 0.0, 1.0)
    nv = jnp.where((i == first_lat - 1) | (i == NCHUNK - 1), 0.0, 1.0)
    for c in range(CONV_DIM // _CC):
        cs = slice(c * _CC, (c + 1) * _CC)
        ext_ref[0:8, :] = xp_ref[:, cs].astype(F32)[8:16, :] * pv
        ext_ref[8:8 + Q, :] = x_ref[:, cs].astype(F32)
        ext_ref[8 + Q:16 + Q, :] = xn_ref[:, cs].astype(F32)[0:8, :] * nv
        acc = jnp.broadcast_to(b_ref[:, cs], (Q, _CC))
        for k in range(SSM_K):
            wk = jnp.concatenate([w_ref[8 * k:8 * k + 8, cs]] * (Q // 8), axis=0)
            acc = acc + wk * ext_ref[6 + k:6 + k + Q, :]
        y = _silu(acc)
        lo = c * _CC
        if lo < D_INNER:
            xs_ref[:, cs] = y.astype(BF16)
        elif lo < D_INNER + GROUPS * NSTATE:
            o = lo - D_INNER
            bt_ref[o:o + _CC, :] = y.T.astype(BF16)
        else:
            o = lo - D_INNER - GROUPS * NSTATE
            cm_ref[:, o:o + _CC] = y.astype(BF16)


def _ssm_conv(xbc, w, b):
    nh = Q // _HALO
    last = LTOT // _HALO - 1
    gn = GROUPS * NSTATE
    return pl.pallas_call(
        _ssm_conv_kernel,
        grid=(BATCH, NCHUNK),
        in_specs=[pl.BlockSpec((None, Q, CONV_DIM), lambda b, i: (b, i, 0)),
                  pl.BlockSpec((None, _HALO, CONV_DIM), lambda b, i: (b, jnp.maximum(i * nh - 1, 0), 0)),
                  pl.BlockSpec((None, _HALO, CONV_DIM), lambda b, i: (b, jnp.minimum(i * nh + nh, last), 0)),
                  pl.BlockSpec((SSM_K * 8, CONV_DIM), lambda b, i: (0, 0)),
                  pl.BlockSpec((1, CONV_DIM), lambda b, i: (0, 0))],
        out_specs=[pl.BlockSpec((None, Q, D_INNER), lambda b, i: (b, i, 0)),
                   pl.BlockSpec((None, gn, Q), lambda b, i: (b, 0, i)),
                   pl.BlockSpec((None, Q, gn), lambda b, i: (b, i, 0))],
        out_shape=[jax.ShapeDtypeStruct((BATCH, LTOT, D_INNER), BF16),
                   jax.ShapeDtypeStruct((BATCH, gn, LTOT), BF16),
                   jax.ShapeDtypeStruct((BATCH, LTOT, gn), BF16)],
        scratch_shapes=[pltpu.VMEM((Q + 16, _CC), F32)],
        compiler_params=_cp(("arbitrary", "arbitrary")),
        name="ssm_conv",
    )(xbc, xbc, xbc, w, b)


def _ssd_chunk(direction, xs_ref, cm_ref, bt_ref, dt_ref, dtT_ref, arow_ref, acol_ref, rexp_ref, s_ref):
    d0 = direction * HEADS
    dtc = dt_ref[:, d0:d0 + HEADS]
    dtr = dtT_ref[d0:d0 + HEADS, :]
    da_c = dtc * arow_ref[:, d0:d0 + HEADS]
    da_r = dtr * acol_ref[d0:d0 + HEADS, :]
    ii = lax.broadcasted_iota(I32, (Q, Q), 0)
    jj = lax.broadcasted_iota(I32, (Q, Q), 1)
    if direction == 0:
        lower = jj <= ii
        tot_idx = Q - 1
    else:
        lower = jj >= ii
        tot_idx = 0
    tri_c = jnp.where(lower, 1.0, 0.0).astype(BF16)
    upper = (ii <= jj) if direction == 0 else (ii >= jj)
    tri_r = jnp.where(upper, 1.0, 0.0).astype(BF16)
    c1, c2, c3 = _split3(da_c)
    cum_c = _dot(tri_c, c1) + _dot(tri_c, c2) + _dot(tri_c, c3)
    r1, r2, r3 = _split3(da_r)
    cum_r = _dot(r1, tri_r) + _dot(r2, tri_r) + _dot(r3, tri_r)
    tot_c = cum_c[tot_idx:tot_idx + 1, :]
    tot_r = cum_r[:, tot_idx:tot_idx + 1]
    rfac = dtr * jnp.exp(tot_r - cum_r)
    dec = jnp.exp(tot_c)
    dh = dec.astype(BF16)
    dl = (dec - dh.astype(F32)).astype(BF16)
    dec_x = (_dot(jnp.broadcast_to(dh, (8, HEADS)), rexp_ref[...])
             + _dot(jnp.broadcast_to(dl, (8, HEADS)), rexp_ref[...]))[0:1, :]
    assert Q == NSTATE and 2 * HEADDIM == 128
    odd_head = lax.broadcasted_iota(I32, (Q, 128), 1) >= HEADDIM
    ys = []
    for g in range(GROUPS):
        cg = cm_ref[:, g * NSTATE:(g + 1) * NSTATE]
        btg = bt_ref[g * NSTATE:(g + 1) * NSTATE, :]
        xg = xs_ref[:, g * 256:(g + 1) * 256]
        sg = s_ref[g]
        cb = _dot(cg, btg)
        sgb = sg.astype(BF16)
        cg32 = cg.astype(F32)
        btg32 = btg.astype(F32)
        yh = []
        uh = []
        for r in range(HPG):
            h = g * HPG + r
            hs = slice((r // 2) * 128, (r // 2 + 1) * 128)
            colb = cum_c[:, h:h + 1]
            rowb = cum_r[h:h + 1, :]
            decay = jnp.exp(jnp.where(lower, colb - rowb, NEG))
            m = (cb * decay * dtr[h:h + 1, :]).astype(BF16)
            cs = (cg32 * jnp.exp(colb)).astype(BF16)
            rhs = jnp.concatenate([xg[:, hs], sgb[:, hs]], axis=0)
            res = _dot(jnp.concatenate([m, cs], axis=1), rhs)
            bw = (btg32 * rfac[h:h + 1, :]).astype(BF16)
            upd = _dot(bw, xg[:, hs])
            if r % 2 == 0:
                yh.append(res)
                uh.append(upd)
            else:
                yh[-1] = jnp.where(odd_head, res, yh[-1])
                uh[-1] = jnp.where(odd_head, upd, uh[-1])
        s_ref[g] = sg * dec_x[:, g * 256:(g + 1) * 256] + jnp.concatenate(uh, axis=1)
        ys.append(jnp.concatenate(yh, axis=1))
    return ys


def _ssd_bwd_kernel(xs_ref, cm_ref, bt_ref, dt_ref, dtT_ref, arow_ref, acol_ref, rexp_ref,
                    y_ref, s_ref):
    @pl.when(pl.program_id(1) == 0)
    def _():
        s_ref[...] = jnp.zeros_like(s_ref)
    ys = _ssd_chunk(1, xs_ref, cm_ref, bt_ref, dt_ref, dtT_ref, arow_ref, acol_ref, rexp_ref, s_ref)
    for g in range(GROUPS):
        y_ref[:, g * 256:(g + 1) * 256] = ys[g].astype(BF16)


def _ssd_fwd_kernel(xs_ref, cm_ref, bt_ref, dt_ref, dtT_ref, arow_ref, acol_ref, rexp_ref,
                    z_ref, yb_ref, dsk_ref, nw_ref, y_ref, s_ref):
    @pl.when(pl.program_id(1) == 0)
    def _():
        s_ref[...] = jnp.zeros_like(s_ref)
    ys = _ssd_chunk(0, xs_ref, cm_ref, bt_ref, dt_ref, dtT_ref, arow_ref, acol_ref, rexp_ref, s_ref)
    for g in range(GROUPS):
        gs = slice(g * 256, (g + 1) * 256)
        y = ys[g] + yb_ref[:, gs].astype(F32) + xs_ref[:, gs].astype(F32) * dsk_ref[:, gs]
        y = y * _silu(z_ref[:, gs].astype(F32))
        y = y * lax.rsqrt(jnp.mean(y * y, axis=-1, keepdims=True) + EPS) * nw_ref[:, gs]
        y_ref[:, gs] = y.astype(BF16)


def _ssd_specs(cmap):
    gn = GROUPS * NSTATE
    full = lambda b, j: (0, 0)
    return [pl.BlockSpec((None, Q, D_INNER), lambda b, j: (b, cmap(j), 0)),
            pl.BlockSpec((None, Q, gn), lambda b, j: (b, cmap(j), 0)),
            pl.BlockSpec((None, gn, Q), lambda b, j: (b, 0, cmap(j))),
            pl.BlockSpec((None, Q, 2 * HEADS), lambda b, j: (b, cmap(j), 0)),
            pl.BlockSpec((None, 2 * HEADS, Q), lambda b, j: (b, 0, cmap(j))),
            pl.BlockSpec((1, 2 * HEADS), full),
            pl.BlockSpec((2 * HEADS, 1), full),
            pl.BlockSpec((HEADS, D_INNER), full)]


def _ssd_bwd(xs, cm, bt, dt, dtT, arow, acol, rexp):
    cmap = lambda j: jnp.where(j < CTX_CHUNKS, CTX_CHUNKS - 1 - j, NCHUNK + CTX_CHUNKS - 1 - j)
    omap = lambda b, j: (b, NCHUNK - 1 - jnp.maximum(j, CTX_CHUNKS), 0)
    return pl.pallas_call(
        _ssd_bwd_kernel,
        grid=(BATCH, NCHUNK),
        in_specs=_ssd_specs(cmap),
        out_specs=pl.BlockSpec((None, Q, D_INNER), omap),
        out_shape=jax.ShapeDtypeStruct((BATCH, SEQ, D_INNER), BF16),
        scratch_shapes=[pltpu.VMEM((GROUPS, NSTATE, HPG * HEADDIM), F32)],
        compiler_params=_cp(("arbitrary", "arbitrary")),
        name="ssd_bwd",
    )(xs, cm, bt, dt, dtT, arow, acol, rexp)


def _ssd_fwd(xs, cm, bt, dt, dtT, arow, acol, rexp, z, yb, dsk, nw):
    cmap = lambda j: j
    lat = lambda b, j: (b, jnp.maximum(j - CTX_CHUNKS, 0), 0)
    full = lambda b, j: (0, 0)
    return pl.pallas_call(
        _ssd_fwd_kernel,
        grid=(BATCH, NCHUNK),
        in_specs=_ssd_specs(cmap) + [
            pl.BlockSpec((None, Q, D_INNER), lambda b, j: (b, j, 0)),
            pl.BlockSpec((None, Q, D_INNER), lat),
            pl.BlockSpec((1, D_INNER), full),
            pl.BlockSpec((1, D_INNER), full)],
        out_specs=pl.BlockSpec((None, Q, D_INNER), lat),
        out_shape=jax.ShapeDtypeStruct((BATCH, SEQ, D_INNER), BF16),
        scratch_shapes=[pltpu.VMEM((GROUPS, NSTATE, HPG * HEADDIM), F32)],
        compiler_params=_cp(("arbitrary", "arbitrary")),
        name="ssd_fwd",
    )(xs, cm, bt, dt, dtT, arow, acol, rexp, z, yb, dsk, nw)


def _route(h, rwT_ref, rb_ref, cnt_ref, eidx_ref, pos_ref, gate_ref):
    hh = h.astype(BF16)
    hl = (h - hh.astype(F32)).astype(BF16)
    w = rwT_ref[...]
    wh = w.astype(BF16)
    wl = (w - wh.astype(F32)).astype(BF16)
    logits = _dot_nt(wh, hh) + _dot_nt(wh, hl) + _dot_nt(wl, hh)
    scores = jax.nn.sigmoid(logits)
    sel = scores + rb_ref[...]
    per = E // NGRP
    sub = lax.broadcasted_iota(I32, (per, TM), 0)
    gscore = []
    for g in range(NGRP):
        blk = sel[g * per:(g + 1) * per, :]
        m1 = jnp.max(blk, axis=0, keepdims=True)
        first = jnp.min(jnp.where(blk == m1, sub, per), axis=0, keepdims=True)
        m2 = jnp.max(jnp.where(sub == first, -jnp.inf, blk), axis=0, keepdims=True)
        gscore.append(m1 + m2)
    masked = []
    for g in range(NGRP):
        rank = jnp.zeros((1, TM), F32)
        for o in range(NGRP):
            if o == g:
                continue
            ahead = (gscore[o] >= gscore[g]) if o < g else (gscore[o] > gscore[g])
            rank = rank + jnp.where(ahead, 1.0, 0.0)
        blk = sel[g * per:(g + 1) * per, :]
        masked.append(jnp.where(rank < TOPG, blk, -jnp.inf))
    v = jnp.concatenate(masked, axis=0)
    eio = lax.broadcasted_iota(I32, (E, TM), 0)
    kio = lax.broadcasted_iota(I32, (TOPK, TM), 0)
    eidx = jnp.zeros((TOPK, TM), I32)
    gsc = jnp.zeros((TOPK, TM), F32)
    hot = jnp.zeros((E, TM), F32)
    picks = []
    for k in range(TOPK):
        m = jnp.max(v, axis=0, keepdims=True)
        first = jnp.min(jnp.where(v == m, eio, E), axis=0, keepdims=True)
        pick = eio == first
        sc = jnp.sum(jnp.where(pick, scores, 0.0), axis=0, keepdims=True)
        eidx = jnp.where(kio == k, first, eidx)
        gsc = jnp.where(kio == k, sc, gsc)
        hot = jnp.where(pick, 1.0, hot)
        v = jnp.where(pick, -jnp.inf, v)
        picks.append(pick)
    gate = gsc / jnp.sum(gsc, axis=0, keepdims=True) * ROUTED_SCALE
    ti = lax.broadcasted_iota(I32, (TM, TM), 0)
    tj = lax.broadcasted_iota(I32, (TM, TM), 1)
    before = jnp.where(ti < tj, 1.0, 0.0).astype(BF16)
    posfull = _dot(hot.astype(BF16), before) + cnt_ref[...]
    pos = jnp.zeros((TOPK, TM), F32)
    for k in range(TOPK):
        pk = jnp.sum(jnp.where(picks[k], posfull, 0.0), axis=0, keepdims=True)
        pos = jnp.where(kio == k, pk, pos)
    cnt_ref[...] = cnt_ref[...] + jnp.sum(hot, axis=1, keepdims=True)
    eidx_ref[...] = eidx
    pos_ref[...] = pos.astype(I32)
    eye = jnp.where(ti == tj, 1.0, 0.0).astype(BF16)
    g1, g2, g3 = _split3(gate)
    gate_ref[...] = _dot_nt(eye, g1) + _dot_nt(eye, g2) + _dot_nt(eye, g3)


def _mix_epilogue(y, x_ref, mod_ref, gpost_ref, gpre_ref, rwT_ref, rb_ref,
                  x1_ref, h2_ref, eidx_ref, pos_ref, gate_ref, cnt_out_ref, cnt_ref):
    first = (pl.program_id(0) == 0) & (pl.program_id(1) == 0)

    @pl.when(first)
    def _():
        cnt_ref[...] = jnp.zeros_like(cnt_ref)
    x1 = x_ref[...] + mod_ref[2:3, :] * _rms(y, gpost_ref[...])
    x1_ref[...] = x1
    h2 = _rms(x1, gpre_ref[...]) * (1.0 + mod_ref[4:5, :]) + mod_ref[3:4, :]
    _store_packed(h2_ref, h2)
    _route(h2, rwT_ref, rb_ref, cnt_ref, eidx_ref, pos_ref, gate_ref)
    cnt_out_ref[...] = cnt_ref[...]


def _ssm_out_kernel(y_ref, w_ref, x_ref, mod_ref, gpost_ref, gpre_ref, rwT_ref, rb_ref,
                    x1_ref, h2_ref, eidx_ref, pos_ref, gate_ref, cnt_out_ref, cnt_ref):
    y = _dot(y_ref[...], w_ref[...])
    _mix_epilogue(y, x_ref, mod_ref, gpost_ref, gpre_ref, rwT_ref, rb_ref,
                  x1_ref, h2_ref, eidx_ref, pos_ref, gate_ref, cnt_out_ref, cnt_ref)


_CW = 256
_PAD = 16


def _conf_out_kernel(u_ref, dww_ref, dwb_ref, lng_ref, lnb_ref, w_ref, b_ref,
                     x_ref, mod_ref, gpost_ref, gpre_ref, rwT_ref, rb_ref,
                     x1_ref, h2_ref, eidx_ref, pos_ref, gate_ref, cnt_out_ref,
                     cnt_ref, ext_ref, v_ref):
    nrow = TM // GRID_W
    zpad = jnp.zeros((_PAD, D), F32)
    for r in range(nrow):
        base = r * (GRID_W + 2 * _PAD)
        ext_ref[base:base + _PAD, :] = zpad
        ext_ref[base + _PAD:base + _PAD + GRID_W, :] = u_ref[r * GRID_W:(r + 1) * GRID_W, :].astype(F32)
        ext_ref[base + _PAD + GRID_W:base + 2 * _PAD + GRID_W, :] = zpad
    for r in range(nrow):
        base = r * (GRID_W + 2 * _PAD)
        for c in range(D // _CW):
            cs = slice(c * _CW, (c + 1) * _CW)
            acc = jnp.broadcast_to(dwb_ref[:, cs], (GRID_W, _CW))
            first = base + _PAD - CONF_K // 2
            span = GRID_W + 8 * ((CONF_K - 1) // 8)
            for b in range(8):
                win = ext_ref[first + b:first + b + span, cs]
                for a in range(-(-CONF_K // 8)):
                    k = 8 * a + b
                    if k < CONF_K:
                        wk = jnp.concatenate([dww_ref[8 * k:8 * k + 8, cs]] * (GRID_W // 8), axis=0)
                        acc = acc + wk * win[8 * a:8 * a + GRID_W, :]
            v_ref[r * GRID_W:(r + 1) * GRID_W, cs] = acc
    v = v_ref[...]
    mu = jnp.mean(v, axis=-1, keepdims=True)
    vc = v - mu
    ln = vc * lax.rsqrt(jnp.mean(vc * vc, axis=-1, keepdims=True) + EPS) * lng_ref[...] + lnb_ref[...]
    y = _dot(_silu(ln).astype(BF16), w_ref[...]) + b_ref[...]
    _mix_epilogue(y, x_ref, mod_ref, gpost_ref, gpre_ref, rwT_ref, rb_ref,
                  x1_ref, h2_ref, eidx_ref, pos_ref, gate_ref, cnt_out_ref, cnt_ref)


def _mix_out_common_specs():
    full = lambda b, i: (0, 0)
    nt = SEQ // TM
    in_specs = [pl.BlockSpec((None, TM, D), lambda b, i: (b, i, 0)),
                pl.BlockSpec((None, 6, D), lambda b, i: (b, 0, 0)),
                pl.BlockSpec((1, D), full), pl.BlockSpec((1, D), full),
                pl.BlockSpec((E, D), full), pl.BlockSpec((E, 1), full)]
    out_specs = [pl.BlockSpec((None, TM, D), lambda b, i: (b, i, 0)),
                 pl.BlockSpec((TM * RW, 128), lambda b, i: (b * nt + i, 0)),
                 pl.BlockSpec((TOPK, TM), lambda b, i: (0, b * nt + i)),
                 pl.BlockSpec((TOPK, TM), lambda b, i: (0, b * nt + i)),
                 pl.BlockSpec((TM, TOPK), lambda b, i: (b * nt + i, 0)),
                 pl.BlockSpec((E, 1), full)]
    out_shape = [jax.ShapeDtypeStruct((BATCH, SEQ, D), F32),
                 jax.ShapeDtypeStruct((T * RW, 128), U32),
                 jax.ShapeDtypeStruct((TOPK, T), I32),
                 jax.ShapeDtypeStruct((TOPK, T), I32),
                 jax.ShapeDtypeStruct((T, TOPK), F32),
                 jax.ShapeDtypeStruct((E, 1), F32)]
    return in_specs, out_specs, out_shape


def _ssm_out(y, w, x, mod, gpost, gpre, rwT, rb):
    common_in, out_specs, out_shape = _mix_out_common_specs()
    return pl.pallas_call(
        _ssm_out_kernel,
        grid=(BATCH, SEQ // TM),
        in_specs=[pl.BlockSpec((None, TM, D_INNER), lambda b, i: (b, i, 0)),
                  pl.BlockSpec((D_INNER, D), lambda b, i: (0, 0))] + common_in,
        out_specs=out_specs, out_shape=out_shape,
        scratch_shapes=[pltpu.VMEM((E, 1), F32)],
        compiler_params=_cp(("arbitrary", "arbitrary")),
        name="ssm_out",
    )(y, w, x, mod, gpost, gpre, rwT, rb)


def _conf_out(u, dww, dwb, lng, lnb, w, b, x, mod, gpost, gpre, rwT, rb):
    common_in, out_specs, out_shape = _mix_out_common_specs()
    full = lambda b_, i: (0, 0)
    nrow = TM // GRID_W
    return pl.pallas_call(
        _conf_out_kernel,
        grid=(BATCH, SEQ // TM),
        in_specs=[pl.BlockSpec((None, TM, D), lambda b_, i: (b_, i, 0)),
                  pl.BlockSpec((CONF_K * 8, D), full), pl.BlockSpec((1, D), full),
                  pl.BlockSpec((1, D), full), pl.BlockSpec((1, D), full),
                  pl.BlockSpec((D, D), full), pl.BlockSpec((1, D), full)] + common_in,
        out_specs=out_specs, out_shape=out_shape,
        scratch_shapes=[pltpu.VMEM((E, 1), F32),
                        pltpu.VMEM((nrow * (GRID_W + 2 * _PAD), D), F32),
                        pltpu.VMEM((TM, D), F32)],
        compiler_params=_cp(("arbitrary", "arbitrary")),
        name="conf_out",
    )(u, dww, dwb, lng, lnb, w, b, x, mod, gpost, gpre, rwT, rb)


def _conf_in_kernel(x_ref, mod_ref, g_ref, wa_ref, wg_ref, ba_ref, bg_ref, u_ref):
    h = _rms(x_ref[...], g_ref[...]) * (1.0 + mod_ref[1:2, :]) + mod_ref[0:1, :]
    hb = h.astype(BF16)
    a = _dot(hb, wa_ref[...]) + ba_ref[...]
    g = _dot(hb, wg_ref[...]) + bg_ref[...]
    u_ref[...] = (a * jax.nn.sigmoid(g)).astype(BF16)


def _conf_in(x, mod, gain, w, bias):
    full = lambda b, i: (0, 0)
    return pl.pallas_call(
        _conf_in_kernel,
        grid=(BATCH, SEQ // TM),
        in_specs=[pl.BlockSpec((None, TM, D), lambda b, i: (b, i, 0)),
                  pl.BlockSpec((None, 6, D), lambda b, i: (b, 0, 0)),
                  pl.BlockSpec((1, D), full),
                  pl.BlockSpec((D, D), full), pl.BlockSpec((D, D), lambda b, i: (0, 1)),
                  pl.BlockSpec((1, D), full), pl.BlockSpec((1, D), lambda b, i: (0, 1))],
        out_specs=pl.BlockSpec((None, TM, D), lambda b, i: (b, i, 0)),
        out_shape=jax.ShapeDtypeStruct((BATCH, SEQ, D), BF16),
        compiler_params=_cp(("arbitrary", "arbitrary")),
        name="conf_in",
    )(x, mod, gain, w, w, bias, bias)


def _row_copy(src, si, dst, di, sem):
    return pltpu.make_async_copy(src.at[pl.ds(pl.multiple_of(si * RW, RW), RW), :],
                                 dst.at[pl.ds(pl.multiple_of(di * RW, RW), RW), :], sem)


ZROWS = BLK + 8


def _dispatch_kernel(zs_ref, dest_ref, h_ref, xs_ref, zero_ref, zsem, sem):
    @pl.when(pl.program_id(0) == 0)
    def _():
        zero_ref[...] = jnp.zeros_like(zero_ref)

        def zcopy(e):
            start = pl.multiple_of(zs_ref[e] * RW, 8 * RW)
            return pltpu.make_async_copy(zero_ref, xs_ref.at[pl.ds(start, ZROWS * RW), :], zsem)

        def zstart(e, c):
            zcopy(e).start()
            return c

        def zwait(e, c):
            zcopy(e).wait()
            return c
        lax.fori_loop(0, E, zstart, 0)
        lax.fori_loop(0, E, zwait, 0)

    def issue(t, c):
        for k in range(TOPK):
            _row_copy(h_ref, t, xs_ref, dest_ref[t * TOPK + k], sem).start(priority=k % 2)
        return c

    def drain(t, c):
        for k in range(TOPK):
            _row_copy(h_ref, t, xs_ref, 0, sem).wait()
        return c
    lax.fori_loop(0, TM, issue, 0)
    lax.fori_loop(0, TM, drain, 0)


def _dispatch(zstart, dest, hp):
    grid_spec = pltpu.PrefetchScalarGridSpec(
        num_scalar_prefetch=1,
        grid=(T // TM,),
        in_specs=[pl.BlockSpec((TM * TOPK,), lambda i, zs: (i,), memory_space=pltpu.SMEM),
                  pl.BlockSpec((TM * RW, 128), lambda i, zs: (i, 0))],
        out_specs=pl.BlockSpec(memory_space=pl.ANY),
        scratch_shapes=[pltpu.VMEM((ZROWS * RW, 128), U32), pltpu.SemaphoreType.DMA(()),
                        pltpu.SemaphoreType.DMA(())])
    return pl.pallas_call(
        _dispatch_kernel,
        grid_spec=grid_spec,
        out_shape=jax.ShapeDtypeStruct((PROWS * RW, 128), U32),
        compiler_params=_cp(("arbitrary",)),
        name="moe_dispatch",
    )(zstart, dest, hp)


def _experts_kernel(be_ref, na_ref, x_ref, wg_ref, wu_ref, wd_ref, y_ref, wgb_ref, wub_ref, wdb_ref):
    i = pl.program_id(0)

    @pl.when((i == 0) | (be_ref[i] != be_ref[jnp.maximum(i - 1, 0)]))
    def _():
        wgb_ref[...] = wg_ref[...].astype(BF16)
        wub_ref[...] = wu_ref[...].astype(BF16)
        wdb_ref[...] = wd_ref[...].astype(BF16)

    @pl.when(i < na_ref[0])
    def _():
        xb = _load_packed(x_ref, BLK)
        g = _dot(xb, wgb_ref[...])
        u = _dot(xb, wub_ref[...])
        a = (_silu(g) * u).astype(BF16)
        _store_packed(y_ref, _dot(a, wdb_ref[...]))

    @pl.when(i >= na_ref[0])
    def _():
        y_ref[...] = jnp.zeros_like(y_ref)


def _experts(layer, block_e, nact, xs, wg, wu, wd):
    xmap = lambda i, be, na: (jnp.minimum(i, na[0] - 1), 0)
    wmap = lambda i, be, na: (layer, be[i], 0, 0)
    grid_spec = pltpu.PrefetchScalarGridSpec(
        num_scalar_prefetch=2,
        grid=(NBLK,),
        in_specs=[pl.BlockSpec((BLK * RW, 128), xmap),
                  pl.BlockSpec((None, None, D, DE), wmap),
                  pl.BlockSpec((None, None, D, DE), wmap),
                  pl.BlockSpec((None, None, DE, D), wmap)],
        out_specs=pl.BlockSpec((BLK * RW, 128), lambda i, be, na: (i, 0)),
        scratch_shapes=[pltpu.VMEM((D, DE), BF16), pltpu.VMEM((D, DE), BF16),
                        pltpu.VMEM((DE, D), BF16)])
    return pl.pallas_call(
        _experts_kernel,
        grid_spec=grid_spec,
        out_shape=jax.ShapeDtypeStruct((NBLK * BLK * RW, 128), U32),
        compiler_params=_cp(("arbitrary",)),
        name="moe_experts",
    )(block_e, nact, xs, wg, wu, wd)


def _combine_kernel(dcur_ref, dnxt_ref, gate_ref, h_ref, x1_ref, mod_ref, gpost_ref,
                    sg_ref, su_ref, sd_ref, ys_ref, o_ref, buf_ref, sems):
    i = pl.program_id(0)
    n = pl.num_programs(0)

    def gather(dest_ref, slot):
        def body(t, c):
            for k in range(TOPK):
                _row_copy(ys_ref, dest_ref[t * TOPK + k], buf_ref.at[slot, k], t,
                          sems.at[slot]).start(priority=k % 2)
            return c
        lax.fori_loop(0, TM, body, 0)

    @pl.when(i == 0)
    def _():
        gather(dcur_ref, 0)

    @pl.when(i + 1 < n)
    def _():
        gather(dnxt_ref, (i + 1) % 2)

    hb = _load_packed(h_ref, TM)
    a = (_silu(_dot(hb, sg_ref[...])) * _dot(hb, su_ref[...])).astype(BF16)
    f = _dot(a, sd_ref[...])

    slot = i % 2

    def drain(t, c):
        for k in range(TOPK):
            _row_copy(ys_ref, 0, buf_ref.at[slot, k], 0, sems.at[slot]).wait()
        return c
    lax.fori_loop(0, TM, drain, 0)

    gate = gate_ref[...]
    gb = [jnp.broadcast_to(gate[:, k:k + 1], (TM, 128)) for k in range(TOPK)]
    los, his = [], []
    for s in range(RW):
        lo = f[:, s * 128:(s + 1) * 128]
        hi = f[:, D // 2 + s * 128:D // 2 + (s + 1) * 128]
        for k in range(TOPK):
            wl, wh = _unpack2(buf_ref[slot, k, pl.ds(s, TM, stride=RW), :])
            lo = lo + gb[k] * wl
            hi = hi + gb[k] * wh
        los.append(lo)
        his.append(hi)
    f = jnp.concatenate(los + his, axis=1)
    o_ref[...] = x1_ref[...] + mod_ref[5:6, :] * _rms(f, gpost_ref[...])


def _combine(dest, gate, hp, x1, mod, gpost, sg, su, sd, ys):
    nt = SEQ // TM
    n = T // TM
    full = lambda i: (0, 0)
    return pl.pallas_call(
        _combine_kernel,
        grid=(n,),
        in_specs=[pl.BlockSpec((TM * TOPK,), lambda i: (i,), memory_space=pltpu.SMEM),
                  pl.BlockSpec((TM * TOPK,), lambda i: (jnp.minimum(i + 1, n - 1),),
                               memory_space=pltpu.SMEM),
                  pl.BlockSpec((TM, TOPK), lambda i: (i, 0)),
                  pl.BlockSpec((TM * RW, 128), lambda i: (i, 0)),
                  pl.BlockSpec((TM, D), lambda i: (i, 0)),
                  pl.BlockSpec((None, 6, D), lambda i: (i // nt, 0, 0)),
                  pl.BlockSpec((1, D), full),
                  pl.BlockSpec((D, DE), full), pl.BlockSpec((D, DE), full), pl.BlockSpec((DE, D), full),
                  pl.BlockSpec(memory_space=pl.ANY)],
        out_specs=pl.BlockSpec((TM, D), lambda i: (i, 0)),
        out_shape=jax.ShapeDtypeStruct((T, D), F32),
        scratch_shapes=[pltpu.VMEM((2, TOPK, TM * RW, 128), U32), pltpu.SemaphoreType.DMA((2,))],
        compiler_params=_cp(("arbitrary",)),
        name="moe_combine",
    )(dest, dest, gate, hp, x1.reshape(T, D), mod, gpost, sg, su, sd, ys).reshape(BATCH, SEQ, D)


def _moe(layer, x1, hp, eidx, pos, gate, counts, mod, gpost, wg, wu, wd, sg, su, sd):
    cnt = counts.reshape(E).astype(I32)
    padded = (cnt + BLK - 1) // BLK * BLK
    pend = jnp.cumsum(padded)
    pstart = pend - padded
    ids = jnp.arange(E, dtype=I32)
    dest = pos + jnp.sum(jnp.where(eidx[:, :, None] == ids, pstart, 0), axis=-1)
    dest = dest.T.reshape(T * TOPK)
    nact = (pend[-1] // BLK).reshape(1).astype(I32)
    blk0 = jnp.arange(NBLK, dtype=I32) * BLK
    block_e = jnp.sum((pend[None, :] <= blk0[:, None]).astype(I32), axis=1)
    block_e = jnp.minimum(block_e, E - 1).astype(I32)
    xs = _dispatch(((pstart + cnt) // 8 * 8).astype(I32), dest, hp)
    ys = _experts(layer, block_e, nact, xs, wg, wu, wd)
    return _combine(dest, gate, hp, x1, mod, gpost, sg.astype(BF16), su.astype(BF16),
                    sd.astype(BF16), ys)


def kernel(x, c, ctx, c_ctx, ada_w, ada_b, norm_mix_pre, norm_mix_post, norm_ffn_pre, norm_ffn_post, ssm_w_in, ssm_conv_w, ssm_conv_b, ssm_dt_bias, ssm_a_log, ssm_d, ssm_norm, ssm_w_out, cv_w_in, cv_b_in, cv_dw_w, cv_dw_b, cv_ln_g, cv_ln_b, cv_w_out, cv_b_out, router_w, router_b, exp_w_gate, exp_w_up, exp_w_down, sh_w_gate, sh_w_up, sh_w_down):
    row = lambda v: v.reshape(1, -1)
    cvec = jnp.concatenate([c, c_ctx[None, :], jnp.zeros((3, D), F32)], axis=0)
    mod = _ada(cvec, ada_w, ada_b).reshape(2, 8, 6, D)

    hcat = jnp.concatenate([ctx, x], axis=1)
    w_in = ssm_w_in[0].astype(BF16)
    dtb = ssm_dt_bias[0].reshape(1, 2 * HEADS)
    wdt = w_in[:, D_INNER + CONV_DIM:]
    z, xbc, dt, dtT = _ssm_in(hcat, mod[0], row(norm_mix_pre[0]), w_in, wdt, wdt.T,
                              dtb, dtb.reshape(2 * HEADS, 1))
    xs, bt, cm = _ssm_conv(xbc, jnp.repeat(ssm_conv_w[0], 8, axis=0), row(ssm_conv_b[0]))
    a = -jnp.exp(ssm_a_log[0].astype(F32)).reshape(1, 2 * HEADS)
    rexp = (jnp.arange(D_INNER)[None, :] // HEADDIM == jnp.arange(HEADS)[:, None]).astype(BF16)
    yb = _ssd_bwd(xs, cm, bt, dt, dtT, a, a.reshape(2 * HEADS, 1), rexp)
    dsk = jnp.repeat(ssm_d[0], HEADDIM).reshape(1, D_INNER)
    ygn = _ssd_fwd(xs, cm, bt, dt, dtT, a, a.reshape(2 * HEADS, 1), rexp, z, yb, dsk,
                   row(ssm_norm[0]))
    x1, h2, eidx, pos, gate, counts = _ssm_out(
        ygn, ssm_w_out[0].astype(BF16), x, mod[0], row(norm_mix_post[0]), row(norm_ffn_pre[0]),
        router_w[0].T, router_b[0].reshape(E, 1))
    x2 = _moe(0, x1, h2, eidx, pos, gate, counts, mod[0], row(norm_ffn_post[0]),
              exp_w_gate, exp_w_up, exp_w_down, sh_w_gate[0], sh_w_up[0], sh_w_down[0])

    u = _conf_in(x2, mod[1], row(norm_mix_pre[1]), cv_w_in[0].astype(BF16), row(cv_b_in[0]))
    x3, h4, eidx, pos, gate, counts = _conf_out(
        u, jnp.repeat(cv_dw_w[0], 8, axis=0), row(cv_dw_b[0]), row(cv_ln_g[0]), row(cv_ln_b[0]),
        cv_w_out[0].astype(BF16), row(cv_b_out[0]),
        x2, mod[1], row(norm_mix_post[1]), row(norm_ffn_pre[1]),
        router_w[1].T, router_b[1].reshape(E, 1))
    return _moe(1, x3, h4, eidx, pos, gate, counts, mod[1], row(norm_ffn_post[1]),
                exp_w_gate, exp_w_up, exp_w_down, sh_w_gate[1], sh_w_up[1], sh_w_down[1])
```

```python
import functools

import jax
import jax.numpy as jnp
from jax import lax
from jax.experimental import pallas as pl
from jax.experimental.pallas import tpu as pltpu

F32 = jnp.float32
BF16 = jnp.bfloat16
I32 = jnp.int32

D = 1024
BATCH = 4
SEQ = 4096
CTX = 256
LTOT = CTX + SEQ
GRID_W = 64

D_INNER = 2048
HEADS = 32
GROUPS = 8
HPG = 4
HEADDIM = 64
NSTATE = 128
Q = 128
NCHUNK = LTOT // Q
CTX_CHUNKS = CTX // Q
CONV_DIM = D_INNER + 2 * GROUPS * NSTATE
SSM_K = 5
CONF_K = 31

E = 64
TOPK = 8
NGRP = 8
TOPG = 4
DE = 256
ROUTED_SCALE = 2.5
EPS = 1e-6

T = BATCH * SEQ
TM = 512
BLK = 512
NBLK = -(-(T * TOPK + E * (BLK - 1)) // BLK)
PROWS = (NBLK + 1) * BLK + 8

VMEM_LIMIT = 56 * 1024 * 1024
NEG = -1e30


def _cp(sem):
    return pltpu.CompilerParams(dimension_semantics=sem, vmem_limit_bytes=VMEM_LIMIT)


def _silu(v):
    return v * jax.nn.sigmoid(v)


def _rms(v, g):
    return v * lax.rsqrt(jnp.mean(v * v, axis=-1, keepdims=True) + EPS) * g


def _split3(v):
    a = v.astype(BF16)
    r = v - a.astype(F32)
    b = r.astype(BF16)
    c = (r - b.astype(F32)).astype(BF16)
    return a, b, c


def _dot(a, b):
    return jnp.dot(a, b, preferred_element_type=F32)


def _dot_nt(a, b):
    return lax.dot_general(a, b, (((1,), (1,)), ((), ())), preferred_element_type=F32)


U32 = jnp.uint32
RW = D // 2 // 128
_HI = 0xFFFF0000


def _pack2(lo, hi):
    ul = pltpu.bitcast(lo.astype(BF16).astype(F32), U32)
    uh = pltpu.bitcast(hi.astype(BF16).astype(F32), U32)
    return (ul >> 16) | (uh & U32(_HI))


def _unpack2(w):
    return pltpu.bitcast(w << 16, F32), pltpu.bitcast(w & U32(_HI), F32)


def _store_packed(ref, v):
    m = v.shape[0]
    for s in range(RW):
        lo = v[:, s * 128:(s + 1) * 128]
        hi = v[:, D // 2 + s * 128:D // 2 + (s + 1) * 128]
        ref[pl.ds(s, m, stride=RW), :] = _pack2(lo, hi)


def _load_packed(ref, m, base=0):
    los, his = [], []
    for s in range(RW):
        lo, hi = _unpack2(ref[pl.ds(base + s, m, stride=RW), :])
        los.append(lo.astype(BF16))
        his.append(hi.astype(BF16))
    return jnp.concatenate(los + his, axis=1)


def _ada_kernel(c_ref, w_ref, b_ref, o_ref):
    s = _silu(c_ref[...])
    o_ref[...] = jnp.dot(s, w_ref[...], preferred_element_type=F32,
                         precision=lax.Precision.HIGHEST) + b_ref[...]


def _ada(cvec, ada_w, ada_b):
    depth = ada_w.shape[0]
    tn = 1536
    return pl.pallas_call(
        _ada_kernel,
        grid=(depth, 6 * D // tn),
        in_specs=[pl.BlockSpec((8, D), lambda l, j: (0, 0)),
                  pl.BlockSpec((None, D, tn), lambda l, j: (l, 0, j)),
                  pl.BlockSpec((None, 1, tn), lambda l, j: (l, 0, j))],
        out_specs=pl.BlockSpec((None, 8, tn), lambda l, j: (l, 0, j)),
        out_shape=jax.ShapeDtypeStruct((depth, 8, 6 * D), F32),
        compiler_params=_cp(("arbitrary", "arbitrary")),
        name="ada",
    )(cvec, ada_w, ada_b.reshape(depth, 1, 6 * D))


def _ssm_in_kernel(c_ref, x_ref, mod_ref, g_ref, wz_ref, wx0_ref, wx1_ref, wdt_ref, wdtT_ref, dtb_ref,
                   dtbT_ref, z_ref, xbc_ref, dt_ref, dtT_ref):
    xin = jnp.where(pl.program_id(1) == 0, c_ref[...], x_ref[...])
    h = _rms(xin, g_ref[...]) * (1.0 + mod_ref[1:2, :]) + mod_ref[0:1, :]
    hb = h.astype(BF16)
    z_ref[...] = _dot(hb, wz_ref[...]).astype(BF16)
    xbc_ref[:, :D_INNER] = _dot(hb, wx0_ref[...]).astype(BF16)
    xbc_ref[:, D_INNER:] = _dot(hb, wx1_ref[...]).astype(BF16)
    dt_ref[...] = jax.nn.softplus(_dot(hb, wdt_ref[...]) + dtb_ref[...])
    dtT_ref[...] = jax.nn.softplus(_dot_nt(wdtT_ref[...], hb) + dtbT_ref[...])


def _ssm_in(ctx, x, mod, gain, w_in, wdt, wdtT, dtb, dtbT):
    ts = CTX
    nt = LTOT // ts
    full = lambda b, i: (0, 0)
    assert CONV_DIM == 2 * D_INNER and SEQ % ts == 0
    return pl.pallas_call(
        _ssm_in_kernel,
        grid=(BATCH, nt),
        in_specs=[pl.BlockSpec((None, ts, D), lambda b, i: (b, 0, 0)),
                  pl.BlockSpec((None, ts, D), lambda b, i: (b, jnp.maximum(i - 1, 0), 0)),
                  pl.BlockSpec((None, 6, D), lambda b, i: (jnp.where(i == 0, BATCH, b), 0, 0)),
                  pl.BlockSpec((1, D), full),
                  pl.BlockSpec((D, D_INNER), lambda b, i: (0, 0)),
                  pl.BlockSpec((D, D_INNER), lambda b, i: (0, 1)),
                  pl.BlockSpec((D, D_INNER), lambda b, i: (0, 2)),
                  pl.BlockSpec((D, 2 * HEADS), full),
                  pl.BlockSpec((2 * HEADS, D), full),
                  pl.BlockSpec((1, 2 * HEADS), full),
                  pl.BlockSpec((2 * HEADS, 1), full)],
        out_specs=[pl.BlockSpec((None, ts, D_INNER), lambda b, i: (b, i, 0)),
                   pl.BlockSpec((None, ts, CONV_DIM), lambda b, i: (b, i, 0)),
                   pl.BlockSpec((None, ts, 2 * HEADS), lambda b, i: (b, i, 0)),
                   pl.BlockSpec((None, 2 * HEADS, ts), lambda b, i: (b, 0, i))],
        out_shape=[jax.ShapeDtypeStruct((BATCH, LTOT, D_INNER), BF16),
                   jax.ShapeDtypeStruct((BATCH, LTOT, CONV_DIM), BF16),
                   jax.ShapeDtypeStruct((BATCH, LTOT, 2 * HEADS), F32),
                   jax.ShapeDtypeStruct((BATCH, 2 * HEADS, LTOT), F32)],
        compiler_params=_cp(("arbitrary", "arbitrary")),
        name="ssm_in",
    )(ctx, x, mod, gain, w_in, w_in, w_in, wdt, wdtT, dtb, dtbT)


_HALO = 16
_CC = 512


def _ssm_conv_kernel(x_ref, xp_ref, xn_ref, w_ref, b_ref, xs_ref, bt_ref, cm_ref, ext_ref):
    i = pl.program_id(1)
    first_lat = CTX // Q
    pv = jnp.where((i == 0) | (i == first_lat), 0.0, 1.0)
    nv = jnp.where((i == first_lat - 1) | (i == NCHUNK - 1), 0.0, 1.0)
    for c in range(CONV_DIM // _CC):
        cs = slice(c * _CC, (c + 1) * _CC)
        ext_ref[0:8, :] = xp_ref[:, cs].astype(F32)[8:16, :] * pv
        ext_ref[8:8 + Q, :] = x_ref[:, cs].astype(F32)
        ext_ref[8 + Q:16 + Q, :] = xn_ref[:, cs].astype(F32)[0:8, :] * nv
        acc = jnp.broadcast_to(b_ref[:, cs], (Q, _CC))
        for k in range(SSM_K):
            wk = jnp.concatenate([w_ref[8 * k:8 * k + 8, cs]] * (Q // 8), axis=0)
            acc = acc + wk * ext_ref[6 + k:6 + k + Q, :]
        y = _silu(acc)
        lo = c * _CC
        if lo < D_INNER:
            xs_ref[:, cs] = y.astype(BF16)
        elif lo < D_INNER + GROUPS * NSTATE:
            o = lo - D_INNER
            bt_ref[o:o + _CC, :] = y.T.astype(BF16)
        else:
            o = lo - D_INNER - GROUPS * NSTATE
            cm_ref[:, o:o + _CC] = y.astype(BF16)


def _ssm_conv(xbc, w, b):
    nh = Q // _HALO
    last = LTOT // _HALO - 1
    gn = GROUPS * NSTATE
    return pl.pallas_call(
        _ssm_conv_kernel,
        grid=(BATCH, NCHUNK),
        in_specs=[pl.BlockSpec((None, Q, CONV_DIM), lambda b, i: (b, i, 0)),
                  pl.BlockSpec((None, _HALO, CONV_DIM), lambda b, i: (b, jnp.maximum(i * nh - 1, 0), 0)),
                  pl.BlockSpec((None, _HALO, CONV_DIM), lambda b, i: (b, jnp.minimum(i * nh + nh, last), 0)),
                  pl.BlockSpec((SSM_K * 8, CONV_DIM), lambda b, i: (0, 0)),
                  pl.BlockSpec((1, CONV_DIM), lambda b, i: (0, 0))],
        out_specs=[pl.BlockSpec((None, Q, D_INNER), lambda b, i: (b, i, 0)),
                   pl.BlockSpec((None, gn, Q), lambda b, i: (b, 0, i)),
                   pl.BlockSpec((None, Q, gn), lambda b, i: (b, i, 0))],
        out_shape=[jax.ShapeDtypeStruct((BATCH, LTOT, D_INNER), BF16),
                   jax.ShapeDtypeStruct((BATCH, gn, LTOT), BF16),
                   jax.ShapeDtypeStruct((BATCH, LTOT, gn), BF16)],
        scratch_shapes=[pltpu.VMEM((Q + 16, _CC), F32)],
        compiler_params=_cp(("arbitrary", "arbitrary")),
        name="ssm_conv",
    )(xbc, xbc, xbc, w, b)


def _ssd_chunk(direction, xs_ref, cm_ref, bt_ref, dt_ref, dtT_ref, arow_ref, acol_ref, rexp_ref, s_ref):
    d0 = direction * HEADS
    dtc = dt_ref[:, d0:d0 + HEADS]
    dtr = dtT_ref[d0:d0 + HEADS, :]
    da_c = dtc * arow_ref[:, d0:d0 + HEADS]
    da_r = dtr * acol_ref[d0:d0 + HEADS, :]
    ii = lax.broadcasted_iota(I32, (Q, Q), 0)
    jj = lax.broadcasted_iota(I32, (Q, Q), 1)
    if direction == 0:
        lower = jj <= ii
        tot_idx = Q - 1
    else:
        lower = jj >= ii
        tot_idx = 0
    tri_c = jnp.where(lower, 1.0, 0.0).astype(BF16)
    upper = (ii <= jj) if direction == 0 else (ii >= jj)
    tri_r = jnp.where(upper, 1.0, 0.0).astype(BF16)
    c1, c2, c3 = _split3(da_c)
    cum_c = _dot(tri_c, c1) + _dot(tri_c, c2) + _dot(tri_c, c3)
    r1, r2, r3 = _split3(da_r)
    cum_r = _dot(r1, tri_r) + _dot(r2, tri_r) + _dot(r3, tri_r)
    tot_c = cum_c[tot_idx:tot_idx + 1, :]
    tot_r = cum_r[:, tot_idx:tot_idx + 1]
    rfac = dtr * jnp.exp(tot_r - cum_r)
    dec = jnp.exp(tot_c)
    dh = dec.astype(BF16)
    dl = (dec - dh.astype(F32)).astype(BF16)
    dec_x = (_dot(jnp.broadcast_to(dh, (8, HEADS)), rexp_ref[...])
             + _dot(jnp.broadcast_to(dl, (8, HEADS)), rexp_ref[...]))[0:1, :]
    assert Q == NSTATE and 2 * HEADDIM == 128
    odd_head = lax.broadcasted_iota(I32, (Q, 128), 1) >= HEADDIM
    ys = []
    for g in range(GROUPS):
        cg = cm_ref[:, g * NSTATE:(g + 1) * NSTATE]
        btg = bt_ref[g * NSTATE:(g + 1) * NSTATE, :]
        xg = xs_ref[:, g * 256:(g + 1) * 256]
        sg = s_ref[g]
        cb = _dot(cg, btg)
        sgb = sg.astype(BF16)
        cg32 = cg.astype(F32)
        btg32 = btg.astype(F32)
        yh = []
        uh = []
        for r in range(HPG):
            h = g * HPG + r
            hs = slice((r // 2) * 128, (r // 2 + 1) * 128)
            colb = cum_c[:, h:h + 1]
            rowb = cum_r[h:h + 1, :]
            decay = jnp.exp(jnp.where(lower, colb - rowb, NEG))
            m = (cb * decay * dtr[h:h + 1, :]).astype(BF16)
            cs = (cg32 * jnp.exp(colb)).astype(BF16)
            rhs = jnp.concatenate([xg[:, hs], sgb[:, hs]], axis=0)
            res = _dot(jnp.concatenate([m, cs], axis=1), rhs)
            bw = (btg32 * rfac[h:h + 1, :]).astype(BF16)
            upd = _dot(bw, xg[:, hs])
            if r % 2 == 0:
                yh.append(res)
                uh.append(upd)
            else:
                yh[-1] = jnp.where(odd_head, res, yh[-1])
                uh[-1] = jnp.where(odd_head, upd, uh[-1])
        s_ref[g] = sg * dec_x[:, g * 256:(g + 1) * 256] + jnp.concatenate(uh, axis=1)
        ys.append(jnp.concatenate(yh, axis=1))
    return ys


def _ssd_bwd_kernel(xs_ref, cm_ref, bt_ref, dt_ref, dtT_ref, arow_ref, acol_ref, rexp_ref,
                    y_ref, s_ref):
    @pl.when(pl.program_id(1) == 0)
    def _():
        s_ref[...] = jnp.zeros_like(s_ref)
    ys = _ssd_chunk(1, xs_ref, cm_ref, bt_ref, dt_ref, dtT_ref, arow_ref, acol_ref, rexp_ref, s_ref)
    for g in range(GROUPS):
        y_ref[:, g * 256:(g + 1) * 256] = ys[g].astype(BF16)


def _ssd_fwd_kernel(xs_ref, cm_ref, bt_ref, dt_ref, dtT_ref, arow_ref, acol_ref, rexp_ref,
                    z_ref, yb_ref, dsk_ref, nw_ref, y_ref, s_ref):
    @pl.when(pl.program_id(1) == 0)
    def _():
        s_ref[...] = jnp.zeros_like(s_ref)
    ys = _ssd_chunk(0, xs_ref, cm_ref, bt_ref, dt_ref, dtT_ref, arow_ref, acol_ref, rexp_ref, s_ref)
    for g in range(GROUPS):
        gs = slice(g * 256, (g + 1) * 256)
        y = ys[g] + yb_ref[:, gs].astype(F32) + xs_ref[:, gs].astype(F32) * dsk_ref[:, gs]
        y = y * _silu(z_ref[:, gs].astype(F32))
        y = y * lax.rsqrt(jnp.mean(y * y, axis=-1, keepdims=True) + EPS) * nw_ref[:, gs]
        y_ref[:, gs] = y.astype(BF16)


def _ssd_specs(cmap):
    gn = GROUPS * NSTATE
    full = lambda b, j: (0, 0)
    return [pl.BlockSpec((None, Q, D_INNER), lambda b, j: (b, cmap(j), 0)),
            pl.BlockSpec((None, Q, gn), lambda b, j: (b, cmap(j), 0)),
            pl.BlockSpec((None, gn, Q), lambda b, j: (b, 0, cmap(j))),
            pl.BlockSpec((None, Q, 2 * HEADS), lambda b, j: (b, cmap(j), 0)),
            pl.BlockSpec((None, 2 * HEADS, Q), lambda b, j: (b, 0, cmap(j))),
            pl.BlockSpec((1, 2 * HEADS), full),
            pl.BlockSpec((2 * HEADS, 1), full),
            pl.BlockSpec((HEADS, D_INNER), full)]


def _ssd_bwd(xs, cm, bt, dt, dtT, arow, acol, rexp):
    cmap = lambda j: jnp.where(j < CTX_CHUNKS, CTX_CHUNKS - 1 - j, NCHUNK + CTX_CHUNKS - 1 - j)
    omap = lambda b, j: (b, NCHUNK - 1 - jnp.maximum(j, CTX_CHUNKS), 0)
    return pl.pallas_call(
        _ssd_bwd_kernel,
        grid=(BATCH, NCHUNK),
        in_specs=_ssd_specs(cmap),
        out_specs=pl.BlockSpec((None, Q, D_INNER), omap),
        out_shape=jax.ShapeDtypeStruct((BATCH, SEQ, D_INNER), BF16),
        scratch_shapes=[pltpu.VMEM((GROUPS, NSTATE, HPG * HEADDIM), F32)],
        compiler_params=_cp(("arbitrary", "arbitrary")),
        name="ssd_bwd",
    )(xs, cm, bt, dt, dtT, arow, acol, rexp)


def _ssd_fwd(xs, cm, bt, dt, dtT, arow, acol, rexp, z, yb, dsk, nw):
    cmap = lambda j: j
    lat = lambda b, j: (b, jnp.maximum(j - CTX_CHUNKS, 0), 0)
    full = lambda b, j: (0, 0)
    return pl.pallas_call(
        _ssd_fwd_kernel,
        grid=(BATCH, NCHUNK),
        in_specs=_ssd_specs(cmap) + [
            pl.BlockSpec((None, Q, D_INNER), lambda b, j: (b, j, 0)),
            pl.BlockSpec((None, Q, D_INNER), lat),
            pl.BlockSpec((1, D_INNER), full),
            pl.BlockSpec((1, D_INNER), full)],
        out_specs=pl.BlockSpec((None, Q, D_INNER), lat),
        out_shape=jax.ShapeDtypeStruct((BATCH, SEQ, D_INNER), BF16),
        scratch_shapes=[pltpu.VMEM((GROUPS, NSTATE, HPG * HEADDIM), F32)],
        compiler_params=_cp(("arbitrary", "arbitrary")),
        name="ssd_fwd",
    )(xs, cm, bt, dt, dtT, arow, acol, rexp, z, yb, dsk, nw)


def _route(h, rwT_ref, rb_ref, cnt_ref, eidx_ref, pos_ref, gate_ref):
    hh = h.astype(BF16)
    hl = (h - hh.astype(F32)).astype(BF16)
    w = rwT_ref[...]
    wh = w.astype(BF16)
    wl = (w - wh.astype(F32)).astype(BF16)
    logits = _dot_nt(wh, hh) + _dot_nt(wh, hl) + _dot_nt(wl, hh)
    scores = jax.nn.sigmoid(logits)
    sel = scores + rb_ref[...]
    per = E // NGRP
    sub = lax.broadcasted_iota(I32, (per, TM), 0)
    gscore = []
    for g in range(NGRP):
        blk = sel[g * per:(g + 1) * per, :]
        m1 = jnp.max(blk, axis=0, keepdims=True)
        first = jnp.min(jnp.where(blk == m1, sub, per), axis=0, keepdims=True)
        m2 = jnp.max(jnp.where(sub == first, -jnp.inf, blk), axis=0, keepdims=True)
        gscore.append(m1 + m2)
    masked = []
    for g in range(NGRP):
        rank = jnp.zeros((1, TM), F32)
        for o in range(NGRP):
            if o == g:
                continue
            ahead = (gscore[o] >= gscore[g]) if o < g else (gscore[o] > gscore[g])
            rank = rank + jnp.where(ahead, 1.0, 0.0)
        blk = sel[g * per:(g + 1) * per, :]
        masked.append(jnp.where(rank < TOPG, blk, -jnp.inf))
    v = jnp.concatenate(masked, axis=0)
    eio = lax.broadcasted_iota(I32, (E, TM), 0)
    kio = lax.broadcasted_iota(I32, (TOPK, TM), 0)
    eidx = jnp.zeros((TOPK, TM), I32)
    gsc = jnp.zeros((TOPK, TM), F32)
    hot = jnp.zeros((E, TM), F32)
    picks = []
    for k in range(TOPK):
        m = jnp.max(v, axis=0, keepdims=True)
        first = jnp.min(jnp.where(v == m, eio, E), axis=0, keepdims=True)
        pick = eio == first
        sc = jnp.sum(jnp.where(pick, scores, 0.0), axis=0, keepdims=True)
        eidx = jnp.where(kio == k, first, eidx)
        gsc = jnp.where(kio == k, sc, gsc)
        hot = jnp.where(pick, 1.0, hot)
        v = jnp.where(pick, -jnp.inf, v)
        picks.append(pick)
    gate = gsc / jnp.sum(gsc, axis=0, keepdims=True) * ROUTED_SCALE
    ti = lax.broadcasted_iota(I32, (TM, TM), 0)
    tj = lax.broadcasted_iota(I32, (TM, TM), 1)
    before = jnp.where(ti < tj, 1.0, 0.0).astype(BF16)
    posfull = _dot(hot.astype(BF16), before) + cnt_ref[...]
    pos = jnp.zeros((TOPK, TM), F32)
    for k in range(TOPK):
        pk = jnp.sum(jnp.where(picks[k], posfull, 0.0), axis=0, keepdims=True)
        pos = jnp.where(kio == k, pk, pos)
    cnt_ref[...] = cnt_ref[...] + jnp.sum(hot, axis=1, keepdims=True)
    eidx_ref[...] = eidx
    pos_ref[...] = pos.astype(I32)
    eye = jnp.where(ti == tj, 1.0, 0.0).astype(BF16)
    g1, g2, g3 = _split3(gate)
    gate_ref[...] = _dot_nt(eye, g1) + _dot_nt(eye, g2) + _dot_nt(eye, g3)


def _mix_epilogue(y, x_ref, mod_ref, gpost_ref, gpre_ref, rwT_ref, rb_ref,
                  x1_ref, h2_ref, eidx_ref, pos_ref, gate_ref, cnt_out_ref, cnt_ref):
    first = (pl.program_id(0) == 0) & (pl.program_id(1) == 0)

    @pl.when(first)
    def _():
        cnt_ref[...] = jnp.zeros_like(cnt_ref)
    x1 = x_ref[...] + mod_ref[2:3, :] * _rms(y, gpost_ref[...])
    x1_ref[...] = x1
    h2 = _rms(x1, gpre_ref[...]) * (1.0 + mod_ref[4:5, :]) + mod_ref[3:4, :]
    _store_packed(h2_ref, h2)
    _route(h2, rwT_ref, rb_ref, cnt_ref, eidx_ref, pos_ref, gate_ref)
    cnt_out_ref[...] = cnt_ref[...]


def _ssm_out_kernel(y_ref, w_ref, x_ref, mod_ref, gpost_ref, gpre_ref, rwT_ref, rb_ref,
                    x1_ref, h2_ref, eidx_ref, pos_ref, gate_ref, cnt_out_ref, cnt_ref):
    y = _dot(y_ref[...], w_ref[...])
    _mix_epilogue(y, x_ref, mod_ref, gpost_ref, gpre_ref, rwT_ref, rb_ref,
                  x1_ref, h2_ref, eidx_ref, pos_ref, gate_ref, cnt_out_ref, cnt_ref)


_CW = 256
_PAD = 16


def _conf_out_kernel(u_ref, dww_ref, dwb_ref, lng_ref, lnb_ref, w_ref, b_ref,
                     x_ref, mod_ref, gpost_ref, gpre_ref, rwT_ref, rb_ref,
                     x1_ref, h2_ref, eidx_ref, pos_ref, gate_ref, cnt_out_ref,
                     cnt_ref, ext_ref, v_ref):
    nrow = TM // GRID_W
    zpad = jnp.zeros((_PAD, D), F32)
    for r in range(nrow):
        base = r * (GRID_W + 2 * _PAD)
        ext_ref[base:base + _PAD, :] = zpad
        ext_ref[base + _PAD:base + _PAD + GRID_W, :] = u_ref[r * GRID_W:(r + 1) * GRID_W, :].astype(F32)
        ext_ref[base + _PAD + GRID_W:base + 2 * _PAD + GRID_W, :] = zpad
    for r in range(nrow):
        base = r * (GRID_W + 2 * _PAD)
        for c in range(D // _CW):
            cs = slice(c * _CW, (c + 1) * _CW)
            acc = jnp.broadcast_to(dwb_ref[:, cs], (GRID_W, _CW))
            first = base + _PAD - CONF_K // 2
            span = GRID_W + 8 * ((CONF_K - 1) // 8)
            for b in range(8):
                win = ext_ref[first + b:first + b + span, cs]
                for a in range(-(-CONF_K // 8)):
                    k = 8 * a + b
                    if k < CONF_K:
                        wk = jnp.concatenate([dww_ref[8 * k:8 * k + 8, cs]] * (GRID_W // 8), axis=0)
                        acc = acc + wk * win[8 * a:8 * a + GRID_W, :]
            v_ref[r * GRID_W:(r + 1) * GRID_W, cs] = acc
    v = v_ref[...]
    mu = jnp.mean(v, axis=-1, keepdims=True)
    vc = v - mu
    ln = vc * lax.rsqrt(jnp.mean(vc * vc, axis=-1, keepdims=True) + EPS) * lng_ref[...] + lnb_ref[...]
    y = _dot(_silu(ln).astype(BF16), w_ref[...]) + b_ref[...]
    _mix_epilogue(y, x_ref, mod_ref, gpost_ref, gpre_ref, rwT_ref, rb_ref,
                  x1_ref, h2_ref, eidx_ref, pos_ref, gate_ref, cnt_out_ref, cnt_ref)


def _mix_out_common_specs():
    full = lambda b, i: (0, 0)
    nt = SEQ // TM
    in_specs = [pl.BlockSpec((None, TM, D), lambda b, i: (b, i, 0)),
                pl.BlockSpec((None, 6, D), lambda b, i: (b, 0, 0)),
                pl.BlockSpec((1, D), full), pl.BlockSpec((1, D), full),
                pl.BlockSpec((E, D), full), pl.BlockSpec((E, 1), full)]
    out_specs = [pl.BlockSpec((None, TM, D), lambda b, i: (b, i, 0)),
                 pl.BlockSpec((TM * RW, 128), lambda b, i: (b * nt + i, 0)),
                 pl.BlockSpec((TOPK, TM), lambda b, i: (0, b * nt + i)),
                 pl.BlockSpec((TOPK, TM), lambda b, i: (0, b * nt + i)),
                 pl.BlockSpec((TM, TOPK), lambda b, i: (b * nt + i, 0)),
                 pl.BlockSpec((E, 1), full)]
    out_shape = [jax.ShapeDtypeStruct((BATCH, SEQ, D), F32),
                 jax.ShapeDtypeStruct((T * RW, 128), U32),
                 jax.ShapeDtypeStruct((TOPK, T), I32),
                 jax.ShapeDtypeStruct((TOPK, T), I32),
                 jax.ShapeDtypeStruct((T, TOPK), F32),
                 jax.ShapeDtypeStruct((E, 1), F32)]
    return in_specs, out_specs, out_shape


def _ssm_out(y, w, x, mod, gpost, gpre, rwT, rb):
    common_in, out_specs, out_shape = _mix_out_common_specs()
    return pl.pallas_call(
        _ssm_out_kernel,
        grid=(BATCH, SEQ // TM),
        in_specs=[pl.BlockSpec((None, TM, D_INNER), lambda b, i: (b, i, 0)),
                  pl.BlockSpec((D_INNER, D), lambda b, i: (0, 0))] + common_in,
        out_specs=out_specs, out_shape=out_shape,
        scratch_shapes=[pltpu.VMEM((E, 1), F32)],
        compiler_params=_cp(("arbitrary", "arbitrary")),
        name="ssm_out",
    )(y, w, x, mod, gpost, gpre, rwT, rb)


def _conf_out(u, dww, dwb, lng, lnb, w, b, x, mod, gpost, gpre, rwT, rb):
    common_in, out_specs, out_shape = _mix_out_common_specs()
    full = lambda b_, i: (0, 0)
    nrow = TM // GRID_W
    return pl.pallas_call(
        _conf_out_kernel,
        grid=(BATCH, SEQ // TM),
        in_specs=[pl.BlockSpec((None, TM, D), lambda b_, i: (b_, i, 0)),
                  pl.BlockSpec((CONF_K * 8, D), full), pl.BlockSpec((1, D), full),
                  pl.BlockSpec((1, D), full), pl.BlockSpec((1, D), full),
                  pl.BlockSpec((D, D), full), pl.BlockSpec((1, D), full)] + common_in,
        out_specs=out_specs, out_shape=out_shape,
        scratch_shapes=[pltpu.VMEM((E, 1), F32),
                        pltpu.VMEM((nrow * (GRID_W + 2 * _PAD), D), F32),
                        pltpu.VMEM((TM, D), F32)],
        compiler_params=_cp(("arbitrary", "arbitrary")),
        name="conf_out",
    )(u, dww, dwb, lng, lnb, w, b, x, mod, gpost, gpre, rwT, rb)


def _conf_in_kernel(x_ref, mod_ref, g_ref, wa_ref, wg_ref, ba_ref, bg_ref, u_ref):
    h = _rms(x_ref[...], g_ref[...]) * (1.0 + mod_ref[1:2, :]) + mod_ref[0:1, :]
    hb = h.astype(BF16)
    a = _dot(hb, wa_ref[...]) + ba_ref[...]
    g = _dot(hb, wg_ref[...]) + bg_ref[...]
    u_ref[...] = (a * jax.nn.sigmoid(g)).astype(BF16)


def _conf_in(x, mod, gain, w, bias):
    full = lambda b, i: (0, 0)
    return pl.pallas_call(
        _conf_in_kernel,
        grid=(BATCH, SEQ // TM),
        in_specs=[pl.BlockSpec((None, TM, D), lambda b, i: (b, i, 0)),
                  pl.BlockSpec((None, 6, D), lambda b, i: (b, 0, 0)),
                  pl.BlockSpec((1, D), full),
                  pl.BlockSpec((D, D), full), pl.BlockSpec((D, D), lambda b, i: (0, 1)),
                  pl.BlockSpec((1, D), full), pl.BlockSpec((1, D), lambda b, i: (0, 1))],
        out_specs=pl.BlockSpec((None, TM, D), lambda b, i: (b, i, 0)),
        out_shape=jax.ShapeDtypeStruct((BATCH, SEQ, D), BF16),
        compiler_params=_cp(("arbitrary", "arbitrary")),
        name="conf_in",
    )(x, mod, gain, w, w, bias, bias)


def _row_copy(src, si, dst, di, sem):
    return pltpu.make_async_copy(src.at[pl.ds(pl.multiple_of(si * RW, RW), RW), :],
                                 dst.at[pl.ds(pl.multiple_of(di * RW, RW), RW), :], sem)


ZROWS = BLK + 8


def _dispatch_kernel(zs_ref, dest_ref, h_ref, xs_ref, zero_ref, zsem, sem):
    @pl.when(pl.program_id(0) == 0)
    def _():
        zero_ref[...] = jnp.zeros_like(zero_ref)

        def zcopy(e):
            start = pl.multiple_of(zs_ref[e] * RW, 8 * RW)
            return pltpu.make_async_copy(zero_ref, xs_ref.at[pl.ds(start, ZROWS * RW), :], zsem)

        def zstart(e, c):
            zcopy(e).start()
            return c

        def zwait(e, c):
            zcopy(e).wait()
            return c
        lax.fori_loop(0, E, zstart, 0)
        lax.fori_loop(0, E, zwait, 0)

    def issue(t, c):
        for k in range(TOPK):
            _row_copy(h_ref, t, xs_ref, dest_ref[t * TOPK + k], sem).start(priority=k % 2)
        return c

    def drain(t, c):
        for k in range(TOPK):
            _row_copy(h_ref, t, xs_ref, 0, sem).wait()
        return c
    lax.fori_loop(0, TM, issue, 0)
    lax.fori_loop(0, TM, drain, 0)


def _dispatch(zstart, dest, hp):
    grid_spec = pltpu.PrefetchScalarGridSpec(
        num_scalar_prefetch=1,
        grid=(T // TM,),
        in_specs=[pl.BlockSpec((TM * TOPK,), lambda i, zs: (i,), memory_space=pltpu.SMEM),
                  pl.BlockSpec((TM * RW, 128), lambda i, zs: (i, 0))],
        out_specs=pl.BlockSpec(memory_space=pl.ANY),
        scratch_shapes=[pltpu.VMEM((ZROWS * RW, 128), U32), pltpu.SemaphoreType.DMA(()),
                        pltpu.SemaphoreType.DMA(())])
    return pl.pallas_call(
        _dispatch_kernel,
        grid_spec=grid_spec,
        out_shape=jax.ShapeDtypeStruct((PROWS * RW, 128), U32),
        compiler_params=_cp(("arbitrary",)),
        name="moe_dispatch",
    )(zstart, dest, hp)


def _experts_kernel(be_ref, na_ref, x_ref, wg_ref, wu_ref, wd_ref, y_ref, wgb_ref, wub_ref, wdb_ref):
    i = pl.program_id(0)

    @pl.when((i == 0) | (be_ref[i] != be_ref[jnp.maximum(i - 1, 0)]))
    def _():
        wgb_ref[...] = wg_ref[...].astype(BF16)
        wub_ref[...] = wu_ref[...].astype(BF16)
        wdb_ref[...] = wd_ref[...].astype(BF16)

    @pl.when(i < na_ref[0])
    def _():
        xb = _load_packed(x_ref, BLK)
        g = _dot(xb, wgb_ref[...])
        u = _dot(xb, wub_ref[...])
        a = (_silu(g) * u).astype(BF16)
        _store_packed(y_ref, _dot(a, wdb_ref[...]))

    @pl.when(i >= na_ref[0])
    def _():
        y_ref[...] = jnp.zeros_like(y_ref)


def _experts(layer, block_e, nact, xs, wg, wu, wd):
    xmap = lambda i, be, na: (jnp.minimum(i, na[0] - 1), 0)
    wmap = lambda i, be, na: (layer, be[i], 0, 0)
    grid_spec = pltpu.PrefetchScalarGridSpec(
        num_scalar_prefetch=2,
        grid=(NBLK,),
        in_specs=[pl.BlockSpec((BLK * RW, 128), xmap),
                  pl.BlockSpec((None, None, D, DE), wmap),
                  pl.BlockSpec((None, None, D, DE), wmap),
                  pl.BlockSpec((None, None, DE, D), wmap)],
        out_specs=pl.BlockSpec((BLK * RW, 128), lambda i, be, na: (i, 0)),
        scratch_shapes=[pltpu.VMEM((D, DE), BF16), pltpu.VMEM((D, DE), BF16),
                        pltpu.VMEM((DE, D), BF16)])
    return pl.pallas_call(
        _experts_kernel,
        grid_spec=grid_spec,
        out_shape=jax.ShapeDtypeStruct((NBLK * BLK * RW, 128), U32),
        compiler_params=_cp(("arbitrary",)),
        name="moe_experts",
    )(block_e, nact, xs, wg, wu, wd)


def _combine_kernel(dcur_ref, dnxt_ref, gate_ref, h_ref, x1_ref, mod_ref, gpost_ref,
                    sg_ref, su_ref, sd_ref, ys_ref, o_ref, buf_ref, sems):
    i = pl.program_id(0)
    n = pl.num_programs(0)

    def gather(dest_ref, slot):
        def body(t, c):
            for k in range(TOPK):
                _row_copy(ys_ref, dest_ref[t * TOPK + k], buf_ref.at[slot, k], t,
                          sems.at[slot]).start(priority=k % 2)
            return c
        lax.fori_loop(0, TM, body, 0)

    @pl.when(i == 0)
    def _():
        gather(dcur_ref, 0)

    @pl.when(i + 1 < n)
    def _():
        gather(dnxt_ref, (i + 1) % 2)

    hb = _load_packed(h_ref, TM)
    a = (_silu(_dot(hb, sg_ref[...])) * _dot(hb, su_ref[...])).astype(BF16)
    f = _dot(a, sd_ref[...])

    slot = i % 2

    def drain(t, c):
        for k in range(TOPK):
            _row_copy(ys_ref, 0, buf_ref.at[slot, k], 0, sems.at[slot]).wait()
        return c
    lax.fori_loop(0, TM, drain, 0)

    gate = gate_ref[...]
    gb = [jnp.broadcast_to(gate[:, k:k + 1], (TM, 128)) for k in range(TOPK)]
    los, his = [], []
    for s in range(RW):
        lo = f[:, s * 128:(s + 1) * 128]
        hi = f[:, D // 2 + s * 128:D // 2 + (s + 1) * 128]
        for k in range(TOPK):
            wl, wh = _unpack2(buf_ref[slot, k, pl.ds(s, TM, stride=RW), :])
            lo = lo + gb[k] * wl
            hi = hi + gb[k] * wh
        los.append(lo)
        his.append(hi)
    f = jnp.concatenate(los + his, axis=1)
    o_ref[...] = x1_ref[...] + mod_ref[5:6, :] * _rms(f, gpost_ref[...])


def _combine(dest, gate, hp, x1, mod, gpost, sg, su, sd, ys):
    nt = SEQ // TM
    n = T // TM
    full = lambda i: (0, 0)
    return pl.pallas_call(
        _combine_kernel,
        grid=(n,),
        in_specs=[pl.BlockSpec((TM * TOPK,), lambda i: (i,), memory_space=pltpu.SMEM),
                  pl.BlockSpec((TM * TOPK,), lambda i: (jnp.minimum(i + 1, n - 1),),
                               memory_space=pltpu.SMEM),
                  pl.BlockSpec((TM, TOPK), lambda i: (i, 0)),
                  pl.BlockSpec((TM * RW, 128), lambda i: (i, 0)),
                  pl.BlockSpec((TM, D), lambda i: (i, 0)),
                  pl.BlockSpec((None, 6, D), lambda i: (i // nt, 0, 0)),
                  pl.BlockSpec((1, D), full),
                  pl.BlockSpec((D, DE), full), pl.BlockSpec((D, DE), full), pl.BlockSpec((DE, D), full),
                  pl.BlockSpec(memory_space=pl.ANY)],
        out_specs=pl.BlockSpec((TM, D), lambda i: (i, 0)),
        out_shape=jax.ShapeDtypeStruct((T, D), F32),
        scratch_shapes=[pltpu.VMEM((2, TOPK, TM * RW, 128), U32), pltpu.SemaphoreType.DMA((2,))],
        compiler_params=_cp(("arbitrary",)),
        name="moe_combine",
    )(dest, dest, gate, hp, x1.reshape(T, D), mod, gpost, sg, su, sd, ys).reshape(BATCH, SEQ, D)


def _moe(layer, x1, hp, eidx, pos, gate, counts, mod, gpost, wg, wu, wd, sg, su, sd):
    cnt = counts.reshape(E).astype(I32)
    padded = (cnt + BLK - 1) // BLK * BLK
    pend = jnp.cumsum(padded)
    pstart = pend - padded
    ids = jnp.arange(E, dtype=I32)
    dest = pos + jnp.sum(jnp.where(eidx[:, :, None] == ids, pstart, 0), axis=-1)
    dest = dest.T.reshape(T * TOPK)
    nact = (pend[-1] // BLK).reshape(1).astype(I32)
    blk0 = jnp.arange(NBLK, dtype=I32) * BLK
    block_e = jnp.sum((pend[None, :] <= blk0[:, None]).astype(I32), axis=1)
    block_e = jnp.minimum(block_e, E - 1).astype(I32)
    xs = _dispatch(((pstart + cnt) // 8 * 8).astype(I32), dest, hp)
    ys = _experts(layer, block_e, nact, xs, wg, wu, wd)
    return _combine(dest, gate, hp, x1, mod, gpost, sg.astype(BF16), su.astype(BF16),
                    sd.astype(BF16), ys)


def kernel(x, c, ctx, c_ctx, ada_w, ada_b, norm_mix_pre, norm_mix_post, norm_ffn_pre, norm_ffn_post, ssm_w_in, ssm_conv_w, ssm_conv_b, ssm_dt_bias, ssm_a_log, ssm_d, ssm_norm, ssm_w_out, cv_w_in, cv_b_in, cv_dw_w, cv_dw_b, cv_ln_g, cv_ln_b, cv_w_out, cv_b_out, router_w, router_b, exp_w_gate, exp_w_up, exp_w_down, sh_w_gate, sh_w_up, sh_w_down):
    row = lambda v: v.reshape(1, -1)
    cvec = jnp.concatenate([c, c_ctx[None, :], jnp.zeros((3, D), F32)], axis=0)
    mod = _ada(cvec, ada_w, ada_b).reshape(2, 8, 6, D)

    w_in = ssm_w_in[0].astype(BF16)
    dtb = ssm_dt_bias[0].reshape(1, 2 * HEADS)
    wdt = w_in[:, D_INNER + CONV_DIM:]
    z, xbc, dt, dtT = _ssm_in(ctx, x, mod[0], row(norm_mix_pre[0]), w_in, wdt, wdt.T,
                              dtb, dtb.reshape(2 * HEADS, 1))
    xs, bt, cm = _ssm_conv(xbc, jnp.repeat(ssm_conv_w[0], 8, axis=0), row(ssm_conv_b[0]))
    a = -jnp.exp(ssm_a_log[0].astype(F32)).reshape(1, 2 * HEADS)
    rexp = (jnp.arange(D_INNER)[None, :] // HEADDIM == jnp.arange(HEADS)[:, None]).astype(BF16)
    yb = _ssd_bwd(xs, cm, bt, dt, dtT, a, a.reshape(2 * HEADS, 1), rexp)
    dsk = jnp.repeat(ssm_d[0], HEADDIM).reshape(1, D_INNER)
    ygn = _ssd_fwd(xs, cm, bt, dt, dtT, a, a.reshape(2 * HEADS, 1), rexp, z, yb, dsk,
                   row(ssm_norm[0]))
    x1, h2, eidx, pos, gate, counts = _ssm_out(
        ygn, ssm_w_out[0].astype(BF16), x, mod[0], row(norm_mix_post[0]), row(norm_ffn_pre[0]),
        router_w[0].T, router_b[0].reshape(E, 1))
    x2 = _moe(0, x1, h2, eidx, pos, gate, counts, mod[0], row(norm_ffn_post[0]),
              exp_w_gate, exp_w_up, exp_w_down, sh_w_gate[0], sh_w_up[0], sh_w_down[0])

    u = _conf_in(x2, mod[1], row(norm_mix_pre[1]), cv_w_in[0].astype(BF16), row(cv_b_in[0]))
    x3, h4, eidx, pos, gate, counts = _conf_out(
        u, jnp.repeat(cv_dw_w[0], 8, axis=0), row(cv_dw_b[0]), row(cv_ln_g[0]), row(cv_ln_b[0]),
        cv_w_out[0].astype(BF16), row(cv_b_out[0]),
        x2, mod[1], row(norm_mix_post[1]), row(norm_ffn_pre[1]),
        router_w[1].T, router_b[1].reshape(E, 1))
    return _moe(1, x3, h4, eidx, pos, gate, counts, mod[1], row(norm_ffn_post[1]),
                exp_w_gate, exp_w_up, exp_w_down, sh_w_gate[1], sh_w_up[1], sh_w_down[1])
```

```python
import functools

import jax
import jax.numpy as jnp
from jax import lax
from jax.experimental import pallas as pl
from jax.experimental.pallas import tpu as pltpu

F32 = jnp.float32
BF16 = jnp.bfloat16
I32 = jnp.int32

D = 1024
BATCH = 4
SEQ = 4096
CTX = 256
LTOT = CTX + SEQ
GRID_W = 64

D_INNER = 2048
HEADS = 32
GROUPS = 8
HPG = 4
HEADDIM = 64
NSTATE = 128
Q = 128
NCHUNK = LTOT // Q
CTX_CHUNKS = CTX // Q
CONV_DIM = D_INNER + 2 * GROUPS * NSTATE
SSM_K = 5
CONF_K = 31

E = 64
TOPK = 8
NGRP = 8
TOPG = 4
DE = 256
ROUTED_SCALE = 2.5
EPS = 1e-6

T = BATCH * SEQ
TM = 512
BLK = 512
NBLK = -(-(T * TOPK + E * (BLK - 1)) // BLK)
PROWS = (NBLK + 1) * BLK + 8

VMEM_LIMIT = 56 * 1024 * 1024
NEG = -1e30


def _cp(sem):
    return pltpu.CompilerParams(dimension_semantics=sem, vmem_limit_bytes=VMEM_LIMIT)


def _silu(v):
    return v * jax.nn.sigmoid(v)


def _rms(v, g):
    return v * lax.rsqrt(jnp.mean(v * v, axis=-1, keepdims=True) + EPS) * g


def _split3(v):
    a = v.astype(BF16)
    r = v - a.astype(F32)
    b = r.astype(BF16)
    c = (r - b.astype(F32)).astype(BF16)
    return a, b, c


def _dot(a, b):
    return jnp.dot(a, b, preferred_element_type=F32)


def _dot_nt(a, b):
    return lax.dot_general(a, b, (((1,), (1,)), ((), ())), preferred_element_type=F32)


U32 = jnp.uint32
RW = D // 2 // 128
_HI = 0xFFFF0000


def _pack2(lo, hi):
    ul = pltpu.bitcast(lo.astype(BF16).astype(F32), U32)
    uh = pltpu.bitcast(hi.astype(BF16).astype(F32), U32)
    return (ul >> 16) | (uh & U32(_HI))


def _unpack2(w):
    return pltpu.bitcast(w << 16, F32), pltpu.bitcast(w & U32(_HI), F32)


def _store_packed(ref, v):
    m = v.shape[0]
    for s in range(RW):
        lo = v[:, s * 128:(s + 1) * 128]
        hi = v[:, D // 2 + s * 128:D // 2 + (s + 1) * 128]
        ref[pl.ds(s, m, stride=RW), :] = _pack2(lo, hi)


def _load_packed(ref, m, base=0):
    los, his = [], []
    for s in range(RW):
        lo, hi = _unpack2(ref[pl.ds(base + s, m, stride=RW), :])
        los.append(lo.astype(BF16))
        his.append(hi.astype(BF16))
    return jnp.concatenate(los + his, axis=1)


def _ada_kernel(c_ref, w_ref, b_ref, o_ref):
    s = _silu(c_ref[...])
    o_ref[...] = jnp.dot(s, w_ref[...], preferred_element_type=F32,
                         precision=lax.Precision.HIGHEST) + b_ref[...]


def _ada(cvec, ada_w, ada_b):
    depth = ada_w.shape[0]
    tn = 1536
    return pl.pallas_call(
        _ada_kernel,
        grid=(depth, 6 * D // tn),
        in_specs=[pl.BlockSpec((8, D), lambda l, j: (0, 0)),
                  pl.BlockSpec((None, D, tn), lambda l, j: (l, 0, j)),
                  pl.BlockSpec((None, 1, tn), lambda l, j: (l, 0, j))],
        out_specs=pl.BlockSpec((None, 8, tn), lambda l, j: (l, 0, j)),
        out_shape=jax.ShapeDtypeStruct((depth, 8, 6 * D), F32),
        compiler_params=_cp(("arbitrary", "arbitrary")),
        name="ada",
    )(cvec, ada_w, ada_b.reshape(depth, 1, 6 * D))


def _ssm_in_kernel(c_ref, x_ref, mod_ref, g_ref, wz_ref, wx0_ref, wx1_ref, wdt_ref, wdtT_ref, dtb_ref,
                   dtbT_ref, z_ref, xbc_ref, dt_ref, dtT_ref):
    xin = jnp.where(pl.program_id(1) == 0, c_ref[...], x_ref[...])
    h = _rms(xin, g_ref[...]) * (1.0 + mod_ref[1:2, :]) + mod_ref[0:1, :]
    hb = h.astype(BF16)
    z_ref[...] = _dot(hb, wz_ref[...]).astype(BF16)
    xbc_ref[:, :D_INNER] = _dot(hb, wx0_ref[...]).astype(BF16)
    xbc_ref[:, D_INNER:] = _dot(hb, wx1_ref[...]).astype(BF16)
    dt_ref[...] = jax.nn.softplus(_dot(hb, wdt_ref[...]) + dtb_ref[...])
    dtT_ref[...] = jax.nn.softplus(_dot_nt(wdtT_ref[...], hb) + dtbT_ref[...])


def _ssm_in(ctx, x, mod, gain, w_in, wdt, wdtT, dtb, dtbT):
    ts = CTX
    nt = LTOT // ts
    full = lambda b, i: (0, 0)
    assert CONV_DIM == 2 * D_INNER and SEQ % ts == 0
    return pl.pallas_call(
        _ssm_in_kernel,
        grid=(BATCH, nt),
        in_specs=[pl.BlockSpec((None, ts, D), lambda b, i: (b, 0, 0)),
                  pl.BlockSpec((None, ts, D), lambda b, i: (b, jnp.maximum(i - 1, 0), 0)),
                  pl.BlockSpec((None, 6, D), lambda b, i: (jnp.where(i == 0, BATCH, b), 0, 0)),
                  pl.BlockSpec((1, D), full),
                  pl.BlockSpec((D, D_INNER), lambda b, i: (0, 0)),
                  pl.BlockSpec((D, D_INNER), lambda b, i: (0, 1)),
                  pl.BlockSpec((D, D_INNER), lambda b, i: (0, 2)),
                  pl.BlockSpec((D, 2 * HEADS), full),
                  pl.BlockSpec((2 * HEADS, D), full),
                  pl.BlockSpec((1, 2 * HEADS), full),
                  pl.BlockSpec((2 * HEADS, 1), full)],
        out_specs=[pl.BlockSpec((None, ts, D_INNER), lambda b, i: (b, i, 0)),
                   pl.BlockSpec((None, ts, CONV_DIM), lambda b, i: (b, i, 0)),
                   pl.BlockSpec((None, ts, 2 * HEADS), lambda b, i: (b, i, 0)),
                   pl.BlockSpec((None, 2 * HEADS, ts), lambda b, i: (b, 0, i))],
        out_shape=[jax.ShapeDtypeStruct((BATCH, LTOT, D_INNER), BF16),
                   jax.ShapeDtypeStruct((BATCH, LTOT, CONV_DIM), BF16),
                   jax.ShapeDtypeStruct((BATCH, LTOT, 2 * HEADS), F32),
                   jax.ShapeDtypeStruct((BATCH, 2 * HEADS, LTOT), F32)],
        compiler_params=_cp(("arbitrary", "arbitrary")),
        name="ssm_in",
    )(ctx, x, mod, gain, w_in, w_in, w_in, wdt, wdtT, dtb, dtbT)


_HALO = 16
_CC = 512


def _ssm_conv_kernel(x_ref, xp_ref, xn_ref, w_ref, b_ref, xs_ref, bt_ref, cm_ref):
    i = pl.program_id(1)
    first_lat = CTX // Q
    pvalid = (i != 0) & (i != first_lat)
    nvalid = (i != first_lat - 1) & (i != NCHUNK - 1)
    li = lax.broadcasted_iota(I32, (Q, Q + 2 * _HALO), 0)
    ji = lax.broadcasted_iota(I32, (Q, Q + 2 * _HALO), 1)
    shift = {k: jnp.where(ji == li + _HALO + k - SSM_K // 2, 1.0, 0.0).astype(BF16)
             for k in range(SSM_K) if k != SSM_K // 2}
    for c in range(CONV_DIM // _CC):
        cs = slice(c * _CC, (c + 1) * _CC)
        cur = x_ref[:, cs]
        prev = jnp.where(pvalid, xp_ref[:, cs], jnp.zeros((_HALO, _CC), BF16))
        nxt = jnp.where(nvalid, xn_ref[:, cs], jnp.zeros((_HALO, _CC), BF16))
        ext = jnp.concatenate([prev, cur, nxt], axis=0)
        acc = jnp.broadcast_to(b_ref[:, cs], (Q, _CC))
        for k in range(SSM_K):
            wk = jnp.concatenate([w_ref[8 * k:8 * k + 8, cs]] * (Q // 8), axis=0)
            tap = cur.astype(F32) if k == SSM_K // 2 else _dot(shift[k], ext)
            acc = acc + wk * tap
        y = _silu(acc)
        lo = c * _CC
        if lo < D_INNER:
            xs_ref[:, cs] = y.astype(BF16)
        elif lo < D_INNER + GROUPS * NSTATE:
            o = lo - D_INNER
            bt_ref[o:o + _CC, :] = y.T.astype(BF16)
        else:
            o = lo - D_INNER - GROUPS * NSTATE
            cm_ref[:, o:o + _CC] = y.astype(BF16)


def _ssm_conv(xbc, w, b):
    nh = Q // _HALO
    last = LTOT // _HALO - 1
    gn = GROUPS * NSTATE
    return pl.pallas_call(
        _ssm_conv_kernel,
        grid=(BATCH, NCHUNK),
        in_specs=[pl.BlockSpec((None, Q, CONV_DIM), lambda b, i: (b, i, 0)),
                  pl.BlockSpec((None, _HALO, CONV_DIM), lambda b, i: (b, jnp.maximum(i * nh - 1, 0), 0)),
                  pl.BlockSpec((None, _HALO, CONV_DIM), lambda b, i: (b, jnp.minimum(i * nh + nh, last), 0)),
                  pl.BlockSpec((SSM_K * 8, CONV_DIM), lambda b, i: (0, 0)),
                  pl.BlockSpec((1, CONV_DIM), lambda b, i: (0, 0))],
        out_specs=[pl.BlockSpec((None, Q, D_INNER), lambda b, i: (b, i, 0)),
                   pl.BlockSpec((None, gn, Q), lambda b, i: (b, 0, i)),
                   pl.BlockSpec((None, Q, gn), lambda b, i: (b, i, 0))],
        out_shape=[jax.ShapeDtypeStruct((BATCH, LTOT, D_INNER), BF16),
                   jax.ShapeDtypeStruct((BATCH, gn, LTOT), BF16),
                   jax.ShapeDtypeStruct((BATCH, LTOT, gn), BF16)],
        compiler_params=_cp(("arbitrary", "arbitrary")),
        name="ssm_conv",
    )(xbc, xbc, xbc, w, b)


def _ssd_chunk(direction, xs_ref, cm_ref, bt_ref, dt_ref, dtT_ref, arow_ref, acol_ref, rexp_ref, s_ref):
    d0 = direction * HEADS
    dtc = dt_ref[:, d0:d0 + HEADS]
    dtr = dtT_ref[d0:d0 + HEADS, :]
    da_c = dtc * arow_ref[:, d0:d0 + HEADS]
    da_r = dtr * acol_ref[d0:d0 + HEADS, :]
    ii = lax.broadcasted_iota(I32, (Q, Q), 0)
    jj = lax.broadcasted_iota(I32, (Q, Q), 1)
    if direction == 0:
        lower = jj <= ii
        tot_idx = Q - 1
    else:
        lower = jj >= ii
        tot_idx = 0
    tri_c = jnp.where(lower, 1.0, 0.0).astype(BF16)
    upper = (ii <= jj) if direction == 0 else (ii >= jj)
    tri_r = jnp.where(upper, 1.0, 0.0).astype(BF16)
    c1, c2, c3 = _split3(da_c)
    cum_c = _dot(tri_c, c1) + _dot(tri_c, c2) + _dot(tri_c, c3)
    r1, r2, r3 = _split3(da_r)
    cum_r = _dot(r1, tri_r) + _dot(r2, tri_r) + _dot(r3, tri_r)
    tot_c = cum_c[tot_idx:tot_idx + 1, :]
    tot_r = cum_r[:, tot_idx:tot_idx + 1]
    rfac = dtr * jnp.exp(tot_r - cum_r)
    dec = jnp.exp(tot_c)
    dh = dec.astype(BF16)
    dl = (dec - dh.astype(F32)).astype(BF16)
    dec_x = (_dot(jnp.broadcast_to(dh, (8, HEADS)), rexp_ref[...])
             + _dot(jnp.broadcast_to(dl, (8, HEADS)), rexp_ref[...]))[0:1, :]
    assert Q == NSTATE and 2 * HEADDIM == 128
    odd_head = lax.broadcasted_iota(I32, (Q, 128), 1) >= HEADDIM
    ys = []
    for g in range(GROUPS):
        cg = cm_ref[:, g * NSTATE:(g + 1) * NSTATE]
        btg = bt_ref[g * NSTATE:(g + 1) * NSTATE, :]
        xg = xs_ref[:, g * 256:(g + 1) * 256]
        sg = s_ref[g]
        cb = _dot(cg, btg)
        sgb = sg.astype(BF16)
        cg32 = cg.astype(F32)
        btg32 = btg.astype(F32)
        yh = []
        uh = []
        for r in range(HPG):
            h = g * HPG + r
            hs = slice((r // 2) * 128, (r // 2 + 1) * 128)
            colb = cum_c[:, h:h + 1]
            rowb = cum_r[h:h + 1, :]
            decay = jnp.exp(jnp.where(lower, colb - rowb, NEG))
            m = (cb * decay * dtr[h:h + 1, :]).astype(BF16)
            cs = (cg32 * jnp.exp(colb)).astype(BF16)
            rhs = jnp.concatenate([xg[:, hs], sgb[:, hs]], axis=0)
            res = _dot(jnp.concatenate([m, cs], axis=1), rhs)
            bw = (btg32 * rfac[h:h + 1, :]).astype(BF16)
            upd = _dot(bw, xg[:, hs])
            if r % 2 == 0:
                yh.append(res)
                uh.append(upd)
            else:
                yh[-1] = jnp.where(odd_head, res, yh[-1])
                uh[-1] = jnp.where(odd_head, upd, uh[-1])
        s_ref[g] = sg * dec_x[:, g * 256:(g + 1) * 256] + jnp.concatenate(uh, axis=1)
        ys.append(jnp.concatenate(yh, axis=1))
    return ys


def _ssd_bwd_kernel(xs_ref, cm_ref, bt_ref, dt_ref, dtT_ref, arow_ref, acol_ref, rexp_ref,
                    y_ref, s_ref):
    @pl.when(pl.program_id(1) == 0)
    def _():
        s_ref[...] = jnp.zeros_like(s_ref)
    for i in range(NB):
        ys = _ssd_chunk(1, xs_ref.at[i], cm_ref.at[i], bt_ref.at[i], dt_ref.at[i], dtT_ref.at[i],
                        arow_ref, acol_ref, rexp_ref, s_ref.at[i])
        for g in range(GROUPS):
            y_ref[i, :, g * 256:(g + 1) * 256] = ys[g].astype(BF16)


def _ssd_fwd_kernel(xs_ref, cm_ref, bt_ref, dt_ref, dtT_ref, arow_ref, acol_ref, rexp_ref,
                    z_ref, yb_ref, dsk_ref, nw_ref, y_ref, s_ref):
    @pl.when(pl.program_id(1) == 0)
    def _():
        s_ref[...] = jnp.zeros_like(s_ref)
    for i in range(NB):
        ys = _ssd_chunk(0, xs_ref.at[i], cm_ref.at[i], bt_ref.at[i], dt_ref.at[i], dtT_ref.at[i],
                        arow_ref, acol_ref, rexp_ref, s_ref.at[i])
        for g in range(GROUPS):
            gs = slice(g * 256, (g + 1) * 256)
            y = ys[g] + yb_ref[i, :, gs].astype(F32) + xs_ref[i, :, gs].astype(F32) * dsk_ref[:, gs]
            y = y * _silu(z_ref[i, :, gs].astype(F32))
            y = y * lax.rsqrt(jnp.mean(y * y, axis=-1, keepdims=True) + EPS) * nw_ref[:, gs]
            y_ref[i, :, gs] = y.astype(BF16)


NB = 2


def _ssd_specs(cmap):
    gn = GROUPS * NSTATE
    full = lambda b, j: (0, 0)
    return [pl.BlockSpec((NB, Q, D_INNER), lambda b, j: (b, cmap(j), 0)),
            pl.BlockSpec((NB, Q, gn), lambda b, j: (b, cmap(j), 0)),
            pl.BlockSpec((NB, gn, Q), lambda b, j: (b, 0, cmap(j))),
            pl.BlockSpec((NB, Q, 2 * HEADS), lambda b, j: (b, cmap(j), 0)),
            pl.BlockSpec((NB, 2 * HEADS, Q), lambda b, j: (b, 0, cmap(j))),
            pl.BlockSpec((1, 2 * HEADS), full),
            pl.BlockSpec((2 * HEADS, 1), full),
            pl.BlockSpec((HEADS, D_INNER), full)]


def _ssd_bwd(xs, cm, bt, dt, dtT, arow, acol, rexp):
    cmap = lambda j: jnp.where(j < CTX_CHUNKS, CTX_CHUNKS - 1 - j, NCHUNK + CTX_CHUNKS - 1 - j)
    omap = lambda b, j: (b, NCHUNK - 1 - jnp.maximum(j, CTX_CHUNKS), 0)
    return pl.pallas_call(
        _ssd_bwd_kernel,
        grid=(BATCH // NB, NCHUNK),
        in_specs=_ssd_specs(cmap),
        out_specs=pl.BlockSpec((NB, Q, D_INNER), omap),
        out_shape=jax.ShapeDtypeStruct((BATCH, SEQ, D_INNER), BF16),
        scratch_shapes=[pltpu.VMEM((NB, GROUPS, NSTATE, HPG * HEADDIM), F32)],
        compiler_params=_cp(("arbitrary", "arbitrary")),
        name="ssd_bwd",
    )(xs, cm, bt, dt, dtT, arow, acol, rexp)


def _ssd_fwd(xs, cm, bt, dt, dtT, arow, acol, rexp, z, yb, dsk, nw):
    cmap = lambda j: j
    lat = lambda b, j: (b, jnp.maximum(j - CTX_CHUNKS, 0), 0)
    full = lambda b, j: (0, 0)
    return pl.pallas_call(
        _ssd_fwd_kernel,
        grid=(BATCH // NB, NCHUNK),
        in_specs=_ssd_specs(cmap) + [
            pl.BlockSpec((NB, Q, D_INNER), lambda b, j: (b, j, 0)),
            pl.BlockSpec((NB, Q, D_INNER), lat),
            pl.BlockSpec((1, D_INNER), full),
            pl.BlockSpec((1, D_INNER), full)],
        out_specs=pl.BlockSpec((NB, Q, D_INNER), lat),
        out_shape=jax.ShapeDtypeStruct((BATCH, SEQ, D_INNER), BF16),
        scratch_shapes=[pltpu.VMEM((NB, GROUPS, NSTATE, HPG * HEADDIM), F32)],
        compiler_params=_cp(("arbitrary", "arbitrary")),
        name="ssd_fwd",
    )(xs, cm, bt, dt, dtT, arow, acol, rexp, z, yb, dsk, nw)


def _route(h, rwT_ref, rb_ref, cnt_ref, eidx_ref, pos_ref, gate_ref):
    hh = h.astype(BF16)
    hl = (h - hh.astype(F32)).astype(BF16)
    w = rwT_ref[...]
    wh = w.astype(BF16)
    wl = (w - wh.astype(F32)).astype(BF16)
    logits = _dot_nt(wh, hh) + _dot_nt(wh, hl) + _dot_nt(wl, hh)
    scores = jax.nn.sigmoid(logits)
    sel = scores + rb_ref[...]
    per = E // NGRP
    sub = lax.broadcasted_iota(I32, (per, TM), 0)
    gscore = []
    for g in range(NGRP):
        blk = sel[g * per:(g + 1) * per, :]
        m1 = jnp.max(blk, axis=0, keepdims=True)
        first = jnp.min(jnp.where(blk == m1, sub, per), axis=0, keepdims=True)
        m2 = jnp.max(jnp.where(sub == first, -jnp.inf, blk), axis=0, keepdims=True)
        gscore.append(m1 + m2)
    masked = []
    for g in range(NGRP):
        rank = jnp.zeros((1, TM), F32)
        for o in range(NGRP):
            if o == g:
                continue
            ahead = (gscore[o] >= gscore[g]) if o < g else (gscore[o] > gscore[g])
            rank = rank + jnp.where(ahead, 1.0, 0.0)
        blk = sel[g * per:(g + 1) * per, :]
        masked.append(jnp.where(rank < TOPG, blk, -jnp.inf))
    v = jnp.concatenate(masked, axis=0)
    eio = lax.broadcasted_iota(I32, (E, TM), 0)
    kio = lax.broadcasted_iota(I32, (TOPK, TM), 0)
    eidx = jnp.zeros((TOPK, TM), I32)
    gsc = jnp.zeros((TOPK, TM), F32)
    hot = jnp.zeros((E, TM), F32)
    picks = []
    for k in range(TOPK):
        m = jnp.max(v, axis=0, keepdims=True)
        first = jnp.min(jnp.where(v == m, eio, E), axis=0, keepdims=True)
        pick = eio == first
        sc = jnp.sum(jnp.where(pick, scores, 0.0), axis=0, keepdims=True)
        eidx = jnp.where(kio == k, first, eidx)
        gsc = jnp.where(kio == k, sc, gsc)
        hot = jnp.where(pick, 1.0, hot)
        v = jnp.where(pick, -jnp.inf, v)
        picks.append(pick)
    gate = gsc / jnp.sum(gsc, axis=0, keepdims=True) * ROUTED_SCALE
    ti = lax.broadcasted_iota(I32, (TM, TM), 0)
    tj = lax.broadcasted_iota(I32, (TM, TM), 1)
    before = jnp.where(ti < tj, 1.0, 0.0).astype(BF16)
    posfull = _dot(hot.astype(BF16), before) + cnt_ref[...]
    pos = jnp.zeros((TOPK, TM), F32)
    for k in range(TOPK):
        pk = jnp.sum(jnp.where(picks[k], posfull, 0.0), axis=0, keepdims=True)
        pos = jnp.where(kio == k, pk, pos)
    cnt_ref[...] = cnt_ref[...] + jnp.sum(hot, axis=1, keepdims=True)
    eidx_ref[...] = eidx
    pos_ref[...] = pos.astype(I32)
    eye = jnp.where(ti == tj, 1.0, 0.0).astype(BF16)
    g1, g2, g3 = _split3(gate)
    gate_ref[...] = _dot_nt(eye, g1) + _dot_nt(eye, g2) + _dot_nt(eye, g3)


def _mix_epilogue(y, x_ref, mod_ref, gpost_ref, gpre_ref, rwT_ref, rb_ref,
                  x1_ref, h2_ref, eidx_ref, pos_ref, gate_ref, cnt_out_ref, cnt_ref):
    first = (pl.program_id(0) == 0) & (pl.program_id(1) == 0)

    @pl.when(first)
    def _():
        cnt_ref[...] = jnp.zeros_like(cnt_ref)
    x1 = x_ref[...] + mod_ref[2:3, :] * _rms(y, gpost_ref[...])
    x1_ref[...] = x1
    h2 = _rms(x1, gpre_ref[...]) * (1.0 + mod_ref[4:5, :]) + mod_ref[3:4, :]
    _store_packed(h2_ref, h2)
    _route(h2, rwT_ref, rb_ref, cnt_ref, eidx_ref, pos_ref, gate_ref)
    cnt_out_ref[...] = cnt_ref[...]


def _ssm_out_kernel(y_ref, w_ref, x_ref, mod_ref, gpost_ref, gpre_ref, rwT_ref, rb_ref,
                    x1_ref, h2_ref, eidx_ref, pos_ref, gate_ref, cnt_out_ref, cnt_ref):
    y = _dot(y_ref[...], w_ref[...])
    _mix_epilogue(y, x_ref, mod_ref, gpost_ref, gpre_ref, rwT_ref, rb_ref,
                  x1_ref, h2_ref, eidx_ref, pos_ref, gate_ref, cnt_out_ref, cnt_ref)


_CW = 256
_PAD = 16


def _conf_out_kernel(u_ref, dww_ref, dwb_ref, lng_ref, lnb_ref, w_ref, b_ref,
                     x_ref, mod_ref, gpost_ref, gpre_ref, rwT_ref, rb_ref,
                     x1_ref, h2_ref, eidx_ref, pos_ref, gate_ref, cnt_out_ref,
                     cnt_ref, v_ref):
    nrow = TM // GRID_W
    ext_rows = GRID_W + 2 * _PAD
    first = _PAD - CONF_K // 2
    span = GRID_W + 8 * ((CONF_K - 1) // 8)
    assert first + 7 + span <= ext_rows
    ri = lax.broadcasted_iota(I32, (span, ext_rows), 0)
    ji = lax.broadcasted_iota(I32, (span, ext_rows), 1)
    shift = [jnp.where(ji == ri + first + b, 1.0, 0.0).astype(BF16) for b in range(8)]
    zpad = jnp.zeros((_PAD, D), BF16)
    for r in range(nrow):
        ext = jnp.concatenate([zpad, u_ref[r * GRID_W:(r + 1) * GRID_W, :], zpad], axis=0)
        for c in range(D // _CW):
            cs = slice(c * _CW, (c + 1) * _CW)
            acc = jnp.broadcast_to(dwb_ref[:, cs], (GRID_W, _CW))
            for b in range(8):
                win = _dot(shift[b], ext[:, cs])
                for a in range(-(-CONF_K // 8)):
                    k = 8 * a + b
                    if k < CONF_K:
                        wk = jnp.concatenate([dww_ref[8 * k:8 * k + 8, cs]] * (GRID_W // 8), axis=0)
                        acc = acc + wk * win[8 * a:8 * a + GRID_W, :]
            v_ref[r * GRID_W:(r + 1) * GRID_W, cs] = acc
    v = v_ref[...]
    mu = jnp.mean(v, axis=-1, keepdims=True)
    vc = v - mu
    ln = vc * lax.rsqrt(jnp.mean(vc * vc, axis=-1, keepdims=True) + EPS) * lng_ref[...] + lnb_ref[...]
    y = _dot(_silu(ln).astype(BF16), w_ref[...]) + b_ref[...]
    _mix_epilogue(y, x_ref, mod_ref, gpost_ref, gpre_ref, rwT_ref, rb_ref,
                  x1_ref, h2_ref, eidx_ref, pos_ref, gate_ref, cnt_out_ref, cnt_ref)


def _mix_out_common_specs():
    full = lambda b, i: (0, 0)
    nt = SEQ // TM
    in_specs = [pl.BlockSpec((None, TM, D), lambda b, i: (b, i, 0)),
                pl.BlockSpec((None, 6, D), lambda b, i: (b, 0, 0)),
                pl.BlockSpec((1, D), full), pl.BlockSpec((1, D), full),
                pl.BlockSpec((E, D), full), pl.BlockSpec((E, 1), full)]
    out_specs = [pl.BlockSpec((None, TM, D), lambda b, i: (b, i, 0)),
                 pl.BlockSpec((TM * RW, 128), lambda b, i: (b * nt + i, 0)),
                 pl.BlockSpec((TOPK, TM), lambda b, i: (0, b * nt + i)),
                 pl.BlockSpec((TOPK, TM), lambda b, i: (0, b * nt + i)),
                 pl.BlockSpec((TM, TOPK), lambda b, i: (b * nt + i, 0)),
                 pl.BlockSpec((E, 1), full)]
    out_shape = [jax.ShapeDtypeStruct((BATCH, SEQ, D), F32),
                 jax.ShapeDtypeStruct((T * RW, 128), U32),
                 jax.ShapeDtypeStruct((TOPK, T), I32),
                 jax.ShapeDtypeStruct((TOPK, T), I32),
                 jax.ShapeDtypeStruct((T, TOPK), F32),
                 jax.ShapeDtypeStruct((E, 1), F32)]
    return in_specs, out_specs, out_shape


def _ssm_out(y, w, x, mod, gpost, gpre, rwT, rb):
    common_in, out_specs, out_shape = _mix_out_common_specs()
    return pl.pallas_call(
        _ssm_out_kernel,
        grid=(BATCH, SEQ // TM),
        in_specs=[pl.BlockSpec((None, TM, D_INNER), lambda b, i: (b, i, 0)),
                  pl.BlockSpec((D_INNER, D), lambda b, i: (0, 0))] + common_in,
        out_specs=out_specs, out_shape=out_shape,
        scratch_shapes=[pltpu.VMEM((E, 1), F32)],
        compiler_params=_cp(("arbitrary", "arbitrary")),
        name="ssm_out",
    )(y, w, x, mod, gpost, gpre, rwT, rb)


def _conf_out(u, dww, dwb, lng, lnb, w, b, x, mod, gpost, gpre, rwT, rb):
    common_in, out_specs, out_shape = _mix_out_common_specs()
    full = lambda b_, i: (0, 0)
    return pl.pallas_call(
        _conf_out_kernel,
        grid=(BATCH, SEQ // TM),
        in_specs=[pl.BlockSpec((None, TM, D), lambda b_, i: (b_, i, 0)),
                  pl.BlockSpec((CONF_K * 8, D), full), pl.BlockSpec((1, D), full),
                  pl.BlockSpec((1, D), full), pl.BlockSpec((1, D), full),
                  pl.BlockSpec((D, D), full), pl.BlockSpec((1, D), full)] + common_in,
        out_specs=out_specs, out_shape=out_shape,
        scratch_shapes=[pltpu.VMEM((E, 1), F32), pltpu.VMEM((TM, D), F32)],
        compiler_params=_cp(("arbitrary", "arbitrary")),
        name="conf_out",
    )(u, dww, dwb, lng, lnb, w, b, x, mod, gpost, gpre, rwT, rb)


def _conf_in_kernel(x_ref, mod_ref, g_ref, wa_ref, wg_ref, ba_ref, bg_ref, u_ref):
    h = _rms(x_ref[...], g_ref[...]) * (1.0 + mod_ref[1:2, :]) + mod_ref[0:1, :]
    hb = h.astype(BF16)
    a = _dot(hb, wa_ref[...]) + ba_ref[...]
    g = _dot(hb, wg_ref[...]) + bg_ref[...]
    u_ref[...] = (a * jax.nn.sigmoid(g)).astype(BF16)


def _conf_in(x, mod, gain, w, bias):
    full = lambda b, i: (0, 0)
    return pl.pallas_call(
        _conf_in_kernel,
        grid=(BATCH, SEQ // TM),
        in_specs=[pl.BlockSpec((None, TM, D), lambda b, i: (b, i, 0)),
                  pl.BlockSpec((None, 6, D), lambda b, i: (b, 0, 0)),
                  pl.BlockSpec((1, D), full),
                  pl.BlockSpec((D, D), full), pl.BlockSpec((D, D), lambda b, i: (0, 1)),
                  pl.BlockSpec((1, D), full), pl.BlockSpec((1, D), lambda b, i: (0, 1))],
        out_specs=pl.BlockSpec((None, TM, D), lambda b, i: (b, i, 0)),
        out_shape=jax.ShapeDtypeStruct((BATCH, SEQ, D), BF16),
        compiler_params=_cp(("arbitrary", "arbitrary")),
        name="conf_in",
    )(x, mod, gain, w, w, bias, bias)


def _row_copy(src, si, dst, di, sem):
    return pltpu.make_async_copy(src.at[pl.ds(pl.multiple_of(si * RW, RW), RW), :],
                                 dst.at[pl.ds(pl.multiple_of(di * RW, RW), RW), :], sem)


ZROWS = BLK + 8


def _dispatch_kernel(zs_ref, dest_ref, h_ref, xs_ref, zero_ref, zsem, sem):
    @pl.when(pl.program_id(0) == 0)
    def _():
        zero_ref[...] = jnp.zeros_like(zero_ref)

        def zcopy(e):
            start = pl.multiple_of(zs_ref[e] * RW, 8 * RW)
            return pltpu.make_async_copy(zero_ref, xs_ref.at[pl.ds(start, ZROWS * RW), :], zsem)

        def zstart(e, c):
            zcopy(e).start()
            return c

        def zwait(e, c):
            zcopy(e).wait()
            return c
        lax.fori_loop(0, E, zstart, 0)
        lax.fori_loop(0, E, zwait, 0)

    def issue(t, c):
        for k in range(TOPK):
            _row_copy(h_ref, t, xs_ref, dest_ref[t * TOPK + k], sem).start(priority=k % 2)
        return c

    def drain(t, c):
        for k in range(TOPK):
            _row_copy(h_ref, t, xs_ref, 0, sem).wait()
        return c
    lax.fori_loop(0, TM, issue, 0)
    lax.fori_loop(0, TM, drain, 0)


def _dispatch(zstart, dest, hp):
    grid_spec = pltpu.PrefetchScalarGridSpec(
        num_scalar_prefetch=1,
        grid=(T // TM,),
        in_specs=[pl.BlockSpec((TM * TOPK,), lambda i, zs: (i,), memory_space=pltpu.SMEM),
                  pl.BlockSpec((TM * RW, 128), lambda i, zs: (i, 0))],
        out_specs=pl.BlockSpec(memory_space=pl.ANY),
        scratch_shapes=[pltpu.VMEM((ZROWS * RW, 128), U32), pltpu.SemaphoreType.DMA(()),
                        pltpu.SemaphoreType.DMA(())])
    return pl.pallas_call(
        _dispatch_kernel,
        grid_spec=grid_spec,
        out_shape=jax.ShapeDtypeStruct((PROWS * RW, 128), U32),
        compiler_params=_cp(("arbitrary",)),
        name="moe_dispatch",
    )(zstart, dest, hp)


def _experts_kernel(be_ref, na_ref, x_ref, wg_ref, wu_ref, wd_ref, y_ref, wgb_ref, wub_ref, wdb_ref):
    i = pl.program_id(0)

    @pl.when((i == 0) | (be_ref[i] != be_ref[jnp.maximum(i - 1, 0)]))
    def _():
        wgb_ref[...] = wg_ref[...].astype(BF16)
        wub_ref[...] = wu_ref[...].astype(BF16)
        wdb_ref[...] = wd_ref[...].astype(BF16)

    @pl.when(i < na_ref[0])
    def _():
        xb = _load_packed(x_ref, BLK)
        g = _dot(xb, wgb_ref[...])
        u = _dot(xb, wub_ref[...])
        a = (_silu(g) * u).astype(BF16)
        _store_packed(y_ref, _dot(a, wdb_ref[...]))

    @pl.when(i >= na_ref[0])
    def _():
        y_ref[...] = jnp.zeros_like(y_ref)


def _experts(layer, block_e, nact, xs, wg, wu, wd):
    xmap = lambda i, be, na: (jnp.minimum(i, na[0] - 1), 0)
    wmap = lambda i, be, na: (layer, be[i], 0, 0)
    grid_spec = pltpu.PrefetchScalarGridSpec(
        num_scalar_prefetch=2,
        grid=(NBLK,),
        in_specs=[pl.BlockSpec((BLK * RW, 128), xmap),
                  pl.BlockSpec((None, None, D, DE), wmap),
                  pl.BlockSpec((None, None, D, DE), wmap),
                  pl.BlockSpec((None, None, DE, D), wmap)],
        out_specs=pl.BlockSpec((BLK * RW, 128), lambda i, be, na: (i, 0)),
        scratch_shapes=[pltpu.VMEM((D, DE), BF16), pltpu.VMEM((D, DE), BF16),
                        pltpu.VMEM((DE, D), BF16)])
    return pl.pallas_call(
        _experts_kernel,
        grid_spec=grid_spec,
        out_shape=jax.ShapeDtypeStruct((NBLK * BLK * RW, 128), U32),
        compiler_params=_cp(("arbitrary",)),
        name="moe_experts",
    )(block_e, nact, xs, wg, wu, wd)


def _combine_kernel(dcur_ref, dnxt_ref, gate_ref, h_ref, x1_ref, mod_ref, gpost_ref,
                    sg_ref, su_ref, sd_ref, ys_ref, o_ref, buf_ref, sems):
    i = pl.program_id(0)
    n = pl.num_programs(0)

    def gather(dest_ref, slot):
        def body(t, c):
            for k in range(TOPK):
                _row_copy(ys_ref, dest_ref[t * TOPK + k], buf_ref.at[slot, k], t,
                          sems.at[slot]).start(priority=k % 2)
            return c
        lax.fori_loop(0, TM, body, 0)

    @pl.when(i == 0)
    def _():
        gather(dcur_ref, 0)

    @pl.when(i + 1 < n)
    def _():
        gather(dnxt_ref, (i + 1) % 2)

    hb = _load_packed(h_ref, TM)
    a = (_silu(_dot(hb, sg_ref[...])) * _dot(hb, su_ref[...])).astype(BF16)
    f = _dot(a, sd_ref[...])

    slot = i % 2

    def drain(t, c):
        for k in range(TOPK):
            _row_copy(ys_ref, 0, buf_ref.at[slot, k], 0, sems.at[slot]).wait()
        return c
    lax.fori_loop(0, TM, drain, 0)

    gate = gate_ref[...]
    gb = [jnp.broadcast_to(gate[:, k:k + 1], (TM, 128)) for k in range(TOPK)]
    los, his = [], []
    for s in range(RW):
        lo = f[:, s * 128:(s + 1) * 128]
        hi = f[:, D // 2 + s * 128:D // 2 + (s + 1) * 128]
        for k in range(TOPK):
            wl, wh = _unpack2(buf_ref[slot, k, pl.ds(s, TM, stride=RW), :])
            lo = lo + gb[k] * wl
            hi = hi + gb[k] * wh
        los.append(lo)
        his.append(hi)
    f = jnp.concatenate(los + his, axis=1)
    o_ref[...] = x1_ref[...] + mod_ref[5:6, :] * _rms(f, gpost_ref[...])


def _combine(dest, gate, hp, x1, mod, gpost, sg, su, sd, ys):
    nt = SEQ // TM
    n = T // TM
    full = lambda i: (0, 0)
    return pl.pallas_call(
        _combine_kernel,
        grid=(n,),
        in_specs=[pl.BlockSpec((TM * TOPK,), lambda i: (i,), memory_space=pltpu.SMEM),
                  pl.BlockSpec((TM * TOPK,), lambda i: (jnp.minimum(i + 1, n - 1),),
                               memory_space=pltpu.SMEM),
                  pl.BlockSpec((TM, TOPK), lambda i: (i, 0)),
                  pl.BlockSpec((TM * RW, 128), lambda i: (i, 0)),
                  pl.BlockSpec((TM, D), lambda i: (i, 0)),
                  pl.BlockSpec((None, 6, D), lambda i: (i // nt, 0, 0)),
                  pl.BlockSpec((1, D), full),
                  pl.BlockSpec((D, DE), full), pl.BlockSpec((D, DE), full), pl.BlockSpec((DE, D), full),
                  pl.BlockSpec(memory_space=pl.ANY)],
        out_specs=pl.BlockSpec((TM, D), lambda i: (i, 0)),
        out_shape=jax.ShapeDtypeStruct((T, D), F32),
        scratch_shapes=[pltpu.VMEM((2, TOPK, TM * RW, 128), U32), pltpu.SemaphoreType.DMA((2,))],
        compiler_params=_cp(("arbitrary",)),
        name="moe_combine",
    )(dest, dest, gate, hp, x1.reshape(T, D), mod, gpost, sg, su, sd, ys).reshape(BATCH, SEQ, D)


def _moe(layer, x1, hp, eidx, pos, gate, counts, mod, gpost, wg, wu, wd, sg, su, sd):
    cnt = counts.reshape(E).astype(I32)
    padded = (cnt + BLK - 1) // BLK * BLK
    pend = jnp.cumsum(padded)
    pstart = pend - padded
    ids = jnp.arange(E, dtype=I32)
    dest = pos + jnp.sum(jnp.where(eidx[:, :, None] == ids, pstart, 0), axis=-1)
    dest = dest.T.reshape(T * TOPK)
    nact = (pend[-1] // BLK).reshape(1).astype(I32)
    blk0 = jnp.arange(NBLK, dtype=I32) * BLK
    block_e = jnp.sum((pend[None, :] <= blk0[:, None]).astype(I32), axis=1)
    block_e = jnp.minimum(block_e, E - 1).astype(I32)
    xs = _dispatch(((pstart + cnt) // 8 * 8).astype(I32), dest, hp)
    ys = _experts(layer, block_e, nact, xs, wg, wu, wd)
    return _combine(dest, gate, hp, x1, mod, gpost, sg.astype(BF16), su.astype(BF16),
                    sd.astype(BF16), ys)


def kernel(x, c, ctx, c_ctx, ada_w, ada_b, norm_mix_pre, norm_mix_post, norm_ffn_pre, norm_ffn_post, ssm_w_in, ssm_conv_w, ssm_conv_b, ssm_dt_bias, ssm_a_log, ssm_d, ssm_norm, ssm_w_out, cv_w_in, cv_b_in, cv_dw_w, cv_dw_b, cv_ln_g, cv_ln_b, cv_w_out, cv_b_out, router_w, router_b, exp_w_gate, exp_w_up, exp_w_down, sh_w_gate, sh_w_up, sh_w_down):
    row = lambda v: v.reshape(1, -1)
    cvec = jnp.concatenate([c, c_ctx[None, :], jnp.zeros((3, D), F32)], axis=0)
    mod = _ada(cvec, ada_w, ada_b).reshape(2, 8, 6, D)

    w_in = ssm_w_in[0].astype(BF16)
    dtb = ssm_dt_bias[0].reshape(1, 2 * HEADS)
    wdt = w_in[:, D_INNER + CONV_DIM:]
    z, xbc, dt, dtT = _ssm_in(ctx, x, mod[0], row(norm_mix_pre[0]), w_in, wdt, wdt.T,
                              dtb, dtb.reshape(2 * HEADS, 1))
    xs, bt, cm = _ssm_conv(xbc, jnp.repeat(ssm_conv_w[0], 8, axis=0), row(ssm_conv_b[0]))
    a = -jnp.exp(ssm_a_log[0].astype(F32)).reshape(1, 2 * HEADS)
    rexp = (jnp.arange(D_INNER)[None, :] // HEADDIM == jnp.arange(HEADS)[:, None]).astype(BF16)
    yb = _ssd_bwd(xs, cm, bt, dt, dtT, a, a.reshape(2 * HEADS, 1), rexp)
    dsk = jnp.repeat(ssm_d[0], HEADDIM).reshape(1, D_INNER)
    ygn = _ssd_fwd(xs, cm, bt, dt, dtT, a, a.reshape(2 * HEADS, 1), rexp, z, yb, dsk,
                   row(ssm_norm[0]))
    x1, h2, eidx, pos, gate, counts = _ssm_out(
        ygn, ssm_w_out[0].astype(BF16), x, mod[0], row(norm_mix_post[0]), row(norm_ffn_pre[0]),
        router_w[0].T, router_b[0].reshape(E, 1))
    x2 = _moe(0, x1, h2, eidx, pos, gate, counts, mod[0], row(norm_ffn_post[0]),
              exp_w_gate, exp_w_up, exp_w_down, sh_w_gate[0], sh_w_up[0], sh_w_down[0])

    u = _conf_in(x2, mod[1], row(norm_mix_pre[1]), cv_w_in[0].astype(BF16), row(cv_b_in[0]))
    x3, h4, eidx, pos, gate, counts = _conf_out(
        u, jnp.repeat(cv_dw_w[0], 8, axis=0), row(cv_dw_b[0]), row(cv_ln_g[0]), row(cv_ln_b[0]),
        cv_w_out[0].astype(BF16), row(cv_b_out[0]),
        x2, mod[1], row(norm_mix_post[1]), row(norm_ffn_pre[1]),
        router_w[1].T, router_b[1].reshape(E, 1))
    return _moe(1, x3, h4, eidx, pos, gate, counts, mod[1], row(norm_ffn_post[1]),
                exp_w_gate, exp_w_up, exp_w_down, sh_w_gate[1], sh_w_up[1], sh_w_down[1])
```

```python
import functools

import jax
import jax.numpy as jnp
from jax import lax
from jax.experimental import pallas as pl
from jax.experimental.pallas import tpu as pltpu

F32 = jnp.float32
BF16 = jnp.bfloat16
I32 = jnp.int32

D = 1024
BATCH = 4
SEQ = 4096
CTX = 256
LTOT = CTX + SEQ
GRID_W = 64

D_INNER = 2048
HEADS = 32
GROUPS = 8
HPG = 4
HEADDIM = 64
NSTATE = 128
Q = 128
NCHUNK = LTOT // Q
CTX_CHUNKS = CTX // Q
CONV_DIM = D_INNER + 2 * GROUPS * NSTATE
SSM_K = 5
CONF_K = 31

E = 64
TOPK = 8
NGRP = 8
TOPG = 4
DE = 256
ROUTED_SCALE = 2.5
EPS = 1e-6

T = BATCH * SEQ
TM = 512
BLK = 512
NBLK = -(-(T * TOPK + E * (BLK - 1)) // BLK)
PROWS = (NBLK + 1) * BLK + 8

VMEM_LIMIT = 56 * 1024 * 1024
NEG = -1e30


def _cp(sem):
    return pltpu.CompilerParams(dimension_semantics=sem, vmem_limit_bytes=VMEM_LIMIT)


def _silu(v):
    return v * jax.nn.sigmoid(v)


def _rms(v, g):
    return v * lax.rsqrt(jnp.mean(v * v, axis=-1, keepdims=True) + EPS) * g


def _split3(v):
    a = v.astype(BF16)
    r = v - a.astype(F32)
    b = r.astype(BF16)
    c = (r - b.astype(F32)).astype(BF16)
    return a, b, c


def _dot(a, b):
    return jnp.dot(a, b, preferred_element_type=F32)


def _dot_nt(a, b):
    return lax.dot_general(a, b, (((1,), (1,)), ((), ())), preferred_element_type=F32)


U32 = jnp.uint32
RW = D // 2 // 128
_HI = 0xFFFF0000


def _pack2(lo, hi):
    ul = pltpu.bitcast(lo.astype(BF16).astype(F32), U32)
    uh = pltpu.bitcast(hi.astype(BF16).astype(F32), U32)
    return (ul >> 16) | (uh & U32(_HI))


def _unpack2(w):
    return pltpu.bitcast(w << 16, F32), pltpu.bitcast(w & U32(_HI), F32)


def _store_packed(ref, v):
    m = v.shape[0]
    for s in range(RW):
        lo = v[:, s * 128:(s + 1) * 128]
        hi = v[:, D // 2 + s * 128:D // 2 + (s + 1) * 128]
        ref[pl.ds(s, m, stride=RW), :] = _pack2(lo, hi)


def _load_packed(ref, m):
    los, his = [], []
    for s in range(RW):
        lo, hi = _unpack2(ref[pl.ds(s, m, stride=RW), :])
        los.append(lo.astype(BF16))
        his.append(hi.astype(BF16))
    return jnp.concatenate(los + his, axis=1)


def _ada_kernel(c_ref, w_ref, b_ref, o_ref):
    s = _silu(c_ref[...])
    o_ref[...] = jnp.dot(s, w_ref[...], preferred_element_type=F32,
                         precision=lax.Precision.HIGHEST) + b_ref[...]


def _ada(cvec, ada_w, ada_b):
    depth = ada_w.shape[0]
    tn = 1536
    return pl.pallas_call(
        _ada_kernel,
        grid=(depth, 6 * D // tn),
        in_specs=[pl.BlockSpec((8, D), lambda l, j: (0, 0)),
                  pl.BlockSpec((None, D, tn), lambda l, j: (l, 0, j)),
                  pl.BlockSpec((None, 1, tn), lambda l, j: (l, 0, j))],
        out_specs=pl.BlockSpec((None, 8, tn), lambda l, j: (l, 0, j)),
        out_shape=jax.ShapeDtypeStruct((depth, 8, 6 * D), F32),
        compiler_params=_cp(("arbitrary", "arbitrary")),
        name="ada",
    )(cvec, ada_w, ada_b.reshape(depth, 1, 6 * D))


def _ssm_in_kernel(c_ref, x_ref, mod_ref, g_ref, wz_ref, wx0_ref, wx1_ref, wdt_ref, wdtT_ref, dtb_ref,
                   dtbT_ref, z_ref, xbc_ref, dt_ref, dtT_ref):
    xin = jnp.where(pl.program_id(1) == 0, c_ref[...], x_ref[...])
    h = _rms(xin, g_ref[...]) * (1.0 + mod_ref[1:2, :]) + mod_ref[0:1, :]
    hb = h.astype(BF16)
    z_ref[...] = _dot(hb, wz_ref[...]).astype(BF16)
    xbc_ref[:, :D_INNER] = _dot(hb, wx0_ref[...]).astype(BF16)
    xbc_ref[:, D_INNER:] = _dot(hb, wx1_ref[...]).astype(BF16)
    dt_ref[...] = jax.nn.softplus(_dot(hb, wdt_ref[...]) + dtb_ref[...])
    dtT_ref[...] = jax.nn.softplus(_dot_nt(wdtT_ref[...], hb) + dtbT_ref[...])


def _ssm_in(ctx, x, mod, gain, w_in, wdt, wdtT, dtb, dtbT):
    ts = CTX
    nt = LTOT // ts
    full = lambda b, i: (0, 0)
    assert CONV_DIM == 2 * D_INNER and SEQ % ts == 0
    return pl.pallas_call(
        _ssm_in_kernel,
        grid=(BATCH, nt),
        in_specs=[pl.BlockSpec((None, ts, D), lambda b, i: (b, 0, 0)),
                  pl.BlockSpec((None, ts, D), lambda b, i: (b, jnp.maximum(i - 1, 0), 0)),
                  pl.BlockSpec((None, 6, D), lambda b, i: (jnp.where(i == 0, BATCH, b), 0, 0)),
                  pl.BlockSpec((1, D), full),
                  pl.BlockSpec((D, D_INNER), lambda b, i: (0, 0)),
                  pl.BlockSpec((D, D_INNER), lambda b, i: (0, 1)),
                  pl.BlockSpec((D, D_INNER), lambda b, i: (0, 2)),
                  pl.BlockSpec((D, 2 * HEADS), full),
                  pl.BlockSpec((2 * HEADS, D), full),
                  pl.BlockSpec((1, 2 * HEADS), full),
                  pl.BlockSpec((2 * HEADS, 1), full)],
        out_specs=[pl.BlockSpec((None, ts, D_INNER), lambda b, i: (b, i, 0)),
                   pl.BlockSpec((None, ts, CONV_DIM), lambda b, i: (b, i, 0)),
                   pl.BlockSpec((None, ts, 2 * HEADS), lambda b, i: (b, i, 0)),
                   pl.BlockSpec((None, 2 * HEADS, ts), lambda b, i: (b, 0, i))],
        out_shape=[jax.ShapeDtypeStruct((BATCH, LTOT, D_INNER), BF16),
                   jax.ShapeDtypeStruct((BATCH, LTOT, CONV_DIM), BF16),
                   jax.ShapeDtypeStruct((BATCH, LTOT, 2 * HEADS), F32),
                   jax.ShapeDtypeStruct((BATCH, 2 * HEADS, LTOT), F32)],
        compiler_params=_cp(("arbitrary", "arbitrary")),
        name="ssm_in",
    )(ctx, x, mod, gain, w_in, w_in, w_in, wdt, wdtT, dtb, dtbT)


_HALO = 16
_CC = 512


def _ssm_conv_kernel(x_ref, xp_ref, xn_ref, w_ref, b_ref, xs_ref, bt_ref, cm_ref):
    i = pl.program_id(1)
    first_lat = CTX // Q
    pvalid = (i != 0) & (i != first_lat)
    nvalid = (i != first_lat - 1) & (i != NCHUNK - 1)
    li = lax.broadcasted_iota(I32, (Q, Q + 2 * _HALO), 0)
    ji = lax.broadcasted_iota(I32, (Q, Q + 2 * _HALO), 1)
    shift = {k: jnp.where(ji == li + _HALO + k - SSM_K // 2, 1.0, 0.0).astype(BF16)
             for k in range(SSM_K) if k != SSM_K // 2}
    for c in range(CONV_DIM // _CC):
        cs = slice(c * _CC, (c + 1) * _CC)
        cur = x_ref[:, cs]
        prev = jnp.where(pvalid, xp_ref[:, cs], jnp.zeros((_HALO, _CC), BF16))
        nxt = jnp.where(nvalid, xn_ref[:, cs], jnp.zeros((_HALO, _CC), BF16))
        ext = jnp.concatenate([prev, cur, nxt], axis=0)
        acc = jnp.broadcast_to(b_ref[:, cs], (Q, _CC))
        for k in range(SSM_K):
            wk = jnp.concatenate([w_ref[8 * k:8 * k + 8, cs]] * (Q // 8), axis=0)
            tap = cur.astype(F32) if k == SSM_K // 2 else _dot(shift[k], ext)
            acc = acc + wk * tap
        y = _silu(acc)
        lo = c * _CC
        if lo < D_INNER:
            xs_ref[:, cs] = y.astype(BF16)
        elif lo < D_INNER + GROUPS * NSTATE:
            o = lo - D_INNER
            bt_ref[o:o + _CC, :] = y.T.astype(BF16)
        else:
            o = lo - D_INNER - GROUPS * NSTATE
            cm_ref[:, o:o + _CC] = y.astype(BF16)


def _ssm_conv(xbc, w, b):
    nh = Q // _HALO
    last = LTOT // _HALO - 1
    gn = GROUPS * NSTATE
    return pl.pallas_call(
        _ssm_conv_kernel,
        grid=(BATCH, NCHUNK),
        in_specs=[pl.BlockSpec((None, Q, CONV_DIM), lambda b, i: (b, i, 0)),
                  pl.BlockSpec((None, _HALO, CONV_DIM), lambda b, i: (b, jnp.maximum(i * nh - 1, 0), 0)),
                  pl.BlockSpec((None, _HALO, CONV_DIM), lambda b, i: (b, jnp.minimum(i * nh + nh, last), 0)),
                  pl.BlockSpec((SSM_K * 8, CONV_DIM), lambda b, i: (0, 0)),
                  pl.BlockSpec((1, CONV_DIM), lambda b, i: (0, 0))],
        out_specs=[pl.BlockSpec((None, Q, D_INNER), lambda b, i: (b, i, 0)),
                   pl.BlockSpec((None, gn, Q), lambda b, i: (b, 0, i)),
                   pl.BlockSpec((None, Q, gn), lambda b, i: (b, i, 0))],
        out_shape=[jax.ShapeDtypeStruct((BATCH, LTOT, D_INNER), BF16),
                   jax.ShapeDtypeStruct((BATCH, gn, LTOT), BF16),
                   jax.ShapeDtypeStruct((BATCH, LTOT, gn), BF16)],
        compiler_params=_cp(("arbitrary", "arbitrary")),
        name="ssm_conv",
    )(xbc, xbc, xbc, w, b)


def _ssd_chunk(direction, xs_ref, cm_ref, bt_ref, dt_ref, dtT_ref, arow_ref, acol_ref, rexp_ref, s_ref):
    d0 = direction * HEADS
    dtc = dt_ref[:, d0:d0 + HEADS]
    dtr = dtT_ref[d0:d0 + HEADS, :]
    da_c = dtc * arow_ref[:, d0:d0 + HEADS]
    da_r = dtr * acol_ref[d0:d0 + HEADS, :]
    ii = lax.broadcasted_iota(I32, (Q, Q), 0)
    jj = lax.broadcasted_iota(I32, (Q, Q), 1)
    if direction == 0:
        lower = jj <= ii
        tot_idx = Q - 1
    else:
        lower = jj >= ii
        tot_idx = 0
    tri_c = jnp.where(lower, 1.0, 0.0).astype(BF16)
    upper = (ii <= jj) if direction == 0 else (ii >= jj)
    tri_r = jnp.where(upper, 1.0, 0.0).astype(BF16)
    c1, c2, c3 = _split3(da_c)
    cum_c = _dot(tri_c, c1) + _dot(tri_c, c2) + _dot(tri_c, c3)
    r1, r2, r3 = _split3(da_r)
    cum_r = _dot(r1, tri_r) + _dot(r2, tri_r) + _dot(r3, tri_r)
    tot_c = cum_c[tot_idx:tot_idx + 1, :]
    tot_r = cum_r[:, tot_idx:tot_idx + 1]
    rfac = dtr * jnp.exp(tot_r - cum_r)
    dec = jnp.exp(tot_c)
    dh = dec.astype(BF16)
    dl = (dec - dh.astype(F32)).astype(BF16)
    dec_x = (_dot(jnp.broadcast_to(dh, (8, HEADS)), rexp_ref[...])
             + _dot(jnp.broadcast_to(dl, (8, HEADS)), rexp_ref[...]))[0:1, :]
    assert Q == NSTATE and 2 * HEADDIM == 128
    odd_head = lax.broadcasted_iota(I32, (Q, 128), 1) >= HEADDIM
    ys = []
    for g in range(GROUPS):
        cg = cm_ref[:, g * NSTATE:(g + 1) * NSTATE]
        btg = bt_ref[g * NSTATE:(g + 1) * NSTATE, :]
        xg = xs_ref[:, g * 256:(g + 1) * 256]
        sg = s_ref[g]
        cb = _dot(cg, btg)
        sgb = sg.astype(BF16)
        cg32 = cg.astype(F32)
        btg32 = btg.astype(F32)
        yh = []
        uh = []
        for r in range(HPG):
            h = g * HPG + r
            hs = slice((r // 2) * 128, (r // 2 + 1) * 128)
            colb = cum_c[:, h:h + 1]
            rowb = cum_r[h:h + 1, :]
            decay = jnp.exp(jnp.where(lower, colb - rowb, NEG))
            m = (cb * decay * dtr[h:h + 1, :]).astype(BF16)
            cs = (cg32 * jnp.exp(colb)).astype(BF16)
            rhs = jnp.concatenate([xg[:, hs], sgb[:, hs]], axis=0)
            res = _dot(jnp.concatenate([m, cs], axis=1), rhs)
            bw = (btg32 * rfac[h:h + 1, :]).astype(BF16)
            upd = _dot(bw, xg[:, hs])
            if r % 2 == 0:
                yh.append(res)
                uh.append(upd)
            else:
                yh[-1] = jnp.where(odd_head, res, yh[-1])
                uh[-1] = jnp.where(odd_head, upd, uh[-1])
        s_ref[g] = sg * dec_x[:, g * 256:(g + 1) * 256] + jnp.concatenate(uh, axis=1)
        ys.append(jnp.concatenate(yh, axis=1))
    return ys


def _ssd_bwd_kernel(xs_ref, cm_ref, bt_ref, dt_ref, dtT_ref, arow_ref, acol_ref, rexp_ref,
                    y_ref, s_ref):
    @pl.when(pl.program_id(1) == 0)
    def _():
        s_ref[...] = jnp.zeros_like(s_ref)
    for i in range(NB):
        ys = _ssd_chunk(1, xs_ref.at[i], cm_ref.at[i], bt_ref.at[i], dt_ref.at[i], dtT_ref.at[i],
                        arow_ref, acol_ref, rexp_ref, s_ref.at[i])
        for g in range(GROUPS):
            y_ref[i, :, g * 256:(g + 1) * 256] = ys[g].astype(BF16)


def _ssd_fwd_kernel(xs_ref, cm_ref, bt_ref, dt_ref, dtT_ref, arow_ref, acol_ref, rexp_ref,
                    z_ref, yb_ref, dsk_ref, nw_ref, y_ref, s_ref):
    @pl.when(pl.program_id(1) == 0)
    def _():
        s_ref[...] = jnp.zeros_like(s_ref)
    for i in range(NB):
        ys = _ssd_chunk(0, xs_ref.at[i], cm_ref.at[i], bt_ref.at[i], dt_ref.at[i], dtT_ref.at[i],
                        arow_ref, acol_ref, rexp_ref, s_ref.at[i])
        for g in range(GROUPS):
            gs = slice(g * 256, (g + 1) * 256)
            y = ys[g] + yb_ref[i, :, gs].astype(F32) + xs_ref[i, :, gs].astype(F32) * dsk_ref[:, gs]
            y = y * _silu(z_ref[i, :, gs].astype(F32))
            y = y * lax.rsqrt(jnp.mean(y * y, axis=-1, keepdims=True) + EPS) * nw_ref[:, gs]
            y_ref[i, :, gs] = y.astype(BF16)


NB = 2


def _ssd_specs(cmap):
    gn = GROUPS * NSTATE
    full = lambda b, j: (0, 0)
    return [pl.BlockSpec((NB, Q, D_INNER), lambda b, j: (b, cmap(j), 0)),
            pl.BlockSpec((NB, Q, gn), lambda b, j: (b, cmap(j), 0)),
            pl.BlockSpec((NB, gn, Q), lambda b, j: (b, 0, cmap(j))),
            pl.BlockSpec((NB, Q, 2 * HEADS), lambda b, j: (b, cmap(j), 0)),
            pl.BlockSpec((NB, 2 * HEADS, Q), lambda b, j: (b, 0, cmap(j))),
            pl.BlockSpec((1, 2 * HEADS), full),
            pl.BlockSpec((2 * HEADS, 1), full),
            pl.BlockSpec((HEADS, D_INNER), full)]


def _ssd_bwd(xs, cm, bt, dt, dtT, arow, acol, rexp):
    cmap = lambda j: jnp.where(j < CTX_CHUNKS, CTX_CHUNKS - 1 - j, NCHUNK + CTX_CHUNKS - 1 - j)
    omap = lambda b, j: (b, NCHUNK - 1 - jnp.maximum(j, CTX_CHUNKS), 0)
    return pl.pallas_call(
        _ssd_bwd_kernel,
        grid=(BATCH // NB, NCHUNK),
        in_specs=_ssd_specs(cmap),
        out_specs=pl.BlockSpec((NB, Q, D_INNER), omap),
        out_shape=jax.ShapeDtypeStruct((BATCH, SEQ, D_INNER), BF16),
        scratch_shapes=[pltpu.VMEM((NB, GROUPS, NSTATE, HPG * HEADDIM), F32)],
        compiler_params=_cp(("arbitrary", "arbitrary")),
        name="ssd_bwd",
    )(xs, cm, bt, dt, dtT, arow, acol, rexp)


def _ssd_fwd(xs, cm, bt, dt, dtT, arow, acol, rexp, z, yb, dsk, nw):
    cmap = lambda j: j
    lat = lambda b, j: (b, jnp.maximum(j - CTX_CHUNKS, 0), 0)
    full = lambda b, j: (0, 0)
    return pl.pallas_call(
        _ssd_fwd_kernel,
        grid=(BATCH // NB, NCHUNK),
        in_specs=_ssd_specs(cmap) + [
            pl.BlockSpec((NB, Q, D_INNER), lambda b, j: (b, j, 0)),
            pl.BlockSpec((NB, Q, D_INNER), lat),
            pl.BlockSpec((1, D_INNER), full),
            pl.BlockSpec((1, D_INNER), full)],
        out_specs=pl.BlockSpec((NB, Q, D_INNER), lat),
        out_shape=jax.ShapeDtypeStruct((BATCH, SEQ, D_INNER), BF16),
        scratch_shapes=[pltpu.VMEM((NB, GROUPS, NSTATE, HPG * HEADDIM), F32)],
        compiler_params=_cp(("arbitrary", "arbitrary")),
        name="ssd_fwd",
    )(xs, cm, bt, dt, dtT, arow, acol, rexp, z, yb, dsk, nw)


def _route(h, rwT_ref, rb_ref, cnt_ref, eidx_ref, pos_ref, gate_ref):
    hh = h.astype(BF16)
    hl = (h - hh.astype(F32)).astype(BF16)
    w = rwT_ref[...]
    wh = w.astype(BF16)
    wl = (w - wh.astype(F32)).astype(BF16)
    logits = _dot_nt(wh, hh) + _dot_nt(wh, hl) + _dot_nt(wl, hh)
    scores = jax.nn.sigmoid(logits)
    sel = scores + rb_ref[...]
    per = E // NGRP
    sub = lax.broadcasted_iota(I32, (per, TM), 0)
    gscore = []
    for g in range(NGRP):
        blk = sel[g * per:(g + 1) * per, :]
        m1 = jnp.max(blk, axis=0, keepdims=True)
        first = jnp.min(jnp.where(blk == m1, sub, per), axis=0, keepdims=True)
        m2 = jnp.max(jnp.where(sub == first, -jnp.inf, blk), axis=0, keepdims=True)
        gscore.append(m1 + m2)
    masked = []
    for g in range(NGRP):
        rank = jnp.zeros((1, TM), F32)
        for o in range(NGRP):
            if o == g:
                continue
            ahead = (gscore[o] >= gscore[g]) if o < g else (gscore[o] > gscore[g])
            rank = rank + jnp.where(ahead, 1.0, 0.0)
        blk = sel[g * per:(g + 1) * per, :]
        masked.append(jnp.where(rank < TOPG, blk, -jnp.inf))
    v = jnp.concatenate(masked, axis=0)
    eio = lax.broadcasted_iota(I32, (E, TM), 0)
    kio = lax.broadcasted_iota(I32, (TOPK, TM), 0)
    eidx = jnp.zeros((TOPK, TM), I32)
    gsc = jnp.zeros((TOPK, TM), F32)
    hot = jnp.zeros((E, TM), F32)
    picks = []
    for k in range(TOPK):
        m = jnp.max(v, axis=0, keepdims=True)
        first = jnp.min(jnp.where(v == m, eio, E), axis=0, keepdims=True)
        pick = eio == first
        sc = jnp.sum(jnp.where(pick, scores, 0.0), axis=0, keepdims=True)
        eidx = jnp.where(kio == k, first, eidx)
        gsc = jnp.where(kio == k, sc, gsc)
        hot = jnp.where(pick, 1.0, hot)
        v = jnp.where(pick, -jnp.inf, v)
        picks.append(pick)
    gate = gsc / jnp.sum(gsc, axis=0, keepdims=True) * ROUTED_SCALE
    ti = lax.broadcasted_iota(I32, (TM, TM), 0)
    tj = lax.broadcasted_iota(I32, (TM, TM), 1)
    before = jnp.where(ti < tj, 1.0, 0.0).astype(BF16)
    posfull = _dot(hot.astype(BF16), before) + cnt_ref[...]
    pos = jnp.zeros((TOPK, TM), F32)
    for k in range(TOPK):
        pk = jnp.sum(jnp.where(picks[k], posfull, 0.0), axis=0, keepdims=True)
        pos = jnp.where(kio == k, pk, pos)
    cnt_ref[...] = cnt_ref[...] + jnp.sum(hot, axis=1, keepdims=True)
    eidx_ref[...] = eidx
    pos_ref[...] = pos.astype(I32)
    eye = jnp.where(ti == tj, 1.0, 0.0).astype(BF16)
    g1, g2, g3 = _split3(gate)
    gate_ref[...] = _dot_nt(eye, g1) + _dot_nt(eye, g2) + _dot_nt(eye, g3)


def _mix_epilogue(y, x_ref, mod_ref, gpost_ref, gpre_ref, rwT_ref, rb_ref,
                  x1_ref, h2_ref, eidx_ref, pos_ref, gate_ref, cnt_out_ref, cnt_ref):
    first = (pl.program_id(0) == 0) & (pl.program_id(1) == 0)

    @pl.when(first)
    def _():
        cnt_ref[...] = jnp.zeros_like(cnt_ref)
    x1 = x_ref[...] + mod_ref[2:3, :] * _rms(y, gpost_ref[...])
    x1_ref[...] = x1
    h2 = _rms(x1, gpre_ref[...]) * (1.0 + mod_ref[4:5, :]) + mod_ref[3:4, :]
    _store_packed(h2_ref, h2)
    _route(h2, rwT_ref, rb_ref, cnt_ref, eidx_ref, pos_ref, gate_ref)
    cnt_out_ref[...] = cnt_ref[...]


def _ssm_out_kernel(y_ref, w_ref, x_ref, mod_ref, gpost_ref, gpre_ref, rwT_ref, rb_ref,
                    x1_ref, h2_ref, eidx_ref, pos_ref, gate_ref, cnt_out_ref, cnt_ref):
    y = _dot(y_ref[...], w_ref[...])
    _mix_epilogue(y, x_ref, mod_ref, gpost_ref, gpre_ref, rwT_ref, rb_ref,
                  x1_ref, h2_ref, eidx_ref, pos_ref, gate_ref, cnt_out_ref, cnt_ref)


_CW = 256
_PAD = 16


def _conf_out_kernel(u_ref, dww_ref, dwb_ref, lng_ref, lnb_ref, w_ref, b_ref,
                     x_ref, mod_ref, gpost_ref, gpre_ref, rwT_ref, rb_ref,
                     x1_ref, h2_ref, eidx_ref, pos_ref, gate_ref, cnt_out_ref,
                     cnt_ref, v_ref):
    nrow = TM // GRID_W
    ext_rows = GRID_W + 2 * _PAD
    first = _PAD - CONF_K // 2
    span = GRID_W + 8 * ((CONF_K - 1) // 8)
    assert first + 7 + span <= ext_rows
    ri = lax.broadcasted_iota(I32, (span, ext_rows), 0)
    ji = lax.broadcasted_iota(I32, (span, ext_rows), 1)
    shift = [jnp.where(ji == ri + first + b, 1.0, 0.0).astype(BF16) for b in range(8)]
    zpad = jnp.zeros((_PAD, D), BF16)
    for r in range(nrow):
        ext = jnp.concatenate([zpad, u_ref[r * GRID_W:(r + 1) * GRID_W, :], zpad], axis=0)
        for c in range(D // _CW):
            cs = slice(c * _CW, (c + 1) * _CW)
            acc = jnp.broadcast_to(dwb_ref[:, cs], (GRID_W, _CW))
            for b in range(8):
                win = _dot(shift[b], ext[:, cs])
                for a in range(-(-CONF_K // 8)):
                    k = 8 * a + b
                    if k < CONF_K:
                        wk = jnp.concatenate([dww_ref[8 * k:8 * k + 8, cs]] * (GRID_W // 8), axis=0)
                        acc = acc + wk * win[8 * a:8 * a + GRID_W, :]
            v_ref[r * GRID_W:(r + 1) * GRID_W, cs] = acc
    v = v_ref[...]
    mu = jnp.mean(v, axis=-1, keepdims=True)
    vc = v - mu
    ln = vc * lax.rsqrt(jnp.mean(vc * vc, axis=-1, keepdims=True) + EPS) * lng_ref[...] + lnb_ref[...]
    y = _dot(_silu(ln).astype(BF16), w_ref[...]) + b_ref[...]
    _mix_epilogue(y, x_ref, mod_ref, gpost_ref, gpre_ref, rwT_ref, rb_ref,
                  x1_ref, h2_ref, eidx_ref, pos_ref, gate_ref, cnt_out_ref, cnt_ref)


def _mix_out_common_specs():
    full = lambda b, i: (0, 0)
    nt = SEQ // TM
    in_specs = [pl.BlockSpec((None, TM, D), lambda b, i: (b, i, 0)),
                pl.BlockSpec((None, 6, D), lambda b, i: (b, 0, 0)),
                pl.BlockSpec((1, D), full), pl.BlockSpec((1, D), full),
                pl.BlockSpec((E, D), full), pl.BlockSpec((E, 1), full)]
    out_specs = [pl.BlockSpec((None, TM, D), lambda b, i: (b, i, 0)),
                 pl.BlockSpec((TM * RW, 128), lambda b, i: (b * nt + i, 0)),
                 pl.BlockSpec((TOPK, TM), lambda b, i: (0, b * nt + i)),
                 pl.BlockSpec((TOPK, TM), lambda b, i: (0, b * nt + i)),
                 pl.BlockSpec((TM, TOPK), lambda b, i: (b * nt + i, 0)),
                 pl.BlockSpec((E, 1), full)]
    out_shape = [jax.ShapeDtypeStruct((BATCH, SEQ, D), F32),
                 jax.ShapeDtypeStruct((T * RW, 128), U32),
                 jax.ShapeDtypeStruct((TOPK, T), I32),
                 jax.ShapeDtypeStruct((TOPK, T), I32),
                 jax.ShapeDtypeStruct((T, TOPK), F32),
                 jax.ShapeDtypeStruct((E, 1), F32)]
    return in_specs, out_specs, out_shape


def _ssm_out(y, w, x, mod, gpost, gpre, rwT, rb):
    common_in, out_specs, out_shape = _mix_out_common_specs()
    return pl.pallas_call(
        _ssm_out_kernel,
        grid=(BATCH, SEQ // TM),
        in_specs=[pl.BlockSpec((None, TM, D_INNER), lambda b, i: (b, i, 0)),
                  pl.BlockSpec((D_INNER, D), lambda b, i: (0, 0))] + common_in,
        out_specs=out_specs, out_shape=out_shape,
        scratch_shapes=[pltpu.VMEM((E, 1), F32)],
        compiler_params=_cp(("arbitrary", "arbitrary")),
        name="ssm_out",
    )(y, w, x, mod, gpost, gpre, rwT, rb)


def _conf_out(u, dww, dwb, lng, lnb, w, b, x, mod, gpost, gpre, rwT, rb):
    common_in, out_specs, out_shape = _mix_out_common_specs()
    full = lambda b_, i: (0, 0)
    return pl.pallas_call(
        _conf_out_kernel,
        grid=(BATCH, SEQ // TM),
        in_specs=[pl.BlockSpec((None, TM, D), lambda b_, i: (b_, i, 0)),
                  pl.BlockSpec((CONF_K * 8, D), full), pl.BlockSpec((1, D), full),
                  pl.BlockSpec((1, D), full), pl.BlockSpec((1, D), full),
                  pl.BlockSpec((D, D), full), pl.BlockSpec((1, D), full)] + common_in,
        out_specs=out_specs, out_shape=out_shape,
        scratch_shapes=[pltpu.VMEM((E, 1), F32), pltpu.VMEM((TM, D), F32)],
        compiler_params=_cp(("arbitrary", "arbitrary")),
        name="conf_out",
    )(u, dww, dwb, lng, lnb, w, b, x, mod, gpost, gpre, rwT, rb)


def _conf_in_kernel(x_ref, mod_ref, g_ref, wa_ref, wg_ref, ba_ref, bg_ref, u_ref):
    h = _rms(x_ref[...], g_ref[...]) * (1.0 + mod_ref[1:2, :]) + mod_ref[0:1, :]
    hb = h.astype(BF16)
    a = _dot(hb, wa_ref[...]) + ba_ref[...]
    g = _dot(hb, wg_ref[...]) + bg_ref[...]
    u_ref[...] = (a * jax.nn.sigmoid(g)).astype(BF16)


def _conf_in(x, mod, gain, w, bias):
    full = lambda b, i: (0, 0)
    return pl.pallas_call(
        _conf_in_kernel,
        grid=(BATCH, SEQ // TM),
        in_specs=[pl.BlockSpec((None, TM, D), lambda b, i: (b, i, 0)),
                  pl.BlockSpec((None, 6, D), lambda b, i: (b, 0, 0)),
                  pl.BlockSpec((1, D), full),
                  pl.BlockSpec((D, D), full), pl.BlockSpec((D, D), lambda b, i: (0, 1)),
                  pl.BlockSpec((1, D), full), pl.BlockSpec((1, D), lambda b, i: (0, 1))],
        out_specs=pl.BlockSpec((None, TM, D), lambda b, i: (b, i, 0)),
        out_shape=jax.ShapeDtypeStruct((BATCH, SEQ, D), BF16),
        compiler_params=_cp(("arbitrary", "arbitrary")),
        name="conf_in",
    )(x, mod, gain, w, w, bias, bias)


def _row_copy(src, si, dst, di, sem):
    return pltpu.make_async_copy(src.at[pl.ds(pl.multiple_of(si * RW, RW), RW), :],
                                 dst.at[pl.ds(pl.multiple_of(di * RW, RW), RW), :], sem)


ZROWS = BLK + 8


def _dispatch_kernel(zs_ref, dest_ref, h_ref, xs_ref, zero_ref, zsem, sem):
    @pl.when(pl.program_id(0) == 0)
    def _():
        zero_ref[...] = jnp.zeros_like(zero_ref)

        def zcopy(e):
            start = pl.multiple_of(zs_ref[e] * RW, 8 * RW)
            return pltpu.make_async_copy(zero_ref, xs_ref.at[pl.ds(start, ZROWS * RW), :], zsem)

        def zstart(e, c):
            zcopy(e).start()
            return c

        def zwait(e, c):
            zcopy(e).wait()
            return c
        lax.fori_loop(0, E, zstart, 0)
        lax.fori_loop(0, E, zwait, 0)

    def issue(t, c):
        for k in range(TOPK):
            _row_copy(h_ref, t, xs_ref, dest_ref[t * TOPK + k], sem).start(priority=k % 2)
        return c

    def drain(t, c):
        for k in range(TOPK):
            _row_copy(h_ref, t, xs_ref, 0, sem).wait()
        return c
    lax.fori_loop(0, TM, issue, 0)
    lax.fori_loop(0, TM, drain, 0)


def _dispatch(zstart, dest, hp):
    grid_spec = pltpu.PrefetchScalarGridSpec(
        num_scalar_prefetch=1,
        grid=(T // TM,),
        in_specs=[pl.BlockSpec((TM * TOPK,), lambda i, zs: (i,), memory_space=pltpu.SMEM),
                  pl.BlockSpec((TM * RW, 128), lambda i, zs: (i, 0))],
        out_specs=pl.BlockSpec(memory_space=pl.ANY),
        scratch_shapes=[pltpu.VMEM((ZROWS * RW, 128), U32), pltpu.SemaphoreType.DMA(()),
                        pltpu.SemaphoreType.DMA(())])
    return pl.pallas_call(
        _dispatch_kernel,
        grid_spec=grid_spec,
        out_shape=jax.ShapeDtypeStruct((PROWS * RW, 128), U32),
        compiler_params=_cp(("arbitrary",)),
        name="moe_dispatch",
    )(zstart, dest, hp)


XRING = 3


def _experts_kernel(be_ref, na_ref, x_hbm, wg_ref, wu_ref, wd_ref, y_ref, wgb_ref, wub_ref, wdb_ref,
                    xbuf_ref, xsem):
    i = pl.program_id(0)
    na = na_ref[0]

    def fetch(j):
        rows = BLK * RW
        src = x_hbm.at[pl.ds(pl.multiple_of(j * rows, rows), rows), :]
        return pltpu.make_async_copy(src, xbuf_ref.at[j % XRING], xsem.at[j % XRING])

    @pl.when(i == 0)
    def _():
        for j in range(XRING - 1):
            @pl.when(j < na)
            def _():
                fetch(j).start()

    @pl.when(i + XRING - 1 < na)
    def _():
        fetch(i + XRING - 1).start()

    @pl.when((i == 0) | (be_ref[i] != be_ref[jnp.maximum(i - 1, 0)]))
    def _():
        wgb_ref[...] = wg_ref[...].astype(BF16)
        wub_ref[...] = wu_ref[...].astype(BF16)
        wdb_ref[...] = wd_ref[...].astype(BF16)

    @pl.when(i < na)
    def _():
        fetch(i).wait()
        xb = _load_packed(xbuf_ref.at[i % XRING], BLK)
        g = _dot(xb, wgb_ref[...])
        u = _dot(xb, wub_ref[...])
        a = (_silu(g) * u).astype(BF16)
        _store_packed(y_ref, _dot(a, wdb_ref[...]))

    @pl.when(i >= na)
    def _():
        y_ref[...] = jnp.zeros_like(y_ref)


def _experts(layer, block_e, nact, xs, wg, wu, wd):
    wmap = lambda i, be, na: (layer, be[i], 0, 0)
    grid_spec = pltpu.PrefetchScalarGridSpec(
        num_scalar_prefetch=2,
        grid=(NBLK,),
        in_specs=[pl.BlockSpec(memory_space=pl.ANY),
                  pl.BlockSpec((None, None, D, DE), wmap),
                  pl.BlockSpec((None, None, D, DE), wmap),
                  pl.BlockSpec((None, None, DE, D), wmap)],
        out_specs=pl.BlockSpec((BLK * RW, 128), lambda i, be, na: (i, 0)),
        scratch_shapes=[pltpu.VMEM((D, DE), BF16), pltpu.VMEM((D, DE), BF16),
                        pltpu.VMEM((DE, D), BF16),
                        pltpu.VMEM((XRING, BLK * RW, 128), U32), pltpu.SemaphoreType.DMA((XRING,))])
    return pl.pallas_call(
        _experts_kernel,
        grid_spec=grid_spec,
        out_shape=jax.ShapeDtypeStruct((NBLK * BLK * RW, 128), U32),
        compiler_params=_cp(("arbitrary",)),
        name="moe_experts",
    )(block_e, nact, xs, wg, wu, wd)


def _combine_kernel(dcur_ref, dnxt_ref, gate_ref, h_ref, x1_ref, mod_ref, gpost_ref,
                    sg_ref, su_ref, sd_ref, ys_ref, o_ref, buf_ref, sems):
    i = pl.program_id(0)
    n = pl.num_programs(0)

    def gather(dest_ref, slot):
        def body(t, c):
            for k in range(TOPK):
                _row_copy(ys_ref, dest_ref[t * TOPK + k], buf_ref.at[slot, k], t,
                          sems.at[slot]).start(priority=k % 2)
            return c
        lax.fori_loop(0, TM, body, 0)

    @pl.when(i == 0)
    def _():
        gather(dcur_ref, 0)

    @pl.when(i + 1 < n)
    def _():
        gather(dnxt_ref, (i + 1) % 2)

    hb = _load_packed(h_ref, TM)
    a = (_silu(_dot(hb, sg_ref[...])) * _dot(hb, su_ref[...])).astype(BF16)
    f = _dot(a, sd_ref[...])

    slot = i % 2

    def drain(t, c):
        for k in range(TOPK):
            _row_copy(ys_ref, 0, buf_ref.at[slot, k], 0, sems.at[slot]).wait()
        return c
    lax.fori_loop(0, TM, drain, 0)

    gate = gate_ref[...]
    gb = [jnp.broadcast_to(gate[:, k:k + 1], (TM, 128)) for k in range(TOPK)]
    los, his = [], []
    for s in range(RW):
        lo = f[:, s * 128:(s + 1) * 128]
        hi = f[:, D // 2 + s * 128:D // 2 + (s + 1) * 128]
        for k in range(TOPK):
            wl, wh = _unpack2(buf_ref[slot, k, pl.ds(s, TM, stride=RW), :])
            lo = lo + gb[k] * wl
            hi = hi + gb[k] * wh
        los.append(lo)
        his.append(hi)
    f = jnp.concatenate(los + his, axis=1)
    o_ref[...] = x1_ref[...] + mod_ref[5:6, :] * _rms(f, gpost_ref[...])


def _combine(dest, gate, hp, x1, mod, gpost, sg, su, sd, ys):
    nt = SEQ // TM
    n = T // TM
    full = lambda i: (0, 0)
    return pl.pallas_call(
        _combine_kernel,
        grid=(n,),
        in_specs=[pl.BlockSpec((TM * TOPK,), lambda i: (i,), memory_space=pltpu.SMEM),
                  pl.BlockSpec((TM * TOPK,), lambda i: (jnp.minimum(i + 1, n - 1),),
                               memory_space=pltpu.SMEM),
                  pl.BlockSpec((TM, TOPK), lambda i: (i, 0)),
                  pl.BlockSpec((TM * RW, 128), lambda i: (i, 0)),
                  pl.BlockSpec((TM, D), lambda i: (i, 0)),
                  pl.BlockSpec((None, 6, D), lambda i: (i // nt, 0, 0)),
                  pl.BlockSpec((1, D), full),
                  pl.BlockSpec((D, DE), full), pl.BlockSpec((D, DE), full), pl.BlockSpec((DE, D), full),
                  pl.BlockSpec(memory_space=pl.ANY)],
        out_specs=pl.BlockSpec((TM, D), lambda i: (i, 0)),
        out_shape=jax.ShapeDtypeStruct((T, D), F32),
        scratch_shapes=[pltpu.VMEM((2, TOPK, TM * RW, 128), U32), pltpu.SemaphoreType.DMA((2,))],
        compiler_params=_cp(("arbitrary",)),
        name="moe_combine",
    )(dest, dest, gate, hp, x1.reshape(T, D), mod, gpost, sg, su, sd, ys).reshape(BATCH, SEQ, D)


def _moe(layer, x1, hp, eidx, pos, gate, counts, mod, gpost, wg, wu, wd, sg, su, sd):
    cnt = counts.reshape(E).astype(I32)
    padded = (cnt + BLK - 1) // BLK * BLK
    pend = jnp.cumsum(padded)
    pstart = pend - padded
    ids = jnp.arange(E, dtype=I32)
    dest = pos + jnp.sum(jnp.where(eidx[:, :, None] == ids, pstart, 0), axis=-1)
    dest = dest.T.reshape(T * TOPK)
    nact = (pend[-1] // BLK).reshape(1).astype(I32)
    blk0 = jnp.arange(NBLK, dtype=I32) * BLK
    block_e = jnp.sum((pend[None, :] <= blk0[:, None]).astype(I32), axis=1)
    block_e = jnp.minimum(block_e, E - 1).astype(I32)
    xs = _dispatch(((pstart + cnt) // 8 * 8).astype(I32), dest, hp)
    ys = _experts(layer, block_e, nact, xs, wg, wu, wd)
    return _combine(dest, gate, hp, x1, mod, gpost, sg.astype(BF16), su.astype(BF16),
                    sd.astype(BF16), ys)


def kernel(x, c, ctx, c_ctx, ada_w, ada_b, norm_mix_pre, norm_mix_post, norm_ffn_pre, norm_ffn_post, ssm_w_in, ssm_conv_w, ssm_conv_b, ssm_dt_bias, ssm_a_log, ssm_d, ssm_norm, ssm_w_out, cv_w_in, cv_b_in, cv_dw_w, cv_dw_b, cv_ln_g, cv_ln_b, cv_w_out, cv_b_out, router_w, router_b, exp_w_gate, exp_w_up, exp_w_down, sh_w_gate, sh_w_up, sh_w_down):
    row = lambda v: v.reshape(1, -1)
    cvec = jnp.concatenate([c, c_ctx[None, :], jnp.zeros((3, D), F32)], axis=0)
    mod = _ada(cvec, ada_w, ada_b).reshape(2, 8, 6, D)

    w_in = ssm_w_in[0].astype(BF16)
    dtb = ssm_dt_bias[0].reshape(1, 2 * HEADS)
    wdt = w_in[:, D_INNER + CONV_DIM:]
    z, xbc, dt, dtT = _ssm_in(ctx, x, mod[0], row(norm_mix_pre[0]), w_in, wdt, wdt.T,
                              dtb, dtb.reshape(2 * HEADS, 1))
    xs, bt, cm = _ssm_conv(xbc, jnp.repeat(ssm_conv_w[0], 8, axis=0), row(ssm_conv_b[0]))
    a = -jnp.exp(ssm_a_log[0].astype(F32)).reshape(1, 2 * HEADS)
    rexp = (jnp.arange(D_INNER)[None, :] // HEADDIM == jnp.arange(HEADS)[:, None]).astype(BF16)
    yb = _ssd_bwd(xs, cm, bt, dt, dtT, a, a.reshape(2 * HEADS, 1), rexp)
    dsk = jnp.repeat(ssm_d[0], HEADDIM).reshape(1, D_INNER)
    ygn = _ssd_fwd(xs, cm, bt, dt, dtT, a, a.reshape(2 * HEADS, 1), rexp, z, yb, dsk,
                   row(ssm_norm[0]))
    x1, h2, eidx, pos, gate, counts = _ssm_out(
        ygn, ssm_w_out[0].astype(BF16), x, mod[0], row(norm_mix_post[0]), row(norm_ffn_pre[0]),
        router_w[0].T, router_b[0].reshape(E, 1))
    x2 = _moe(0, x1, h2, eidx, pos, gate, counts, mod[0], row(norm_ffn_post[0]),
              exp_w_gate, exp_w_up, exp_w_down, sh_w_gate[0], sh_w_up[0], sh_w_down[0])

    u = _conf_in(x2, mod[1], row(norm_mix_pre[1]), cv_w_in[0].astype(BF16), row(cv_b_in[0]))
    x3, h4, eidx, pos, gate, counts = _conf_out(
        u, jnp.repeat(cv_dw_w[0], 8, axis=0), row(cv_dw_b[0]), row(cv_ln_g[0]), row(cv_ln_b[0]),
        cv_w_out[0].astype(BF16), row(cv_b_out[0]),
        x2, mod[1], row(norm_mix_post[1]), row(norm_ffn_pre[1]),
        router_w[1].T, router_b[1].reshape(E, 1))
    return _moe(1, x3, h4, eidx, pos, gate, counts, mod[1], row(norm_ffn_post[1]),
                exp_w_gate, exp_w_up, exp_w_down, sh_w_gate[1], sh_w_up[1], sh_w_down[1])
```

```python
import functools

import jax
import jax.numpy as jnp
from jax import lax
from jax.experimental import pallas as pl
from jax.experimental.pallas import tpu as pltpu

F32 = jnp.float32
BF16 = jnp.bfloat16
I32 = jnp.int32

D = 1024
BATCH = 4
SEQ = 4096
CTX = 256
LTOT = CTX + SEQ
GRID_W = 64

D_INNER = 2048
HEADS = 32
GROUPS = 8
HPG = 4
HEADDIM = 64
NSTATE = 128
Q = 128
NCHUNK = LTOT // Q
CTX_CHUNKS = CTX // Q
CONV_DIM = D_INNER + 2 * GROUPS * NSTATE
SSM_K = 5
CONF_K = 31

E = 64
TOPK = 8
NGRP = 8
TOPG = 4
DE = 256
ROUTED_SCALE = 2.5
EPS = 1e-6

T = BATCH * SEQ
TM = 512
BLK = 512
NBLK = -(-(T * TOPK + E * (BLK - 1)) // BLK)
PROWS = (NBLK + 1) * BLK + 8

VMEM_LIMIT = 56 * 1024 * 1024
NEG = -1e30


def _cp(sem):
    return pltpu.CompilerParams(dimension_semantics=sem, vmem_limit_bytes=VMEM_LIMIT)


def _silu(v):
    return v * jax.nn.sigmoid(v)


def _rms(v, g):
    return v * lax.rsqrt(jnp.mean(v * v, axis=-1, keepdims=True) + EPS) * g


def _split3(v):
    a = v.astype(BF16)
    r = v - a.astype(F32)
    b = r.astype(BF16)
    c = (r - b.astype(F32)).astype(BF16)
    return a, b, c


def _dot(a, b):
    return jnp.dot(a, b, preferred_element_type=F32)


def _dot_nt(a, b):
    return lax.dot_general(a, b, (((1,), (1,)), ((), ())), preferred_element_type=F32)


U32 = jnp.uint32
RW = D // 2 // 128
_HI = 0xFFFF0000


def _pack2(lo, hi):
    ul = pltpu.bitcast(lo.astype(BF16).astype(F32), U32)
    uh = pltpu.bitcast(hi.astype(BF16).astype(F32), U32)
    return (ul >> 16) | (uh & U32(_HI))


def _unpack2(w):
    return pltpu.bitcast(w << 16, F32), pltpu.bitcast(w & U32(_HI), F32)


def _store_packed(ref, v):
    m = v.shape[0]
    for s in range(RW):
        lo = v[:, s * 128:(s + 1) * 128]
        hi = v[:, D // 2 + s * 128:D // 2 + (s + 1) * 128]
        ref[pl.ds(s, m, stride=RW), :] = _pack2(lo, hi)


def _load_packed(ref, m):
    los, his = [], []
    for s in range(RW):
        lo, hi = _unpack2(ref[pl.ds(s, m, stride=RW), :])
        los.append(lo.astype(BF16))
        his.append(hi.astype(BF16))
    return jnp.concatenate(los + his, axis=1)


def _ada_kernel(c_ref, w_ref, b_ref, o_ref):
    s = _silu(c_ref[...])
    o_ref[...] = jnp.dot(s, w_ref[...], preferred_element_type=F32,
                         precision=lax.Precision.HIGHEST) + b_ref[...]


def _ada(cvec, ada_w, ada_b):
    depth = ada_w.shape[0]
    tn = 1536
    return pl.pallas_call(
        _ada_kernel,
        grid=(depth, 6 * D // tn),
        in_specs=[pl.BlockSpec((8, D), lambda l, j: (0, 0)),
                  pl.BlockSpec((None, D, tn), lambda l, j: (l, 0, j)),
                  pl.BlockSpec((None, 1, tn), lambda l, j: (l, 0, j))],
        out_specs=pl.BlockSpec((None, 8, tn), lambda l, j: (l, 0, j)),
        out_shape=jax.ShapeDtypeStruct((depth, 8, 6 * D), F32),
        compiler_params=_cp(("arbitrary", "arbitrary")),
        name="ada",
    )(cvec, ada_w, ada_b.reshape(depth, 1, 6 * D))


def _ssm_in_kernel(c_ref, x_ref, mod_ref, g_ref, wz_ref, wx0_ref, wx1_ref, wdt_ref, wdtT_ref, dtb_ref,
                   dtbT_ref, z_ref, xbc_ref, dt_ref, dtT_ref):
    xin = jnp.where(pl.program_id(1) == 0, c_ref[...], x_ref[...])
    h = _rms(xin, g_ref[...]) * (1.0 + mod_ref[1:2, :]) + mod_ref[0:1, :]
    hb = h.astype(BF16)
    z_ref[...] = _dot(hb, wz_ref[...]).astype(BF16)
    xbc_ref[:, :D_INNER] = _dot(hb, wx0_ref[...]).astype(BF16)
    xbc_ref[:, D_INNER:] = _dot(hb, wx1_ref[...]).astype(BF16)
    dt_ref[...] = jax.nn.softplus(_dot(hb, wdt_ref[...]) + dtb_ref[...])
    dtT_ref[...] = jax.nn.softplus(_dot_nt(wdtT_ref[...], hb) + dtbT_ref[...])


def _ssm_in(ctx, x, mod, gain, w_in, wdt, wdtT, dtb, dtbT):
    ts = CTX
    nt = LTOT // ts
    full = lambda b, i: (0, 0)
    assert CONV_DIM == 2 * D_INNER and SEQ % ts == 0
    return pl.pallas_call(
        _ssm_in_kernel,
        grid=(BATCH, nt),
        in_specs=[pl.BlockSpec((None, ts, D), lambda b, i: (b, 0, 0)),
                  pl.BlockSpec((None, ts, D), lambda b, i: (b, jnp.maximum(i - 1, 0), 0)),
                  pl.BlockSpec((None, 6, D), lambda b, i: (jnp.where(i == 0, BATCH, b), 0, 0)),
                  pl.BlockSpec((1, D), full),
                  pl.BlockSpec((D, D_INNER), lambda b, i: (0, 0)),
                  pl.BlockSpec((D, D_INNER), lambda b, i: (0, 1)),
                  pl.BlockSpec((D, D_INNER), lambda b, i: (0, 2)),
                  pl.BlockSpec((D, 2 * HEADS), full),
                  pl.BlockSpec((2 * HEADS, D), full),
                  pl.BlockSpec((1, 2 * HEADS), full),
                  pl.BlockSpec((2 * HEADS, 1), full)],
        out_specs=[pl.BlockSpec((None, ts, D_INNER), lambda b, i: (b, i, 0)),
                   pl.BlockSpec((None, ts, CONV_DIM), lambda b, i: (b, i, 0)),
                   pl.BlockSpec((None, ts, 2 * HEADS), lambda b, i: (b, i, 0)),
                   pl.BlockSpec((None, 2 * HEADS, ts), lambda b, i: (b, 0, i))],
        out_shape=[jax.ShapeDtypeStruct((BATCH, LTOT, D_INNER), BF16),
                   jax.ShapeDtypeStruct((BATCH, LTOT, CONV_DIM), BF16),
                   jax.ShapeDtypeStruct((BATCH, LTOT, 2 * HEADS), F32),
                   jax.ShapeDtypeStruct((BATCH, 2 * HEADS, LTOT), F32)],
        compiler_params=_cp(("arbitrary", "arbitrary")),
        name="ssm_in",
    )(ctx, x, mod, gain, w_in, w_in, w_in, wdt, wdtT, dtb, dtbT)


_HALO = 16
_CC = 512


def _ssm_conv_kernel(x_ref, xp_ref, xn_ref, w_ref, b_ref, xs_ref, bt_ref, cm_ref):
    i = pl.program_id(1)
    first_lat = CTX // Q
    pvalid = (i != 0) & (i != first_lat)
    nvalid = (i != first_lat - 1) & (i != NCHUNK - 1)
    li = lax.broadcasted_iota(I32, (Q, Q + 2 * _HALO), 0)
    ji = lax.broadcasted_iota(I32, (Q, Q + 2 * _HALO), 1)
    shift = {k: jnp.where(ji == li + _HALO + k - SSM_K // 2, 1.0, 0.0).astype(BF16)
             for k in range(SSM_K) if k != SSM_K // 2}
    for c in range(CONV_DIM // _CC):
        cs = slice(c * _CC, (c + 1) * _CC)
        cur = x_ref[:, cs]
        prev = jnp.where(pvalid, xp_ref[:, cs], jnp.zeros((_HALO, _CC), BF16))
        nxt = jnp.where(nvalid, xn_ref[:, cs], jnp.zeros((_HALO, _CC), BF16))
        ext = jnp.concatenate([prev, cur, nxt], axis=0)
        acc = jnp.broadcast_to(b_ref[:, cs], (Q, _CC))
        for k in range(SSM_K):
            wk = jnp.concatenate([w_ref[8 * k:8 * k + 8, cs]] * (Q // 8), axis=0)
            tap = cur.astype(F32) if k == SSM_K // 2 else _dot(shift[k], ext)
            acc = acc + wk * tap
        y = _silu(acc)
        lo = c * _CC
        if lo < D_INNER:
            xs_ref[:, cs] = y.astype(BF16)
        elif lo < D_INNER + GROUPS * NSTATE:
            o = lo - D_INNER
            bt_ref[o:o + _CC, :] = y.T.astype(BF16)
        else:
            o = lo - D_INNER - GROUPS * NSTATE
            cm_ref[:, o:o + _CC] = y.astype(BF16)


def _ssm_conv(xbc, w, b):
    nh = Q // _HALO
    last = LTOT // _HALO - 1
    gn = GROUPS * NSTATE
    return pl.pallas_call(
        _ssm_conv_kernel,
        grid=(BATCH, NCHUNK),
        in_specs=[pl.BlockSpec((None, Q, CONV_DIM), lambda b, i: (b, i, 0)),
                  pl.BlockSpec((None, _HALO, CONV_DIM), lambda b, i: (b, jnp.maximum(i * nh - 1, 0), 0)),
                  pl.BlockSpec((None, _HALO, CONV_DIM), lambda b, i: (b, jnp.minimum(i * nh + nh, last), 0)),
                  pl.BlockSpec((SSM_K * 8, CONV_DIM), lambda b, i: (0, 0)),
                  pl.BlockSpec((1, CONV_DIM), lambda b, i: (0, 0))],
        out_specs=[pl.BlockSpec((None, Q, D_INNER), lambda b, i: (b, i, 0)),
                   pl.BlockSpec((None, gn, Q), lambda b, i: (b, 0, i)),
                   pl.BlockSpec((None, Q, gn), lambda b, i: (b, i, 0))],
        out_shape=[jax.ShapeDtypeStruct((BATCH, LTOT, D_INNER), BF16),
                   jax.ShapeDtypeStruct((BATCH, gn, LTOT), BF16),
                   jax.ShapeDtypeStruct((BATCH, LTOT, gn), BF16)],
        compiler_params=_cp(("arbitrary", "arbitrary")),
        name="ssm_conv",
    )(xbc, xbc, xbc, w, b)


def _ssd_chunk(direction, xs_ref, cm_ref, bt_ref, dt_ref, dtT_ref, arow_ref, acol_ref, rexp_ref, s_ref):
    d0 = direction * HEADS
    dtc = dt_ref[:, d0:d0 + HEADS]
    dtr = dtT_ref[d0:d0 + HEADS, :]
    da_c = dtc * arow_ref[:, d0:d0 + HEADS]
    da_r = dtr * acol_ref[d0:d0 + HEADS, :]
    ii = lax.broadcasted_iota(I32, (Q, Q), 0)
    jj = lax.broadcasted_iota(I32, (Q, Q), 1)
    if direction == 0:
        lower = jj <= ii
        tot_idx = Q - 1
    else:
        lower = jj >= ii
        tot_idx = 0
    tri_c = jnp.where(lower, 1.0, 0.0).astype(BF16)
    upper = (ii <= jj) if direction == 0 else (ii >= jj)
    tri_r = jnp.where(upper, 1.0, 0.0).astype(BF16)
    c1, c2, c3 = _split3(da_c)
    cum_c = _dot(tri_c, c1) + _dot(tri_c, c2) + _dot(tri_c, c3)
    r1, r2, r3 = _split3(da_r)
    cum_r = _dot(r1, tri_r) + _dot(r2, tri_r) + _dot(r3, tri_r)
    tot_c = cum_c[tot_idx:tot_idx + 1, :]
    tot_r = cum_r[:, tot_idx:tot_idx + 1]
    rfac = dtr * jnp.exp(tot_r - cum_r)
    dec = jnp.exp(tot_c)
    dh = dec.astype(BF16)
    dl = (dec - dh.astype(F32)).astype(BF16)
    dec_x = (_dot(jnp.broadcast_to(dh, (8, HEADS)), rexp_ref[...])
             + _dot(jnp.broadcast_to(dl, (8, HEADS)), rexp_ref[...]))[0:1, :]
    assert Q == NSTATE and 2 * HEADDIM == 128
    odd_head = lax.broadcasted_iota(I32, (Q, 128), 1) >= HEADDIM
    ys = []
    for g in range(GROUPS):
        cg = cm_ref[:, g * NSTATE:(g + 1) * NSTATE]
        btg = bt_ref[g * NSTATE:(g + 1) * NSTATE, :]
        xg = xs_ref[:, g * 256:(g + 1) * 256]
        sg = s_ref[g]
        cb = _dot(cg, btg)
        sgb = sg.astype(BF16)
        cg32 = cg.astype(F32)
        btg32 = btg.astype(F32)
        yh = []
        uh = []
        for r in range(HPG):
            h = g * HPG + r
            hs = slice((r // 2) * 128, (r // 2 + 1) * 128)
            colb = cum_c[:, h:h + 1]
            rowb = cum_r[h:h + 1, :]
            decay = jnp.exp(jnp.where(lower, colb - rowb, NEG))
            m = (cb * decay * dtr[h:h + 1, :]).astype(BF16)
            cs = (cg32 * jnp.exp(colb)).astype(BF16)
            rhs = jnp.concatenate([xg[:, hs], sgb[:, hs]], axis=0)
            res = _dot(jnp.concatenate([m, cs], axis=1), rhs)
            bw = (btg32 * rfac[h:h + 1, :]).astype(BF16)
            upd = _dot(bw, xg[:, hs])
            if r % 2 == 0:
                yh.append(res)
                uh.append(upd)
            else:
                yh[-1] = jnp.where(odd_head, res, yh[-1])
                uh[-1] = jnp.where(odd_head, upd, uh[-1])
        s_ref[g] = sg * dec_x[:, g * 256:(g + 1) * 256] + jnp.concatenate(uh, axis=1)
        ys.append(jnp.concatenate(yh, axis=1))
    return ys


def _ssd_bwd_kernel(xs_ref, cm_ref, bt_ref, dt_ref, dtT_ref, arow_ref, acol_ref, rexp_ref,
                    y_ref, s_ref):
    @pl.when(pl.program_id(1) == 0)
    def _():
        s_ref[...] = jnp.zeros_like(s_ref)
    for i in range(NB):
        ys = _ssd_chunk(1, xs_ref.at[i], cm_ref.at[i], bt_ref.at[i], dt_ref.at[i], dtT_ref.at[i],
                        arow_ref, acol_ref, rexp_ref, s_ref.at[i])
        for g in range(GROUPS):
            y_ref[i, :, g * 256:(g + 1) * 256] = ys[g].astype(BF16)


def _ssd_fwd_kernel(xs_ref, cm_ref, bt_ref, dt_ref, dtT_ref, arow_ref, acol_ref, rexp_ref,
                    z_ref, yb_ref, dsk_ref, nw_ref, y_ref, s_ref):
    @pl.when(pl.program_id(1) == 0)
    def _():
        s_ref[...] = jnp.zeros_like(s_ref)
    for i in range(NB):
        ys = _ssd_chunk(0, xs_ref.at[i], cm_ref.at[i], bt_ref.at[i], dt_ref.at[i], dtT_ref.at[i],
                        arow_ref, acol_ref, rexp_ref, s_ref.at[i])
        for g in range(GROUPS):
            gs = slice(g * 256, (g + 1) * 256)
            y = ys[g] + yb_ref[i, :, gs].astype(F32) + xs_ref[i, :, gs].astype(F32) * dsk_ref[:, gs]
            y = y * _silu(z_ref[i, :, gs].astype(F32))
            y = y * lax.rsqrt(jnp.mean(y * y, axis=-1, keepdims=True) + EPS) * nw_ref[:, gs]
            y_ref[i, :, gs] = y.astype(BF16)


NB = 2


def _ssd_specs(cmap):
    gn = GROUPS * NSTATE
    full = lambda b, j: (0, 0)
    return [pl.BlockSpec((NB, Q, D_INNER), lambda b, j: (b, cmap(j), 0)),
            pl.BlockSpec((NB, Q, gn), lambda b, j: (b, cmap(j), 0)),
            pl.BlockSpec((NB, gn, Q), lambda b, j: (b, 0, cmap(j))),
            pl.BlockSpec((NB, Q, 2 * HEADS), lambda b, j: (b, cmap(j), 0)),
            pl.BlockSpec((NB, 2 * HEADS, Q), lambda b, j: (b, 0, cmap(j))),
            pl.BlockSpec((1, 2 * HEADS), full),
            pl.BlockSpec((2 * HEADS, 1), full),
            pl.BlockSpec((HEADS, D_INNER), full)]


def _ssd_bwd(xs, cm, bt, dt, dtT, arow, acol, rexp):
    cmap = lambda j: jnp.where(j < CTX_CHUNKS, CTX_CHUNKS - 1 - j, NCHUNK + CTX_CHUNKS - 1 - j)
    omap = lambda b, j: (b, NCHUNK - 1 - jnp.maximum(j, CTX_CHUNKS), 0)
    return pl.pallas_call(
        _ssd_bwd_kernel,
        grid=(BATCH // NB, NCHUNK),
        in_specs=_ssd_specs(cmap),
        out_specs=pl.BlockSpec((NB, Q, D_INNER), omap),
        out_shape=jax.ShapeDtypeStruct((BATCH, SEQ, D_INNER), BF16),
        scratch_shapes=[pltpu.VMEM((NB, GROUPS, NSTATE, HPG * HEADDIM), F32)],
        compiler_params=_cp(("arbitrary", "arbitrary")),
        name="ssd_bwd",
    )(xs, cm, bt, dt, dtT, arow, acol, rexp)


def _ssd_fwd(xs, cm, bt, dt, dtT, arow, acol, rexp, z, yb, dsk, nw):
    cmap = lambda j: j
    lat = lambda b, j: (b, jnp.maximum(j - CTX_CHUNKS, 0), 0)
    full = lambda b, j: (0, 0)
    return pl.pallas_call(
        _ssd_fwd_kernel,
        grid=(BATCH // NB, NCHUNK),
        in_specs=_ssd_specs(cmap) + [
            pl.BlockSpec((NB, Q, D_INNER), lambda b, j: (b, j, 0)),
            pl.BlockSpec((NB, Q, D_INNER), lat),
            pl.BlockSpec((1, D_INNER), full),
            pl.BlockSpec((1, D_INNER), full)],
        out_specs=pl.BlockSpec((NB, Q, D_INNER), lat),
        out_shape=jax.ShapeDtypeStruct((BATCH, SEQ, D_INNER), BF16),
        scratch_shapes=[pltpu.VMEM((NB, GROUPS, NSTATE, HPG * HEADDIM), F32)],
        compiler_params=_cp(("arbitrary", "arbitrary")),
        name="ssd_fwd",
    )(xs, cm, bt, dt, dtT, arow, acol, rexp, z, yb, dsk, nw)


def _route(h, rwT_ref, rb_ref, cnt_ref, eidx_ref, pos_ref, gate_ref):
    hh = h.astype(BF16)
    hl = (h - hh.astype(F32)).astype(BF16)
    w = rwT_ref[...]
    wh = w.astype(BF16)
    wl = (w - wh.astype(F32)).astype(BF16)
    logits = _dot_nt(wh, hh) + _dot_nt(wh, hl) + _dot_nt(wl, hh)
    scores = jax.nn.sigmoid(logits)
    sel = scores + rb_ref[...]
    per = E // NGRP
    sub = lax.broadcasted_iota(I32, (per, TM), 0)
    gscore = []
    for g in range(NGRP):
        blk = sel[g * per:(g + 1) * per, :]
        m1 = jnp.max(blk, axis=0, keepdims=True)
        first = jnp.min(jnp.where(blk == m1, sub, per), axis=0, keepdims=True)
        m2 = jnp.max(jnp.where(sub == first, -jnp.inf, blk), axis=0, keepdims=True)
        gscore.append(m1 + m2)
    masked = []
    for g in range(NGRP):
        rank = jnp.zeros((1, TM), F32)
        for o in range(NGRP):
            if o == g:
                continue
            ahead = (gscore[o] >= gscore[g]) if o < g else (gscore[o] > gscore[g])
            rank = rank + jnp.where(ahead, 1.0, 0.0)
        blk = sel[g * per:(g + 1) * per, :]
        masked.append(jnp.where(rank < TOPG, blk, -jnp.inf))
    v = jnp.concatenate(masked, axis=0)
    eio = lax.broadcasted_iota(I32, (E, TM), 0)
    kio = lax.broadcasted_iota(I32, (TOPK, TM), 0)
    eidx = jnp.zeros((TOPK, TM), I32)
    gsc = jnp.zeros((TOPK, TM), F32)
    hot = jnp.zeros((E, TM), F32)
    picks = []
    for k in range(TOPK):
        m = jnp.max(v, axis=0, keepdims=True)
        first = jnp.min(jnp.where(v == m, eio, E), axis=0, keepdims=True)
        pick = eio == first
        sc = jnp.sum(jnp.where(pick, scores, 0.0), axis=0, keepdims=True)
        eidx = jnp.where(kio == k, first, eidx)
        gsc = jnp.where(kio == k, sc, gsc)
        hot = jnp.where(pick, 1.0, hot)
        v = jnp.where(pick, -jnp.inf, v)
        picks.append(pick)
    gate = gsc / jnp.sum(gsc, axis=0, keepdims=True) * ROUTED_SCALE
    ti = lax.broadcasted_iota(I32, (TM, TM), 0)
    tj = lax.broadcasted_iota(I32, (TM, TM), 1)
    before = jnp.where(ti < tj, 1.0, 0.0).astype(BF16)
    posfull = _dot(hot.astype(BF16), before) + cnt_ref[...]
    pos = jnp.zeros((TOPK, TM), F32)
    for k in range(TOPK):
        pk = jnp.sum(jnp.where(picks[k], posfull, 0.0), axis=0, keepdims=True)
        pos = jnp.where(kio == k, pk, pos)
    cnt_ref[...] = cnt_ref[...] + jnp.sum(hot, axis=1, keepdims=True)
    eidx_ref[...] = eidx
    pos_ref[...] = pos.astype(I32)
    eye = jnp.where(ti == tj, 1.0, 0.0).astype(BF16)
    g1, g2, g3 = _split3(gate)
    gate_ref[...] = _dot_nt(eye, g1) + _dot_nt(eye, g2) + _dot_nt(eye, g3)


def _mix_epilogue(y, x_ref, mod_ref, gpost_ref, gpre_ref, rwT_ref, rb_ref,
                  x1_ref, h2_ref, eidx_ref, pos_ref, gate_ref, cnt_out_ref, cnt_ref):
    first = (pl.program_id(0) == 0) & (pl.program_id(1) == 0)

    @pl.when(first)
    def _():
        cnt_ref[...] = jnp.zeros_like(cnt_ref)
    x1 = x_ref[...] + mod_ref[2:3, :] * _rms(y, gpost_ref[...])
    x1_ref[...] = x1
    h2 = _rms(x1, gpre_ref[...]) * (1.0 + mod_ref[4:5, :]) + mod_ref[3:4, :]
    _store_packed(h2_ref, h2)
    _route(h2, rwT_ref, rb_ref, cnt_ref, eidx_ref, pos_ref, gate_ref)
    cnt_out_ref[...] = cnt_ref[...]


def _ssm_out_kernel(y_ref, w_ref, x_ref, mod_ref, gpost_ref, gpre_ref, rwT_ref, rb_ref,
                    x1_ref, h2_ref, eidx_ref, pos_ref, gate_ref, cnt_out_ref, cnt_ref):
    y = _dot(y_ref[...], w_ref[...])
    _mix_epilogue(y, x_ref, mod_ref, gpost_ref, gpre_ref, rwT_ref, rb_ref,
                  x1_ref, h2_ref, eidx_ref, pos_ref, gate_ref, cnt_out_ref, cnt_ref)


_CW = 256
_PAD = 16


def _conf_out_kernel(u_ref, dww_ref, dwb_ref, lng_ref, lnb_ref, w_ref, b_ref,
                     x_ref, mod_ref, gpost_ref, gpre_ref, rwT_ref, rb_ref,
                     x1_ref, h2_ref, eidx_ref, pos_ref, gate_ref, cnt_out_ref,
                     cnt_ref, v_ref):
    nrow = TM // GRID_W
    ext_rows = GRID_W + 2 * _PAD
    first = _PAD - CONF_K // 2
    span = GRID_W + 8 * ((CONF_K - 1) // 8)
    assert first + 7 + span <= ext_rows
    ri = lax.broadcasted_iota(I32, (span, ext_rows), 0)
    ji = lax.broadcasted_iota(I32, (span, ext_rows), 1)
    shift = [jnp.where(ji == ri + first + b, 1.0, 0.0).astype(BF16) for b in range(8)]
    zpad = jnp.zeros((_PAD, D), BF16)
    for r in range(nrow):
        ext = jnp.concatenate([zpad, u_ref[r * GRID_W:(r + 1) * GRID_W, :], zpad], axis=0)
        for c in range(D // _CW):
            cs = slice(c * _CW, (c + 1) * _CW)
            acc = jnp.broadcast_to(dwb_ref[:, cs], (GRID_W, _CW))
            for b in range(8):
                win = _dot(shift[b], ext[:, cs])
                for a in range(-(-CONF_K // 8)):
                    k = 8 * a + b
                    if k < CONF_K:
                        wk = jnp.concatenate([dww_ref[8 * k:8 * k + 8, cs]] * (GRID_W // 8), axis=0)
                        acc = acc + wk * win[8 * a:8 * a + GRID_W, :]
            v_ref[r * GRID_W:(r + 1) * GRID_W, cs] = acc
    v = v_ref[...]
    mu = jnp.mean(v, axis=-1, keepdims=True)
    vc = v - mu
    ln = vc * lax.rsqrt(jnp.mean(vc * vc, axis=-1, keepdims=True) + EPS) * lng_ref[...] + lnb_ref[...]
    y = _dot(_silu(ln).astype(BF16), w_ref[...]) + b_ref[...]
    _mix_epilogue(y, x_ref, mod_ref, gpost_ref, gpre_ref, rwT_ref, rb_ref,
                  x1_ref, h2_ref, eidx_ref, pos_ref, gate_ref, cnt_out_ref, cnt_ref)


def _mix_out_common_specs():
    full = lambda b, i: (0, 0)
    nt = SEQ // TM
    in_specs = [pl.BlockSpec((None, TM, D), lambda b, i: (b, i, 0)),
                pl.BlockSpec((None, 6, D), lambda b, i: (b, 0, 0)),
                pl.BlockSpec((1, D), full), pl.BlockSpec((1, D), full),
                pl.BlockSpec((E, D), full), pl.BlockSpec((E, 1), full)]
    out_specs = [pl.BlockSpec((None, TM, D), lambda b, i: (b, i, 0)),
                 pl.BlockSpec((TM * RW, 128), lambda b, i: (b * nt + i, 0)),
                 pl.BlockSpec((TOPK, TM), lambda b, i: (0, b * nt + i)),
                 pl.BlockSpec((TOPK, TM), lambda b, i: (0, b * nt + i)),
                 pl.BlockSpec((TM, TOPK), lambda b, i: (b * nt + i, 0)),
                 pl.BlockSpec((E, 1), full)]
    out_shape = [jax.ShapeDtypeStruct((BATCH, SEQ, D), F32),
                 jax.ShapeDtypeStruct((T * RW, 128), U32),
                 jax.ShapeDtypeStruct((TOPK, T), I32),
                 jax.ShapeDtypeStruct((TOPK, T), I32),
                 jax.ShapeDtypeStruct((T, TOPK), F32),
                 jax.ShapeDtypeStruct((E, 1), F32)]
    return in_specs, out_specs, out_shape


def _ssm_out(y, w, x, mod, gpost, gpre, rwT, rb):
    common_in, out_specs, out_shape = _mix_out_common_specs()
    return pl.pallas_call(
        _ssm_out_kernel,
        grid=(BATCH, SEQ // TM),
        in_specs=[pl.BlockSpec((None, TM, D_INNER), lambda b, i: (b, i, 0)),
                  pl.BlockSpec((D_INNER, D), lambda b, i: (0, 0))] + common_in,
        out_specs=out_specs, out_shape=out_shape,
        scratch_shapes=[pltpu.VMEM((E, 1), F32)],
        compiler_params=_cp(("arbitrary", "arbitrary")),
        name="ssm_out",
    )(y, w, x, mod, gpost, gpre, rwT, rb)


def _conf_out(u, dww, dwb, lng, lnb, w, b, x, mod, gpost, gpre, rwT, rb):
    common_in, out_specs, out_shape = _mix_out_common_specs()
    full = lambda b_, i: (0, 0)
    return pl.pallas_call(
        _conf_out_kernel,
        grid=(BATCH, SEQ // TM),
        in_specs=[pl.BlockSpec((None, TM, D), lambda b_, i: (b_, i, 0)),
                  pl.BlockSpec((CONF_K * 8, D), full), pl.BlockSpec((1, D), full),
                  pl.BlockSpec((1, D), full), pl.BlockSpec((1, D), full),
                  pl.BlockSpec((D, D), full), pl.BlockSpec((1, D), full)] + common_in,
        out_specs=out_specs, out_shape=out_shape,
        scratch_shapes=[pltpu.VMEM((E, 1), F32), pltpu.VMEM((TM, D), F32)],
        compiler_params=_cp(("arbitrary", "arbitrary")),
        name="conf_out",
    )(u, dww, dwb, lng, lnb, w, b, x, mod, gpost, gpre, rwT, rb)


def _conf_in_kernel(x_ref, mod_ref, g_ref, wa_ref, wg_ref, ba_ref, bg_ref, u_ref):
    h = _rms(x_ref[...], g_ref[...]) * (1.0 + mod_ref[1:2, :]) + mod_ref[0:1, :]
    hb = h.astype(BF16)
    a = _dot(hb, wa_ref[...]) + ba_ref[...]
    g = _dot(hb, wg_ref[...]) + bg_ref[...]
    u_ref[...] = (a * jax.nn.sigmoid(g)).astype(BF16)


def _conf_in(x, mod, gain, w, bias):
    full = lambda b, i: (0, 0)
    return pl.pallas_call(
        _conf_in_kernel,
        grid=(BATCH, SEQ // TM),
        in_specs=[pl.BlockSpec((None, TM, D), lambda b, i: (b, i, 0)),
                  pl.BlockSpec((None, 6, D), lambda b, i: (b, 0, 0)),
                  pl.BlockSpec((1, D), full),
                  pl.BlockSpec((D, D), full), pl.BlockSpec((D, D), lambda b, i: (0, 1)),
                  pl.BlockSpec((1, D), full), pl.BlockSpec((1, D), lambda b, i: (0, 1))],
        out_specs=pl.BlockSpec((None, TM, D), lambda b, i: (b, i, 0)),
        out_shape=jax.ShapeDtypeStruct((BATCH, SEQ, D), BF16),
        compiler_params=_cp(("arbitrary", "arbitrary")),
        name="conf_in",
    )(x, mod, gain, w, w, bias, bias)


def _row_copy(src, si, dst, di, sem):
    return pltpu.make_async_copy(src.at[pl.ds(pl.multiple_of(si * RW, RW), RW), :],
                                 dst.at[pl.ds(pl.multiple_of(di * RW, RW), RW), :], sem)


ZROWS = BLK + 8


def _dispatch_kernel(zs_ref, dest_ref, h_ref, xs_ref, zero_ref, zsem, sem):
    @pl.when(pl.program_id(0) == 0)
    def _():
        zero_ref[...] = jnp.zeros_like(zero_ref)

        def zcopy(e):
            start = pl.multiple_of(zs_ref[e] * RW, 8 * RW)
            return pltpu.make_async_copy(zero_ref, xs_ref.at[pl.ds(start, ZROWS * RW), :], zsem)

        def zstart(e, c):
            zcopy(e).start()
            return c

        def zwait(e, c):
            zcopy(e).wait()
            return c
        lax.fori_loop(0, E, zstart, 0)
        lax.fori_loop(0, E, zwait, 0)

    def issue(t, c):
        for k in range(TOPK):
            _row_copy(h_ref, t, xs_ref, dest_ref[t * TOPK + k], sem).start(priority=k % 2)
        return c

    lax.fori_loop(0, TM, issue, 0)
    for k in range(TOPK):
        pltpu.make_async_copy(h_ref, xs_ref.at[pl.ds(0, TM * RW), :], sem).wait()


def _dispatch(zstart, dest, hp):
    grid_spec = pltpu.PrefetchScalarGridSpec(
        num_scalar_prefetch=1,
        grid=(T // TM,),
        in_specs=[pl.BlockSpec((TM * TOPK,), lambda i, zs: (i,), memory_space=pltpu.SMEM),
                  pl.BlockSpec((TM * RW, 128), lambda i, zs: (i, 0))],
        out_specs=pl.BlockSpec(memory_space=pl.ANY),
        scratch_shapes=[pltpu.VMEM((ZROWS * RW, 128), U32), pltpu.SemaphoreType.DMA(()),
                        pltpu.SemaphoreType.DMA(())])
    return pl.pallas_call(
        _dispatch_kernel,
        grid_spec=grid_spec,
        out_shape=jax.ShapeDtypeStruct((PROWS * RW, 128), U32),
        compiler_params=_cp(("arbitrary",)),
        name="moe_dispatch",
    )(zstart, dest, hp)


XRING = 3


def _experts_kernel(be_ref, na_ref, x_hbm, wg_ref, wu_ref, wd_ref, y_ref, wgb_ref, wub_ref, wdb_ref,
                    xbuf_ref, xsem):
    i = pl.program_id(0)
    na = na_ref[0]

    def fetch(j):
        rows = BLK * RW
        src = x_hbm.at[pl.ds(pl.multiple_of(j * rows, rows), rows), :]
        return pltpu.make_async_copy(src, xbuf_ref.at[j % XRING], xsem.at[j % XRING])

    @pl.when(i == 0)
    def _():
        for j in range(XRING - 1):
            @pl.when(j < na)
            def _():
                fetch(j).start()

    @pl.when(i + XRING - 1 < na)
    def _():
        fetch(i + XRING - 1).start()

    @pl.when((i == 0) | (be_ref[i] != be_ref[jnp.maximum(i - 1, 0)]))
    def _():
        wgb_ref[...] = wg_ref[...].astype(BF16)
        wub_ref[...] = wu_ref[...].astype(BF16)
        wdb_ref[...] = wd_ref[...].astype(BF16)

    @pl.when(i < na)
    def _():
        fetch(i).wait()
        xb = _load_packed(xbuf_ref.at[i % XRING], BLK)
        g = _dot(xb, wgb_ref[...])
        u = _dot(xb, wub_ref[...])
        a = (_silu(g) * u).astype(BF16)
        _store_packed(y_ref, _dot(a, wdb_ref[...]))

    @pl.when(i >= na)
    def _():
        y_ref[...] = jnp.zeros_like(y_ref)


def _experts(layer, block_e, nact, xs, wg, wu, wd):
    wmap = lambda i, be, na: (layer, be[i], 0, 0)
    grid_spec = pltpu.PrefetchScalarGridSpec(
        num_scalar_prefetch=2,
        grid=(NBLK,),
        in_specs=[pl.BlockSpec(memory_space=pl.ANY),
                  pl.BlockSpec((None, None, D, DE), wmap),
                  pl.BlockSpec((None, None, D, DE), wmap),
                  pl.BlockSpec((None, None, DE, D), wmap)],
        out_specs=pl.BlockSpec((BLK * RW, 128), lambda i, be, na: (i, 0)),
        scratch_shapes=[pltpu.VMEM((D, DE), BF16), pltpu.VMEM((D, DE), BF16),
                        pltpu.VMEM((DE, D), BF16),
                        pltpu.VMEM((XRING, BLK * RW, 128), U32), pltpu.SemaphoreType.DMA((XRING,))])
    return pl.pallas_call(
        _experts_kernel,
        grid_spec=grid_spec,
        out_shape=jax.ShapeDtypeStruct((NBLK * BLK * RW, 128), U32),
        compiler_params=_cp(("arbitrary",)),
        name="moe_experts",
    )(block_e, nact, xs, wg, wu, wd)


def _combine_kernel(dcur_ref, dnxt_ref, gate_ref, h_ref, x1_ref, mod_ref, gpost_ref,
                    sg_ref, su_ref, sd_ref, ys_ref, o_ref, buf_ref, sems):
    i = pl.program_id(0)
    n = pl.num_programs(0)

    def gather(dest_ref, slot):
        def body(t, c):
            for k in range(TOPK):
                _row_copy(ys_ref, dest_ref[t * TOPK + k], buf_ref.at[slot, k], t,
                          sems.at[slot]).start(priority=k % 2)
            return c
        lax.fori_loop(0, TM, body, 0)

    @pl.when(i == 0)
    def _():
        gather(dcur_ref, 0)

    @pl.when(i + 1 < n)
    def _():
        gather(dnxt_ref, (i + 1) % 2)

    hb = _load_packed(h_ref, TM)
    a = (_silu(_dot(hb, sg_ref[...])) * _dot(hb, su_ref[...])).astype(BF16)
    f = _dot(a, sd_ref[...])

    slot = i % 2

    for k in range(TOPK):
        pltpu.make_async_copy(ys_ref.at[pl.ds(0, TM * RW), :], buf_ref.at[slot, k],
                              sems.at[slot]).wait()

    gate = gate_ref[...]
    gb = [jnp.broadcast_to(gate[:, k:k + 1], (TM, 128)) for k in range(TOPK)]
    los, his = [], []
    for s in range(RW):
        lo = f[:, s * 128:(s + 1) * 128]
        hi = f[:, D // 2 + s * 128:D // 2 + (s + 1) * 128]
        for k in range(TOPK):
            wl, wh = _unpack2(buf_ref[slot, k, pl.ds(s, TM, stride=RW), :])
            lo = lo + gb[k] * wl
            hi = hi + gb[k] * wh
        los.append(lo)
        his.append(hi)
    f = jnp.concatenate(los + his, axis=1)
    o_ref[...] = x1_ref[...] + mod_ref[5:6, :] * _rms(f, gpost_ref[...])


def _combine(dest, gate, hp, x1, mod, gpost, sg, su, sd, ys):
    nt = SEQ // TM
    n = T // TM
    full = lambda i: (0, 0)
    return pl.pallas_call(
        _combine_kernel,
        grid=(n,),
        in_specs=[pl.BlockSpec((TM * TOPK,), lambda i: (i,), memory_space=pltpu.SMEM),
                  pl.BlockSpec((TM * TOPK,), lambda i: (jnp.minimum(i + 1, n - 1),),
                               memory_space=pltpu.SMEM),
                  pl.BlockSpec((TM, TOPK), lambda i: (i, 0)),
                  pl.BlockSpec((TM * RW, 128), lambda i: (i, 0)),
                  pl.BlockSpec((TM, D), lambda i: (i, 0)),
                  pl.BlockSpec((None, 6, D), lambda i: (i // nt, 0, 0)),
                  pl.BlockSpec((1, D), full),
                  pl.BlockSpec((D, DE), full), pl.BlockSpec((D, DE), full), pl.BlockSpec((DE, D), full),
                  pl.BlockSpec(memory_space=pl.ANY)],
        out_specs=pl.BlockSpec((TM, D), lambda i: (i, 0)),
        out_shape=jax.ShapeDtypeStruct((T, D), F32),
        scratch_shapes=[pltpu.VMEM((2, TOPK, TM * RW, 128), U32), pltpu.SemaphoreType.DMA((2,))],
        compiler_params=_cp(("arbitrary",)),
        name="moe_combine",
    )(dest, dest, gate, hp, x1.reshape(T, D), mod, gpost, sg, su, sd, ys).reshape(BATCH, SEQ, D)


def _moe(layer, x1, hp, eidx, pos, gate, counts, mod, gpost, wg, wu, wd, sg, su, sd):
    cnt = counts.reshape(E).astype(I32)
    padded = (cnt + BLK - 1) // BLK * BLK
    pend = jnp.cumsum(padded)
    pstart = pend - padded
    ids = jnp.arange(E, dtype=I32)
    dest = pos + jnp.sum(jnp.where(eidx[:, :, None] == ids, pstart, 0), axis=-1)
    dest = dest.T.reshape(T * TOPK)
    nact = (pend[-1] // BLK).reshape(1).astype(I32)
    blk0 = jnp.arange(NBLK, dtype=I32) * BLK
    block_e = jnp.sum((pend[None, :] <= blk0[:, None]).astype(I32), axis=1)
    block_e = jnp.minimum(block_e, E - 1).astype(I32)
    xs = _dispatch(((pstart + cnt) // 8 * 8).astype(I32), dest, hp)
    ys = _experts(layer, block_e, nact, xs, wg, wu, wd)
    return _combine(dest, gate, hp, x1, mod, gpost, sg.astype(BF16), su.astype(BF16),
                    sd.astype(BF16), ys)


def kernel(x, c, ctx, c_ctx, ada_w, ada_b, norm_mix_pre, norm_mix_post, norm_ffn_pre, norm_ffn_post, ssm_w_in, ssm_conv_w, ssm_conv_b, ssm_dt_bias, ssm_a_log, ssm_d, ssm_norm, ssm_w_out, cv_w_in, cv_b_in, cv_dw_w, cv_dw_b, cv_ln_g, cv_ln_b, cv_w_out, cv_b_out, router_w, router_b, exp_w_gate, exp_w_up, exp_w_down, sh_w_gate, sh_w_up, sh_w_down):
    row = lambda v: v.reshape(1, -1)
    cvec = jnp.concatenate([c, c_ctx[None, :], jnp.zeros((3, D), F32)], axis=0)
    mod = _ada(cvec, ada_w, ada_b).reshape(2, 8, 6, D)

    w_in = ssm_w_in[0].astype(BF16)
    dtb = ssm_dt_bias[0].reshape(1, 2 * HEADS)
    wdt = w_in[:, D_INNER + CONV_DIM:]
    z, xbc, dt, dtT = _ssm_in(ctx, x, mod[0], row(norm_mix_pre[0]), w_in, wdt, wdt.T,
                              dtb, dtb.reshape(2 * HEADS, 1))
    xs, bt, cm = _ssm_conv(xbc, jnp.repeat(ssm_conv_w[0], 8, axis=0), row(ssm_conv_b[0]))
    a = -jnp.exp(ssm_a_log[0].astype(F32)).reshape(1, 2 * HEADS)
    rexp = (jnp.arange(D_INNER)[None, :] // HEADDIM == jnp.arange(HEADS)[:, None]).astype(BF16)
    yb = _ssd_bwd(xs, cm, bt, dt, dtT, a, a.reshape(2 * HEADS, 1), rexp)
    dsk = jnp.repeat(ssm_d[0], HEADDIM).reshape(1, D_INNER)
    ygn = _ssd_fwd(xs, cm, bt, dt, dtT, a, a.reshape(2 * HEADS, 1), rexp, z, yb, dsk,
                   row(ssm_norm[0]))
    x1, h2, eidx, pos, gate, counts = _ssm_out(
        ygn, ssm_w_out[0].astype(BF16), x, mod[0], row(norm_mix_post[0]), row(norm_ffn_pre[0]),
        router_w[0].T, router_b[0].reshape(E, 1))
    x2 = _moe(0, x1, h2, eidx, pos, gate, counts, mod[0], row(norm_ffn_post[0]),
              exp_w_gate, exp_w_up, exp_w_down, sh_w_gate[0], sh_w_up[0], sh_w_down[0])

    u = _conf_in(x2, mod[1], row(norm_mix_pre[1]), cv_w_in[0].astype(BF16), row(cv_b_in[0]))
    x3, h4, eidx, pos, gate, counts = _conf_out(
        u, jnp.repeat(cv_dw_w[0], 8, axis=0), row(cv_dw_b[0]), row(cv_ln_g[0]), row(cv_ln_b[0]),
        cv_w_out[0].astype(BF16), row(cv_b_out[0]),
        x2, mod[1], row(norm_mix_post[1]), row(norm_ffn_pre[1]),
        router_w[1].T, router_b[1].reshape(E, 1))
    return _moe(1, x3, h4, eidx, pos, gate, counts, mod[1], row(norm_ffn_post[1]),
                exp_w_gate, exp_w_up, exp_w_down, sh_w_gate[1], sh_w_up[1], sh_w_down[1])
```

```python
import functools

import jax
import jax.numpy as jnp
from jax import lax
from jax.experimental import pallas as pl
from jax.experimental.pallas import tpu as pltpu

F32 = jnp.float32
BF16 = jnp.bfloat16
I32 = jnp.int32

D = 1024
BATCH = 4
SEQ = 4096
CTX = 256
LTOT = CTX + SEQ
GRID_W = 64

D_INNER = 2048
HEADS = 32
GROUPS = 8
HPG = 4
HEADDIM = 64
NSTATE = 128
Q = 128
NCHUNK = LTOT // Q
CTX_CHUNKS = CTX // Q
CONV_DIM = D_INNER + 2 * GROUPS * NSTATE
SSM_K = 5
CONF_K = 31

E = 64
TOPK = 8
NGRP = 8
TOPG = 4
DE = 256
ROUTED_SCALE = 2.5
EPS = 1e-6

T = BATCH * SEQ
TM = 512
BLK = 512
NBLK = -(-(T * TOPK + E * (BLK - 1)) // BLK)
PROWS = (NBLK + 1) * BLK + 8

VMEM_LIMIT = 56 * 1024 * 1024
NEG = -1e30


def _cp(sem):
    return pltpu.CompilerParams(dimension_semantics=sem, vmem_limit_bytes=VMEM_LIMIT)


def _silu(v):
    return v * jax.nn.sigmoid(v)


def _rms(v, g):
    return v * lax.rsqrt(jnp.mean(v * v, axis=-1, keepdims=True) + EPS) * g


def _split3(v):
    a = v.astype(BF16)
    r = v - a.astype(F32)
    b = r.astype(BF16)
    c = (r - b.astype(F32)).astype(BF16)
    return a, b, c


def _dot(a, b):
    return jnp.dot(a, b, preferred_element_type=F32)


def _dot_nt(a, b):
    return lax.dot_general(a, b, (((1,), (1,)), ((), ())), preferred_element_type=F32)


U32 = jnp.uint32
RW = D // 2 // 128
_HI = 0xFFFF0000


def _pack2(lo, hi):
    ul = pltpu.bitcast(lo.astype(BF16).astype(F32), U32)
    uh = pltpu.bitcast(hi.astype(BF16).astype(F32), U32)
    return (ul >> 16) | (uh & U32(_HI))


def _unpack2(w):
    return pltpu.bitcast(w << 16, F32), pltpu.bitcast(w & U32(_HI), F32)


def _store_packed(ref, v):
    m = v.shape[0]
    for s in range(RW):
        lo = v[:, s * 128:(s + 1) * 128]
        hi = v[:, D // 2 + s * 128:D // 2 + (s + 1) * 128]
        ref[pl.ds(s, m, stride=RW), :] = _pack2(lo, hi)


def _load_packed(ref, m):
    los, his = [], []
    for s in range(RW):
        lo, hi = _unpack2(ref[pl.ds(s, m, stride=RW), :])
        los.append(lo.astype(BF16))
        his.append(hi.astype(BF16))
    return jnp.concatenate(los + his, axis=1)


def _ada_kernel(c_ref, w_ref, b_ref, o_ref):
    s = _silu(c_ref[...])
    o_ref[...] = jnp.dot(s, w_ref[...], preferred_element_type=F32,
                         precision=lax.Precision.HIGHEST) + b_ref[...]


def _ada(cvec, ada_w, ada_b):
    depth = ada_w.shape[0]
    tn = 1536
    return pl.pallas_call(
        _ada_kernel,
        grid=(depth, 6 * D // tn),
        in_specs=[pl.BlockSpec((8, D), lambda l, j: (0, 0)),
                  pl.BlockSpec((None, D, tn), lambda l, j: (l, 0, j)),
                  pl.BlockSpec((None, 1, tn), lambda l, j: (l, 0, j))],
        out_specs=pl.BlockSpec((None, 8, tn), lambda l, j: (l, 0, j)),
        out_shape=jax.ShapeDtypeStruct((depth, 8, 6 * D), F32),
        compiler_params=_cp(("arbitrary", "arbitrary")),
        name="ada",
    )(cvec, ada_w, ada_b.reshape(depth, 1, 6 * D))


def _ssm_in_kernel(c_ref, x_ref, mod_ref, g_ref, wz_ref, wx0_ref, wx1_ref, wdt_ref, wdtT_ref, dtb_ref,
                   dtbT_ref, z_ref, xbc_ref, dt_ref, dtT_ref):
    xin = jnp.where(pl.program_id(1) == 0, c_ref[...], x_ref[...])
    h = _rms(xin, g_ref[...]) * (1.0 + mod_ref[1:2, :]) + mod_ref[0:1, :]
    hb = h.astype(BF16)
    z_ref[...] = _dot(hb, wz_ref[...]).astype(BF16)
    xbc_ref[:, :D_INNER] = _dot(hb, wx0_ref[...]).astype(BF16)
    xbc_ref[:, D_INNER:] = _dot(hb, wx1_ref[...]).astype(BF16)
    dt_ref[...] = jax.nn.softplus(_dot(hb, wdt_ref[...]) + dtb_ref[...])
    dtT_ref[...] = jax.nn.softplus(_dot_nt(wdtT_ref[...], hb) + dtbT_ref[...])


def _ssm_in(ctx, x, mod, gain, w_in, wdt, wdtT, dtb, dtbT):
    ts = CTX
    nt = LTOT // ts
    full = lambda b, i: (0, 0)
    assert CONV_DIM == 2 * D_INNER and SEQ % ts == 0
    return pl.pallas_call(
        _ssm_in_kernel,
        grid=(BATCH, nt),
        in_specs=[pl.BlockSpec((None, ts, D), lambda b, i: (b, 0, 0)),
                  pl.BlockSpec((None, ts, D), lambda b, i: (b, jnp.maximum(i - 1, 0), 0)),
                  pl.BlockSpec((None, 6, D), lambda b, i: (jnp.where(i == 0, BATCH, b), 0, 0)),
                  pl.BlockSpec((1, D), full),
                  pl.BlockSpec((D, D_INNER), lambda b, i: (0, 0)),
                  pl.BlockSpec((D, D_INNER), lambda b, i: (0, 1)),
                  pl.BlockSpec((D, D_INNER), lambda b, i: (0, 2)),
                  pl.BlockSpec((D, 2 * HEADS), full),
                  pl.BlockSpec((2 * HEADS, D), full),
                  pl.BlockSpec((1, 2 * HEADS), full),
                  pl.BlockSpec((2 * HEADS, 1), full)],
        out_specs=[pl.BlockSpec((None, ts, D_INNER), lambda b, i: (b, i, 0)),
                   pl.BlockSpec((None, ts, CONV_DIM), lambda b, i: (b, i, 0)),
                   pl.BlockSpec((None, ts, 2 * HEADS), lambda b, i: (b, i, 0)),
                   pl.BlockSpec((None, 2 * HEADS, ts), lambda b, i: (b, 0, i))],
        out_shape=[jax.ShapeDtypeStruct((BATCH, LTOT, D_INNER), BF16),
                   jax.ShapeDtypeStruct((BATCH, LTOT, CONV_DIM), BF16),
                   jax.ShapeDtypeStruct((BATCH, LTOT, 2 * HEADS), F32),
                   jax.ShapeDtypeStruct((BATCH, 2 * HEADS, LTOT), F32)],
        compiler_params=_cp(("arbitrary", "arbitrary")),
        name="ssm_in",
    )(ctx, x, mod, gain, w_in, w_in, w_in, wdt, wdtT, dtb, dtbT)


_HALO = 16
_CC = 512


def _ssm_conv_kernel(x_ref, xp_ref, xn_ref, w_ref, b_ref, xs_ref, bt_ref, cm_ref):
    i = pl.program_id(1)
    first_lat = CTX // Q
    pvalid = (i != 0) & (i != first_lat)
    nvalid = (i != first_lat - 1) & (i != NCHUNK - 1)
    li = lax.broadcasted_iota(I32, (Q, Q + 2 * _HALO), 0)
    ji = lax.broadcasted_iota(I32, (Q, Q + 2 * _HALO), 1)
    shift = {k: jnp.where(ji == li + _HALO + k - SSM_K // 2, 1.0, 0.0).astype(BF16)
             for k in range(SSM_K) if k != SSM_K // 2}
    for c in range(CONV_DIM // _CC):
        cs = slice(c * _CC, (c + 1) * _CC)
        cur = x_ref[:, cs]
        prev = jnp.where(pvalid, xp_ref[:, cs], jnp.zeros((_HALO, _CC), BF16))
        nxt = jnp.where(nvalid, xn_ref[:, cs], jnp.zeros((_HALO, _CC), BF16))
        ext = jnp.concatenate([prev, cur, nxt], axis=0)
        acc = jnp.broadcast_to(b_ref[:, cs], (Q, _CC))
        for k in range(SSM_K):
            wk = jnp.concatenate([w_ref[8 * k:8 * k + 8, cs]] * (Q // 8), axis=0)
            tap = cur.astype(F32) if k == SSM_K // 2 else _dot(shift[k], ext)
            acc = acc + wk * tap
        y = _silu(acc)
        lo = c * _CC
        if lo < D_INNER:
            xs_ref[:, cs] = y.astype(BF16)
        elif lo < D_INNER + GROUPS * NSTATE:
            o = lo - D_INNER
            bt_ref[o:o + _CC, :] = y.T.astype(BF16)
        else:
            o = lo - D_INNER - GROUPS * NSTATE
            cm_ref[:, o:o + _CC] = y.astype(BF16)


def _ssm_conv(xbc, w, b):
    nh = Q // _HALO
    last = LTOT // _HALO - 1
    gn = GROUPS * NSTATE
    return pl.pallas_call(
        _ssm_conv_kernel,
        grid=(BATCH, NCHUNK),
        in_specs=[pl.BlockSpec((None, Q, CONV_DIM), lambda b, i: (b, i, 0)),
                  pl.BlockSpec((None, _HALO, CONV_DIM), lambda b, i: (b, jnp.maximum(i * nh - 1, 0), 0)),
                  pl.BlockSpec((None, _HALO, CONV_DIM), lambda b, i: (b, jnp.minimum(i * nh + nh, last), 0)),
                  pl.BlockSpec((SSM_K * 8, CONV_DIM), lambda b, i: (0, 0)),
                  pl.BlockSpec((1, CONV_DIM), lambda b, i: (0, 0))],
        out_specs=[pl.BlockSpec((None, Q, D_INNER), lambda b, i: (b, i, 0)),
                   pl.BlockSpec((None, gn, Q), lambda b, i: (b, 0, i)),
                   pl.BlockSpec((None, Q, gn), lambda b, i: (b, i, 0))],
        out_shape=[jax.ShapeDtypeStruct((BATCH, LTOT, D_INNER), BF16),
                   jax.ShapeDtypeStruct((BATCH, gn, LTOT), BF16),
                   jax.ShapeDtypeStruct((BATCH, LTOT, gn), BF16)],
        compiler_params=_cp(("arbitrary", "arbitrary")),
        name="ssm_conv",
    )(xbc, xbc, xbc, w, b)


def _ssd_chunk(direction, xs_ref, cm_ref, bt_ref, dt_ref, dtT_ref, arow_ref, acol_ref, rexp_ref, s_ref):
    d0 = direction * HEADS
    dtc = dt_ref[:, d0:d0 + HEADS]
    dtr = dtT_ref[d0:d0 + HEADS, :]
    da_c = dtc * arow_ref[:, d0:d0 + HEADS]
    da_r = dtr * acol_ref[d0:d0 + HEADS, :]
    ii = lax.broadcasted_iota(I32, (Q, Q), 0)
    jj = lax.broadcasted_iota(I32, (Q, Q), 1)
    if direction == 0:
        lower = jj <= ii
        tot_idx = Q - 1
    else:
        lower = jj >= ii
        tot_idx = 0
    tri_c = jnp.where(lower, 1.0, 0.0).astype(BF16)
    upper = (ii <= jj) if direction == 0 else (ii >= jj)
    tri_r = jnp.where(upper, 1.0, 0.0).astype(BF16)
    c1, c2, c3 = _split3(da_c)
    cum_c = _dot(tri_c, c1) + _dot(tri_c, c2) + _dot(tri_c, c3)
    r1, r2, r3 = _split3(da_r)
    cum_r = _dot(r1, tri_r) + _dot(r2, tri_r) + _dot(r3, tri_r)
    tot_c = cum_c[tot_idx:tot_idx + 1, :]
    tot_r = cum_r[:, tot_idx:tot_idx + 1]
    rfac = dtr * jnp.exp(tot_r - cum_r)
    dec = jnp.exp(tot_c)
    dh = dec.astype(BF16)
    dl = (dec - dh.astype(F32)).astype(BF16)
    dec_x = (_dot(jnp.broadcast_to(dh, (8, HEADS)), rexp_ref[...])
             + _dot(jnp.broadcast_to(dl, (8, HEADS)), rexp_ref[...]))[0:1, :]
    assert Q == NSTATE and 2 * HEADDIM == 128
    odd_head = lax.broadcasted_iota(I32, (Q, 128), 1) >= HEADDIM
    ys = []
    for g in range(GROUPS):
        cg = cm_ref[:, g * NSTATE:(g + 1) * NSTATE]
        btg = bt_ref[g * NSTATE:(g + 1) * NSTATE, :]
        xg = xs_ref[:, g * 256:(g + 1) * 256]
        sg = s_ref[g]
        cb = _dot(cg, btg)
        sgb = sg.astype(BF16)
        cg32 = cg.astype(F32)
        btg32 = btg.astype(F32)
        yh = []
        uh = []
        for r in range(HPG):
            h = g * HPG + r
            hs = slice((r // 2) * 128, (r // 2 + 1) * 128)
            colb = cum_c[:, h:h + 1]
            rowb = cum_r[h:h + 1, :]
            decay = jnp.exp(jnp.where(lower, colb - rowb, NEG))
            m = (cb * decay * dtr[h:h + 1, :]).astype(BF16)
            cs = (cg32 * jnp.exp(colb)).astype(BF16)
            rhs = jnp.concatenate([xg[:, hs], sgb[:, hs]], axis=0)
            res = _dot(jnp.concatenate([m, cs], axis=1), rhs)
            bw = (btg32 * rfac[h:h + 1, :]).astype(BF16)
            upd = _dot(bw, xg[:, hs])
            if r % 2 == 0:
                yh.append(res)
                uh.append(upd)
            else:
                yh[-1] = jnp.where(odd_head, res, yh[-1])
                uh[-1] = jnp.where(odd_head, upd, uh[-1])
        s_ref[g] = sg * dec_x[:, g * 256:(g + 1) * 256] + jnp.concatenate(uh, axis=1)
        ys.append(jnp.concatenate(yh, axis=1))
    return ys


def _ssd_bwd_kernel(xs_ref, cm_ref, bt_ref, dt_ref, dtT_ref, arow_ref, acol_ref, rexp_ref,
                    y_ref, s_ref):
    @pl.when(pl.program_id(1) == 0)
    def _():
        s_ref[...] = jnp.zeros_like(s_ref)
    for i in range(NB):
        ys = _ssd_chunk(1, xs_ref.at[i], cm_ref.at[i], bt_ref.at[i], dt_ref.at[i], dtT_ref.at[i],
                        arow_ref, acol_ref, rexp_ref, s_ref.at[i])
        for g in range(GROUPS):
            y_ref[i, :, g * 256:(g + 1) * 256] = ys[g].astype(BF16)


def _ssd_fwd_kernel(xs_ref, cm_ref, bt_ref, dt_ref, dtT_ref, arow_ref, acol_ref, rexp_ref,
                    z_ref, yb_ref, dsk_ref, nw_ref, y_ref, s_ref):
    @pl.when(pl.program_id(1) == 0)
    def _():
        s_ref[...] = jnp.zeros_like(s_ref)
    for i in range(NB):
        ys = _ssd_chunk(0, xs_ref.at[i], cm_ref.at[i], bt_ref.at[i], dt_ref.at[i], dtT_ref.at[i],
                        arow_ref, acol_ref, rexp_ref, s_ref.at[i])
        for g in range(GROUPS):
            gs = slice(g * 256, (g + 1) * 256)
            y = ys[g] + yb_ref[i, :, gs].astype(F32) + xs_ref[i, :, gs].astype(F32) * dsk_ref[:, gs]
            y = y * _silu(z_ref[i, :, gs].astype(F32))
            y = y * lax.rsqrt(jnp.mean(y * y, axis=-1, keepdims=True) + EPS) * nw_ref[:, gs]
            y_ref[i, :, gs] = y.astype(BF16)


NB = 2


def _ssd_specs(cmap):
    gn = GROUPS * NSTATE
    full = lambda b, j: (0, 0)
    return [pl.BlockSpec((NB, Q, D_INNER), lambda b, j: (b, cmap(j), 0)),
            pl.BlockSpec((NB, Q, gn), lambda b, j: (b, cmap(j), 0)),
            pl.BlockSpec((NB, gn, Q), lambda b, j: (b, 0, cmap(j))),
            pl.BlockSpec((NB, Q, 2 * HEADS), lambda b, j: (b, cmap(j), 0)),
            pl.BlockSpec((NB, 2 * HEADS, Q), lambda b, j: (b, 0, cmap(j))),
            pl.BlockSpec((1, 2 * HEADS), full),
            pl.BlockSpec((2 * HEADS, 1), full),
            pl.BlockSpec((HEADS, D_INNER), full)]


def _ssd_bwd(xs, cm, bt, dt, dtT, arow, acol, rexp):
    cmap = lambda j: jnp.where(j < CTX_CHUNKS, CTX_CHUNKS - 1 - j, NCHUNK + CTX_CHUNKS - 1 - j)
    omap = lambda b, j: (b, NCHUNK - 1 - jnp.maximum(j, CTX_CHUNKS), 0)
    return pl.pallas_call(
        _ssd_bwd_kernel,
        grid=(BATCH // NB, NCHUNK),
        in_specs=_ssd_specs(cmap),
        out_specs=pl.BlockSpec((NB, Q, D_INNER), omap),
        out_shape=jax.ShapeDtypeStruct((BATCH, SEQ, D_INNER), BF16),
        scratch_shapes=[pltpu.VMEM((NB, GROUPS, NSTATE, HPG * HEADDIM), F32)],
        compiler_params=_cp(("arbitrary", "arbitrary")),
        name="ssd_bwd",
    )(xs, cm, bt, dt, dtT, arow, acol, rexp)


def _ssd_fwd(xs, cm, bt, dt, dtT, arow, acol, rexp, z, yb, dsk, nw):
    cmap = lambda j: j
    lat = lambda b, j: (b, jnp.maximum(j - CTX_CHUNKS, 0), 0)
    full = lambda b, j: (0, 0)
    return pl.pallas_call(
        _ssd_fwd_kernel,
        grid=(BATCH // NB, NCHUNK),
        in_specs=_ssd_specs(cmap) + [
            pl.BlockSpec((NB, Q, D_INNER), lambda b, j: (b, j, 0)),
            pl.BlockSpec((NB, Q, D_INNER), lat),
            pl.BlockSpec((1, D_INNER), full),
            pl.BlockSpec((1, D_INNER), full)],
        out_specs=pl.BlockSpec((NB, Q, D_INNER), lat),
        out_shape=jax.ShapeDtypeStruct((BATCH, SEQ, D_INNER), BF16),
        scratch_shapes=[pltpu.VMEM((NB, GROUPS, NSTATE, HPG * HEADDIM), F32)],
        compiler_params=_cp(("arbitrary", "arbitrary")),
        name="ssd_fwd",
    )(xs, cm, bt, dt, dtT, arow, acol, rexp, z, yb, dsk, nw)


def _route(h, rwT_ref, rb_ref, cnt_ref, eidx_ref, pos_ref, gate_ref):
    hh = h.astype(BF16)
    hl = (h - hh.astype(F32)).astype(BF16)
    w = rwT_ref[...]
    wh = w.astype(BF16)
    wl = (w - wh.astype(F32)).astype(BF16)
    logits = _dot_nt(wh, hh) + _dot_nt(wh, hl) + _dot_nt(wl, hh)
    scores = jax.nn.sigmoid(logits)
    sel = scores + rb_ref[...]
    per = E // NGRP
    sub = lax.broadcasted_iota(I32, (per, TM), 0)
    gscore = []
    for g in range(NGRP):
        blk = sel[g * per:(g + 1) * per, :]
        m1 = jnp.max(blk, axis=0, keepdims=True)
        first = jnp.min(jnp.where(blk == m1, sub, per), axis=0, keepdims=True)
        m2 = jnp.max(jnp.where(sub == first, -jnp.inf, blk), axis=0, keepdims=True)
        gscore.append(m1 + m2)
    masked = []
    for g in range(NGRP):
        rank = jnp.zeros((1, TM), F32)
        for o in range(NGRP):
            if o == g:
                continue
            ahead = (gscore[o] >= gscore[g]) if o < g else (gscore[o] > gscore[g])
            rank = rank + jnp.where(ahead, 1.0, 0.0)
        blk = sel[g * per:(g + 1) * per, :]
        masked.append(jnp.where(rank < TOPG, blk, -jnp.inf))
    v = jnp.concatenate(masked, axis=0)
    eio = lax.broadcasted_iota(I32, (E, TM), 0)
    kio = lax.broadcasted_iota(I32, (TOPK, TM), 0)
    eidx = jnp.zeros((TOPK, TM), I32)
    gsc = jnp.zeros((TOPK, TM), F32)
    hot = jnp.zeros((E, TM), F32)
    picks = []
    for k in range(TOPK):
        m = jnp.max(v, axis=0, keepdims=True)
        first = jnp.min(jnp.where(v == m, eio, E), axis=0, keepdims=True)
        pick = eio == first
        sc = jnp.sum(jnp.where(pick, scores, 0.0), axis=0, keepdims=True)
        eidx = jnp.where(kio == k, first, eidx)
        gsc = jnp.where(kio == k, sc, gsc)
        hot = jnp.where(pick, 1.0, hot)
        v = jnp.where(pick, -jnp.inf, v)
        picks.append(pick)
    gate = gsc / jnp.sum(gsc, axis=0, keepdims=True) * ROUTED_SCALE
    ti = lax.broadcasted_iota(I32, (TM, TM), 0)
    tj = lax.broadcasted_iota(I32, (TM, TM), 1)
    before = jnp.where(ti < tj, 1.0, 0.0).astype(BF16)
    posfull = _dot(hot.astype(BF16), before) + cnt_ref[...]
    pos = jnp.zeros((TOPK, TM), F32)
    for k in range(TOPK):
        pk = jnp.sum(jnp.where(picks[k], posfull, 0.0), axis=0, keepdims=True)
        pos = jnp.where(kio == k, pk, pos)
    cnt_ref[...] = cnt_ref[...] + jnp.sum(hot, axis=1, keepdims=True)
    eidx_ref[...] = eidx
    pos_ref[...] = pos.astype(I32)
    eye = jnp.where(ti == tj, 1.0, 0.0).astype(BF16)
    g1, g2, g3 = _split3(gate)
    gate_ref[...] = _dot_nt(eye, g1) + _dot_nt(eye, g2) + _dot_nt(eye, g3)


def _mix_epilogue(y, x_ref, mod_ref, gpost_ref, gpre_ref, rwT_ref, rb_ref,
                  x1_ref, h2_ref, eidx_ref, pos_ref, gate_ref, cnt_out_ref, cnt_ref):
    first = (pl.program_id(0) == 0) & (pl.program_id(1) == 0)

    @pl.when(first)
    def _():
        cnt_ref[...] = jnp.zeros_like(cnt_ref)
    x1 = x_ref[...] + mod_ref[2:3, :] * _rms(y, gpost_ref[...])
    x1_ref[...] = x1
    h2 = _rms(x1, gpre_ref[...]) * (1.0 + mod_ref[4:5, :]) + mod_ref[3:4, :]
    _store_packed(h2_ref, h2)
    _route(h2, rwT_ref, rb_ref, cnt_ref, eidx_ref, pos_ref, gate_ref)
    cnt_out_ref[...] = cnt_ref[...]


def _ssm_out_kernel(y_ref, w_ref, x_ref, mod_ref, gpost_ref, gpre_ref, rwT_ref, rb_ref,
                    x1_ref, h2_ref, eidx_ref, pos_ref, gate_ref, cnt_out_ref, cnt_ref):
    y = _dot(y_ref[...], w_ref[...])
    _mix_epilogue(y, x_ref, mod_ref, gpost_ref, gpre_ref, rwT_ref, rb_ref,
                  x1_ref, h2_ref, eidx_ref, pos_ref, gate_ref, cnt_out_ref, cnt_ref)


_CW = 256
_PAD = 16


def _conf_out_kernel(u_ref, dww_ref, dwb_ref, lng_ref, lnb_ref, w_ref, b_ref,
                     x_ref, mod_ref, gpost_ref, gpre_ref, rwT_ref, rb_ref,
                     x1_ref, h2_ref, eidx_ref, pos_ref, gate_ref, cnt_out_ref,
                     cnt_ref, v_ref):
    nrow = TM // GRID_W
    ext_rows = GRID_W + 2 * _PAD
    first = _PAD - CONF_K // 2
    span = GRID_W + 8 * ((CONF_K - 1) // 8)
    assert first + 7 + span <= ext_rows
    ri = lax.broadcasted_iota(I32, (span, ext_rows), 0)
    ji = lax.broadcasted_iota(I32, (span, ext_rows), 1)
    shift = [jnp.where(ji == ri + first + b, 1.0, 0.0).astype(BF16) for b in range(8)]
    zpad = jnp.zeros((_PAD, D), BF16)
    for r in range(nrow):
        ext = jnp.concatenate([zpad, u_ref[r * GRID_W:(r + 1) * GRID_W, :], zpad], axis=0)
        for c in range(D // _CW):
            cs = slice(c * _CW, (c + 1) * _CW)
            acc = jnp.broadcast_to(dwb_ref[:, cs], (GRID_W, _CW))
            for b in range(8):
                win = _dot(shift[b], ext[:, cs])
                for a in range(-(-CONF_K // 8)):
                    k = 8 * a + b
                    if k < CONF_K:
                        wk = jnp.concatenate([dww_ref[8 * k:8 * k + 8, cs]] * (GRID_W // 8), axis=0)
                        acc = acc + wk * win[8 * a:8 * a + GRID_W, :]
            v_ref[r * GRID_W:(r + 1) * GRID_W, cs] = acc
    v = v_ref[...]
    mu = jnp.mean(v, axis=-1, keepdims=True)
    vc = v - mu
    ln = vc * lax.rsqrt(jnp.mean(vc * vc, axis=-1, keepdims=True) + EPS) * lng_ref[...] + lnb_ref[...]
    y = _dot(_silu(ln).astype(BF16), w_ref[...]) + b_ref[...]
    _mix_epilogue(y, x_ref, mod_ref, gpost_ref, gpre_ref, rwT_ref, rb_ref,
                  x1_ref, h2_ref, eidx_ref, pos_ref, gate_ref, cnt_out_ref, cnt_ref)


def _mix_out_common_specs():
    full = lambda b, i: (0, 0)
    nt = SEQ // TM
    in_specs = [pl.BlockSpec((None, TM, D), lambda b, i: (b, i, 0)),
                pl.BlockSpec((None, 6, D), lambda b, i: (b, 0, 0)),
                pl.BlockSpec((1, D), full), pl.BlockSpec((1, D), full),
                pl.BlockSpec((E, D), full), pl.BlockSpec((E, 1), full)]
    out_specs = [pl.BlockSpec((None, TM, D), lambda b, i: (b, i, 0)),
                 pl.BlockSpec((TM * RW, 128), lambda b, i: (b * nt + i, 0)),
                 pl.BlockSpec((TOPK, TM), lambda b, i: (0, b * nt + i)),
                 pl.BlockSpec((TOPK, TM), lambda b, i: (0, b * nt + i)),
                 pl.BlockSpec((TM, TOPK), lambda b, i: (b * nt + i, 0)),
                 pl.BlockSpec((E, 1), full)]
    out_shape = [jax.ShapeDtypeStruct((BATCH, SEQ, D), F32),
                 jax.ShapeDtypeStruct((T * RW, 128), U32),
                 jax.ShapeDtypeStruct((TOPK, T), I32),
                 jax.ShapeDtypeStruct((TOPK, T), I32),
                 jax.ShapeDtypeStruct((T, TOPK), F32),
                 jax.ShapeDtypeStruct((E, 1), F32)]
    return in_specs, out_specs, out_shape


def _ssm_out(y, w, x, mod, gpost, gpre, rwT, rb):
    common_in, out_specs, out_shape = _mix_out_common_specs()
    return pl.pallas_call(
        _ssm_out_kernel,
        grid=(BATCH, SEQ // TM),
        in_specs=[pl.BlockSpec((None, TM, D_INNER), lambda b, i: (b, i, 0)),
                  pl.BlockSpec((D_INNER, D), lambda b, i: (0, 0))] + common_in,
        out_specs=out_specs, out_shape=out_shape,
        scratch_shapes=[pltpu.VMEM((E, 1), F32)],
        compiler_params=_cp(("arbitrary", "arbitrary")),
        name="ssm_out",
    )(y, w, x, mod, gpost, gpre, rwT, rb)


def _conf_out(u, dww, dwb, lng, lnb, w, b, x, mod, gpost, gpre, rwT, rb):
    common_in, out_specs, out_shape = _mix_out_common_specs()
    full = lambda b_, i: (0, 0)
    return pl.pallas_call(
        _conf_out_kernel,
        grid=(BATCH, SEQ // TM),
        in_specs=[pl.BlockSpec((None, TM, D), lambda b_, i: (b_, i, 0)),
                  pl.BlockSpec((CONF_K * 8, D), full), pl.BlockSpec((1, D), full),
                  pl.BlockSpec((1, D), full), pl.BlockSpec((1, D), full),
                  pl.BlockSpec((D, D), full), pl.BlockSpec((1, D), full)] + common_in,
        out_specs=out_specs, out_shape=out_shape,
        scratch_shapes=[pltpu.VMEM((E, 1), F32), pltpu.VMEM((TM, D), F32)],
        compiler_params=_cp(("arbitrary", "arbitrary")),
        name="conf_out",
    )(u, dww, dwb, lng, lnb, w, b, x, mod, gpost, gpre, rwT, rb)


def _conf_in_kernel(x_ref, mod_ref, g_ref, wa_ref, wg_ref, ba_ref, bg_ref, u_ref):
    h = _rms(x_ref[...], g_ref[...]) * (1.0 + mod_ref[1:2, :]) + mod_ref[0:1, :]
    hb = h.astype(BF16)
    a = _dot(hb, wa_ref[...]) + ba_ref[...]
    g = _dot(hb, wg_ref[...]) + bg_ref[...]
    u_ref[...] = (a * jax.nn.sigmoid(g)).astype(BF16)


def _conf_in(x, mod, gain, w, bias):
    full = lambda b, i: (0, 0)
    return pl.pallas_call(
        _conf_in_kernel,
        grid=(BATCH, SEQ // TM),
        in_specs=[pl.BlockSpec((None, TM, D), lambda b, i: (b, i, 0)),
                  pl.BlockSpec((None, 6, D), lambda b, i: (b, 0, 0)),
                  pl.BlockSpec((1, D), full),
                  pl.BlockSpec((D, D), full), pl.BlockSpec((D, D), lambda b, i: (0, 1)),
                  pl.BlockSpec((1, D), full), pl.BlockSpec((1, D), lambda b, i: (0, 1))],
        out_specs=pl.BlockSpec((None, TM, D), lambda b, i: (b, i, 0)),
        out_shape=jax.ShapeDtypeStruct((BATCH, SEQ, D), BF16),
        compiler_params=_cp(("arbitrary", "arbitrary")),
        name="conf_in",
    )(x, mod, gain, w, w, bias, bias)


def _row_copy(src, si, dst, di, sem):
    return pltpu.make_async_copy(src.at[pl.ds(pl.multiple_of(si * RW, RW), RW), :],
                                 dst.at[pl.ds(pl.multiple_of(di * RW, RW), RW), :], sem)


ZROWS = BLK + 8
_TU = 4


def _dispatch_kernel(zs_ref, dest_ref, h_ref, xs_ref, zero_ref, zsem, sem):
    @pl.when(pl.program_id(0) == 0)
    def _():
        zero_ref[...] = jnp.zeros_like(zero_ref)

        def zcopy(e):
            start = pl.multiple_of(zs_ref[e] * RW, 8 * RW)
            return pltpu.make_async_copy(zero_ref, xs_ref.at[pl.ds(start, ZROWS * RW), :], zsem)

        def zstart(e, c):
            zcopy(e).start()
            return c

        def zwait(e, c):
            zcopy(e).wait()
            return c
        lax.fori_loop(0, E, zstart, 0)
        lax.fori_loop(0, E, zwait, 0)

    def issue(tb, c):
        for u in range(_TU):
            t = tb * _TU + u
            for k in range(TOPK):
                _row_copy(h_ref, t, xs_ref, dest_ref[t * TOPK + k], sem).start(priority=k % 2)
        return c

    lax.fori_loop(0, TM // _TU, issue, 0)
    for k in range(TOPK):
        pltpu.make_async_copy(h_ref, xs_ref.at[pl.ds(0, TM * RW), :], sem).wait()


def _dispatch(zstart, dest, hp):
    grid_spec = pltpu.PrefetchScalarGridSpec(
        num_scalar_prefetch=1,
        grid=(T // TM,),
        in_specs=[pl.BlockSpec((TM * TOPK,), lambda i, zs: (i,), memory_space=pltpu.SMEM),
                  pl.BlockSpec((TM * RW, 128), lambda i, zs: (i, 0))],
        out_specs=pl.BlockSpec(memory_space=pl.ANY),
        scratch_shapes=[pltpu.VMEM((ZROWS * RW, 128), U32), pltpu.SemaphoreType.DMA(()),
                        pltpu.SemaphoreType.DMA(())])
    return pl.pallas_call(
        _dispatch_kernel,
        grid_spec=grid_spec,
        out_shape=jax.ShapeDtypeStruct((PROWS * RW, 128), U32),
        compiler_params=_cp(("arbitrary",)),
        name="moe_dispatch",
    )(zstart, dest, hp)


XRING = 3


def _experts_kernel(be_ref, na_ref, x_hbm, wg_ref, wu_ref, wd_ref, y_ref, wgb_ref, wub_ref, wdb_ref,
                    xbuf_ref, xsem):
    i = pl.program_id(0)
    na = na_ref[0]

    def fetch(j):
        rows = BLK * RW
        src = x_hbm.at[pl.ds(pl.multiple_of(j * rows, rows), rows), :]
        return pltpu.make_async_copy(src, xbuf_ref.at[j % XRING], xsem.at[j % XRING])

    @pl.when(i == 0)
    def _():
        for j in range(XRING - 1):
            @pl.when(j < na)
            def _():
                fetch(j).start()

    @pl.when(i + XRING - 1 < na)
    def _():
        fetch(i + XRING - 1).start()

    @pl.when((i == 0) | (be_ref[i] != be_ref[jnp.maximum(i - 1, 0)]))
    def _():
        wgb_ref[...] = wg_ref[...].astype(BF16)
        wub_ref[...] = wu_ref[...].astype(BF16)
        wdb_ref[...] = wd_ref[...].astype(BF16)

    @pl.when(i < na)
    def _():
        fetch(i).wait()
        xb = _load_packed(xbuf_ref.at[i % XRING], BLK)
        g = _dot(xb, wgb_ref[...])
        u = _dot(xb, wub_ref[...])
        a = (_silu(g) * u).astype(BF16)
        _store_packed(y_ref, _dot(a, wdb_ref[...]))

    @pl.when(i >= na)
    def _():
        y_ref[...] = jnp.zeros_like(y_ref)


def _experts(layer, block_e, nact, xs, wg, wu, wd):
    wmap = lambda i, be, na: (layer, be[i], 0, 0)
    grid_spec = pltpu.PrefetchScalarGridSpec(
        num_scalar_prefetch=2,
        grid=(NBLK,),
        in_specs=[pl.BlockSpec(memory_space=pl.ANY),
                  pl.BlockSpec((None, None, D, DE), wmap),
                  pl.BlockSpec((None, None, D, DE), wmap),
                  pl.BlockSpec((None, None, DE, D), wmap)],
        out_specs=pl.BlockSpec((BLK * RW, 128), lambda i, be, na: (i, 0)),
        scratch_shapes=[pltpu.VMEM((D, DE), BF16), pltpu.VMEM((D, DE), BF16),
                        pltpu.VMEM((DE, D), BF16),
                        pltpu.VMEM((XRING, BLK * RW, 128), U32), pltpu.SemaphoreType.DMA((XRING,))])
    return pl.pallas_call(
        _experts_kernel,
        grid_spec=grid_spec,
        out_shape=jax.ShapeDtypeStruct((NBLK * BLK * RW, 128), U32),
        compiler_params=_cp(("arbitrary",)),
        name="moe_experts",
    )(block_e, nact, xs, wg, wu, wd)


def _combine_kernel(dcur_ref, dnxt_ref, gate_ref, h_ref, x1_ref, mod_ref, gpost_ref,
                    sg_ref, su_ref, sd_ref, ys_ref, o_ref, buf_ref, sems):
    i = pl.program_id(0)
    n = pl.num_programs(0)

    def gather(dest_ref, slot):
        def body(tb, c):
            for u in range(_TU):
                t = tb * _TU + u
                for k in range(TOPK):
                    _row_copy(ys_ref, dest_ref[t * TOPK + k], buf_ref.at[slot, k], t,
                              sems.at[slot]).start(priority=k % 2)
            return c
        lax.fori_loop(0, TM // _TU, body, 0)

    @pl.when(i == 0)
    def _():
        gather(dcur_ref, 0)

    for slot_next in range(2):
        @pl.when((i + 1 < n) & ((i + 1) % 2 == slot_next))
        def _():
            gather(dnxt_ref, slot_next)

    hb = _load_packed(h_ref, TM)
    a = (_silu(_dot(hb, sg_ref[...])) * _dot(hb, su_ref[...])).astype(BF16)
    f = _dot(a, sd_ref[...])

    slot = i % 2

    for k in range(TOPK):
        pltpu.make_async_copy(ys_ref.at[pl.ds(0, TM * RW), :], buf_ref.at[slot, k],
                              sems.at[slot]).wait()

    gate = gate_ref[...]
    gb = [jnp.broadcast_to(gate[:, k:k + 1], (TM, 128)) for k in range(TOPK)]
    los, his = [], []
    for s in range(RW):
        lo = f[:, s * 128:(s + 1) * 128]
        hi = f[:, D // 2 + s * 128:D // 2 + (s + 1) * 128]
        for k in range(TOPK):
            wl, wh = _unpack2(buf_ref[slot, k, pl.ds(s, TM, stride=RW), :])
            lo = lo + gb[k] * wl
            hi = hi + gb[k] * wh
        los.append(lo)
        his.append(hi)
    f = jnp.concatenate(los + his, axis=1)
    o_ref[...] = x1_ref[...] + mod_ref[5:6, :] * _rms(f, gpost_ref[...])


def _combine(dest, gate, hp, x1, mod, gpost, sg, su, sd, ys):
    nt = SEQ // TM
    n = T // TM
    full = lambda i: (0, 0)
    return pl.pallas_call(
        _combine_kernel,
        grid=(n,),
        in_specs=[pl.BlockSpec((TM * TOPK,), lambda i: (i,), memory_space=pltpu.SMEM),
                  pl.BlockSpec((TM * TOPK,), lambda i: (jnp.minimum(i + 1, n - 1),),
                               memory_space=pltpu.SMEM),
                  pl.BlockSpec((TM, TOPK), lambda i: (i, 0)),
                  pl.BlockSpec((TM * RW, 128), lambda i: (i, 0)),
                  pl.BlockSpec((TM, D), lambda i: (i, 0)),
                  pl.BlockSpec((None, 6, D), lambda i: (i // nt, 0, 0)),
                  pl.BlockSpec((1, D), full),
                  pl.BlockSpec((D, DE), full), pl.BlockSpec((D, DE), full), pl.BlockSpec((DE, D), full),
                  pl.BlockSpec(memory_space=pl.ANY)],
        out_specs=pl.BlockSpec((TM, D), lambda i: (i, 0)),
        out_shape=jax.ShapeDtypeStruct((T, D), F32),
        scratch_shapes=[pltpu.VMEM((2, TOPK, TM * RW, 128), U32), pltpu.SemaphoreType.DMA((2,))],
        compiler_params=_cp(("arbitrary",)),
        name="moe_combine",
    )(dest, dest, gate, hp, x1.reshape(T, D), mod, gpost, sg, su, sd, ys).reshape(BATCH, SEQ, D)


def _moe(layer, x1, hp, eidx, pos, gate, counts, mod, gpost, wg, wu, wd, sg, su, sd):
    cnt = counts.reshape(E).astype(I32)
    padded = (cnt + BLK - 1) // BLK * BLK
    pend = jnp.cumsum(padded)
    pstart = pend - padded
    ids = jnp.arange(E, dtype=I32)
    dest = pos + jnp.sum(jnp.where(eidx[:, :, None] == ids, pstart, 0), axis=-1)
    dest = dest.T.reshape(T * TOPK)
    nact = (pend[-1] // BLK).reshape(1).astype(I32)
    blk0 = jnp.arange(NBLK, dtype=I32) * BLK
    block_e = jnp.sum((pend[None, :] <= blk0[:, None]).astype(I32), axis=1)
    block_e = jnp.minimum(block_e, E - 1).astype(I32)
    xs = _dispatch(((pstart + cnt) // 8 * 8).astype(I32), dest, hp)
    ys = _experts(layer, block_e, nact, xs, wg, wu, wd)
    return _combine(dest, gate, hp, x1, mod, gpost, sg.astype(BF16), su.astype(BF16),
                    sd.astype(BF16), ys)


def kernel(x, c, ctx, c_ctx, ada_w, ada_b, norm_mix_pre, norm_mix_post, norm_ffn_pre, norm_ffn_post, ssm_w_in, ssm_conv_w, ssm_conv_b, ssm_dt_bias, ssm_a_log, ssm_d, ssm_norm, ssm_w_out, cv_w_in, cv_b_in, cv_dw_w, cv_dw_b, cv_ln_g, cv_ln_b, cv_w_out, cv_b_out, router_w, router_b, exp_w_gate, exp_w_up, exp_w_down, sh_w_gate, sh_w_up, sh_w_down):
    row = lambda v: v.reshape(1, -1)
    cvec = jnp.concatenate([c, c_ctx[None, :], jnp.zeros((3, D), F32)], axis=0)
    mod = _ada(cvec, ada_w, ada_b).reshape(2, 8, 6, D)

    w_in = ssm_w_in[0].astype(BF16)
    dtb = ssm_dt_bias[0].reshape(1, 2 * HEADS)
    wdt = w_in[:, D_INNER + CONV_DIM:]
    z, xbc, dt, dtT = _ssm_in(ctx, x, mod[0], row(norm_mix_pre[0]), w_in, wdt, wdt.T,
                              dtb, dtb.reshape(2 * HEADS, 1))
    xs, bt, cm = _ssm_conv(xbc, jnp.repeat(ssm_conv_w[0], 8, axis=0), row(ssm_conv_b[0]))
    a = -jnp.exp(ssm_a_log[0].astype(F32)).reshape(1, 2 * HEADS)
    rexp = (jnp.arange(D_INNER)[None, :] // HEADDIM == jnp.arange(HEADS)[:, None]).astype(BF16)
    yb = _ssd_bwd(xs, cm, bt, dt, dtT, a, a.reshape(2 * HEADS, 1), rexp)
    dsk = jnp.repeat(ssm_d[0], HEADDIM).reshape(1, D_INNER)
    ygn = _ssd_fwd(xs, cm, bt, dt, dtT, a, a.reshape(2 * HEADS, 1), rexp, z, yb, dsk,
                   row(ssm_norm[0]))
    x1, h2, eidx, pos, gate, counts = _ssm_out(
        ygn, ssm_w_out[0].astype(BF16), x, mod[0], row(norm_mix_post[0]), row(norm_ffn_pre[0]),
        router_w[0].T, router_b[0].reshape(E, 1))
    x2 = _moe(0, x1, h2, eidx, pos, gate, counts, mod[0], row(norm_ffn_post[0]),
              exp_w_gate, exp_w_up, exp_w_down, sh_w_gate[0], sh_w_up[0], sh_w_down[0])

    u = _conf_in(x2, mod[1], row(norm_mix_pre[1]), cv_w_in[0].astype(BF16), row(cv_b_in[0]))
    x3, h4, eidx, pos, gate, counts = _conf_out(
        u, jnp.repeat(cv_dw_w[0], 8, axis=0), row(cv_dw_b[0]), row(cv_ln_g[0]), row(cv_ln_b[0]),
        cv_w_out[0].astype(BF16), row(cv_b_out[0]),
        x2, mod[1], row(norm_mix_post[1]), row(norm_ffn_pre[1]),
        router_w[1].T, router_b[1].reshape(E, 1))
    return _moe(1, x3, h4, eidx, pos, gate, counts, mod[1], row(norm_ffn_post[1]),
                exp_w_gate, exp_w_up, exp_w_down, sh_w_gate[1], sh_w_up[1], sh_w_down[1])
```

```python
import functools

import jax
import jax.numpy as jnp
from jax import lax
from jax.experimental import pallas as pl
from jax.experimental.pallas import tpu as pltpu

F32 = jnp.float32
BF16 = jnp.bfloat16
I32 = jnp.int32

D = 1024
BATCH = 4
SEQ = 4096
CTX = 256
LTOT = CTX + SEQ
GRID_W = 64

D_INNER = 2048
HEADS = 32
GROUPS = 8
HPG = 4
HEADDIM = 64
NSTATE = 128
Q = 128
NCHUNK = LTOT // Q
CTX_CHUNKS = CTX // Q
CONV_DIM = D_INNER + 2 * GROUPS * NSTATE
SSM_K = 5
CONF_K = 31

E = 64
TOPK = 8
NGRP = 8
TOPG = 4
DE = 256
ROUTED_SCALE = 2.5
EPS = 1e-6

T = BATCH * SEQ
TM = 512
BLK = 512
NBLK = -(-(T * TOPK + E * (BLK - 1)) // BLK)
PROWS = (NBLK + 1) * BLK + 8

VMEM_LIMIT = 56 * 1024 * 1024
NEG = -1e30


def _cp(sem):
    return pltpu.CompilerParams(dimension_semantics=sem, vmem_limit_bytes=VMEM_LIMIT)


def _silu(v):
    return v * jax.nn.sigmoid(v)


def _rms(v, g):
    return v * lax.rsqrt(jnp.mean(v * v, axis=-1, keepdims=True) + EPS) * g


def _split3(v):
    a = v.astype(BF16)
    r = v - a.astype(F32)
    b = r.astype(BF16)
    c = (r - b.astype(F32)).astype(BF16)
    return a, b, c


def _dot(a, b):
    return jnp.dot(a, b, preferred_element_type=F32)


def _dot_nt(a, b):
    return lax.dot_general(a, b, (((1,), (1,)), ((), ())), preferred_element_type=F32)


U32 = jnp.uint32
RW = D // 2 // 128
_HI = 0xFFFF0000


def _pack2(lo, hi):
    ul = pltpu.bitcast(lo.astype(BF16).astype(F32), U32)
    uh = pltpu.bitcast(hi.astype(BF16).astype(F32), U32)
    return (ul >> 16) | (uh & U32(_HI))


def _unpack2(w):
    return pltpu.bitcast(w << 16, F32), pltpu.bitcast(w & U32(_HI), F32)


def _store_packed(ref, v):
    m = v.shape[0]
    for s in range(RW):
        lo = v[:, s * 128:(s + 1) * 128]
        hi = v[:, D // 2 + s * 128:D // 2 + (s + 1) * 128]
        ref[pl.ds(s, m, stride=RW), :] = _pack2(lo, hi)


def _load_packed(ref, m):
    los, his = [], []
    for s in range(RW):
        lo, hi = _unpack2(ref[pl.ds(s, m, stride=RW), :])
        los.append(lo.astype(BF16))
        his.append(hi.astype(BF16))
    return jnp.concatenate(los + his, axis=1)


def _ada_kernel(c_ref, w_ref, b_ref, o_ref):
    s = _silu(c_ref[...])
    o_ref[...] = jnp.dot(s, w_ref[...], preferred_element_type=F32,
                         precision=lax.Precision.HIGHEST) + b_ref[...]


def _ada(cvec, ada_w, ada_b):
    depth = ada_w.shape[0]
    tn = 1536
    return pl.pallas_call(
        _ada_kernel,
        grid=(depth, 6 * D // tn),
        in_specs=[pl.BlockSpec((8, D), lambda l, j: (0, 0)),
                  pl.BlockSpec((None, D, tn), lambda l, j: (l, 0, j)),
                  pl.BlockSpec((None, 1, tn), lambda l, j: (l, 0, j))],
        out_specs=pl.BlockSpec((None, 8, tn), lambda l, j: (l, 0, j)),
        out_shape=jax.ShapeDtypeStruct((depth, 8, 6 * D), F32),
        compiler_params=_cp(("arbitrary", "arbitrary")),
        name="ada",
    )(cvec, ada_w, ada_b.reshape(depth, 1, 6 * D))


def _ssm_in_kernel(c_ref, x_ref, mod_ref, g_ref, wz_ref, wx0_ref, wx1_ref, wdt_ref, wdtT_ref, dtb_ref,
                   dtbT_ref, z_ref, xbc_ref, dt_ref, dtT_ref):
    xin = jnp.where(pl.program_id(1) == 0, c_ref[...], x_ref[...])
    h = _rms(xin, g_ref[...]) * (1.0 + mod_ref[1:2, :]) + mod_ref[0:1, :]
    hb = h.astype(BF16)
    z_ref[...] = _dot(hb, wz_ref[...]).astype(BF16)
    xbc_ref[:, :D_INNER] = _dot(hb, wx0_ref[...]).astype(BF16)
    xbc_ref[:, D_INNER:] = _dot(hb, wx1_ref[...]).astype(BF16)
    dt_ref[...] = jax.nn.softplus(_dot(hb, wdt_ref[...]) + dtb_ref[...])
    dtT_ref[...] = jax.nn.softplus(_dot_nt(wdtT_ref[...], hb) + dtbT_ref[...])


def _ssm_in(ctx, x, mod, gain, w_in, wdt, wdtT, dtb, dtbT):
    ts = CTX
    nt = LTOT // ts
    full = lambda b, i: (0, 0)
    assert CONV_DIM == 2 * D_INNER and SEQ % ts == 0
    return pl.pallas_call(
        _ssm_in_kernel,
        grid=(BATCH, nt),
        in_specs=[pl.BlockSpec((None, ts, D), lambda b, i: (b, 0, 0)),
                  pl.BlockSpec((None, ts, D), lambda b, i: (b, jnp.maximum(i - 1, 0), 0)),
                  pl.BlockSpec((None, 6, D), lambda b, i: (jnp.where(i == 0, BATCH, b), 0, 0)),
                  pl.BlockSpec((1, D), full),
                  pl.BlockSpec((D, D_INNER), lambda b, i: (0, 0)),
                  pl.BlockSpec((D, D_INNER), lambda b, i: (0, 1)),
                  pl.BlockSpec((D, D_INNER), lambda b, i: (0, 2)),
                  pl.BlockSpec((D, 2 * HEADS), full),
                  pl.BlockSpec((2 * HEADS, D), full),
                  pl.BlockSpec((1, 2 * HEADS), full),
                  pl.BlockSpec((2 * HEADS, 1), full)],
        out_specs=[pl.BlockSpec((None, ts, D_INNER), lambda b, i: (b, i, 0)),
                   pl.BlockSpec((None, ts, CONV_DIM), lambda b, i: (b, i, 0)),
                   pl.BlockSpec((None, ts, 2 * HEADS), lambda b, i: (b, i, 0)),
                   pl.BlockSpec((None, 2 * HEADS, ts), lambda b, i: (b, 0, i))],
        out_shape=[jax.ShapeDtypeStruct((BATCH, LTOT, D_INNER), BF16),
                   jax.ShapeDtypeStruct((BATCH, LTOT, CONV_DIM), BF16),
                   jax.ShapeDtypeStruct((BATCH, LTOT, 2 * HEADS), F32),
                   jax.ShapeDtypeStruct((BATCH, 2 * HEADS, LTOT), F32)],
        compiler_params=_cp(("arbitrary", "arbitrary")),
        name="ssm_in",
    )(ctx, x, mod, gain, w_in, w_in, w_in, wdt, wdtT, dtb, dtbT)


_HALO = 16
_CC = 512


def _ssm_conv_kernel(x_ref, xp_ref, xn_ref, w_ref, b_ref, xs_ref, bt_ref, cm_ref):
    i = pl.program_id(1)
    first_lat = CTX // Q
    pvalid = (i != 0) & (i != first_lat)
    nvalid = (i != first_lat - 1) & (i != NCHUNK - 1)
    li = lax.broadcasted_iota(I32, (Q, Q + 2 * _HALO), 0)
    ji = lax.broadcasted_iota(I32, (Q, Q + 2 * _HALO), 1)
    shift = {k: jnp.where(ji == li + _HALO + k - SSM_K // 2, 1.0, 0.0).astype(BF16)
             for k in range(SSM_K) if k != SSM_K // 2}
    for c in range(CONV_DIM // _CC):
        cs = slice(c * _CC, (c + 1) * _CC)
        cur = x_ref[:, cs]
        prev = jnp.where(pvalid, xp_ref[:, cs], jnp.zeros((_HALO, _CC), BF16))
        nxt = jnp.where(nvalid, xn_ref[:, cs], jnp.zeros((_HALO, _CC), BF16))
        ext = jnp.concatenate([prev, cur, nxt], axis=0)
        acc = jnp.broadcast_to(b_ref[:, cs], (Q, _CC))
        for k in range(SSM_K):
            wk = jnp.concatenate([w_ref[8 * k:8 * k + 8, cs]] * (Q // 8), axis=0)
            tap = cur.astype(F32) if k == SSM_K // 2 else _dot(shift[k], ext)
            acc = acc + wk * tap
        y = _silu(acc)
        lo = c * _CC
        if lo < D_INNER:
            xs_ref[:, cs] = y.astype(BF16)
        elif lo < D_INNER + GROUPS * NSTATE:
            o = lo - D_INNER
            bt_ref[o:o + _CC, :] = y.T.astype(BF16)
        else:
            o = lo - D_INNER - GROUPS * NSTATE
            cm_ref[:, o:o + _CC] = y.astype(BF16)


def _ssm_conv(xbc, w, b):
    nh = Q // _HALO
    last = LTOT // _HALO - 1
    gn = GROUPS * NSTATE
    return pl.pallas_call(
        _ssm_conv_kernel,
        grid=(BATCH, NCHUNK),
        in_specs=[pl.BlockSpec((None, Q, CONV_DIM), lambda b, i: (b, i, 0)),
                  pl.BlockSpec((None, _HALO, CONV_DIM), lambda b, i: (b, jnp.maximum(i * nh - 1, 0), 0)),
                  pl.BlockSpec((None, _HALO, CONV_DIM), lambda b, i: (b, jnp.minimum(i * nh + nh, last), 0)),
                  pl.BlockSpec((SSM_K * 8, CONV_DIM), lambda b, i: (0, 0)),
                  pl.BlockSpec((1, CONV_DIM), lambda b, i: (0, 0))],
        out_specs=[pl.BlockSpec((None, Q, D_INNER), lambda b, i: (b, i, 0)),
                   pl.BlockSpec((None, gn, Q), lambda b, i: (b, 0, i)),
                   pl.BlockSpec((None, Q, gn), lambda b, i: (b, i, 0))],
        out_shape=[jax.ShapeDtypeStruct((BATCH, LTOT, D_INNER), BF16),
                   jax.ShapeDtypeStruct((BATCH, gn, LTOT), BF16),
                   jax.ShapeDtypeStruct((BATCH, LTOT, gn), BF16)],
        compiler_params=_cp(("arbitrary", "arbitrary")),
        name="ssm_conv",
    )(xbc, xbc, xbc, w, b)


def _ssd_chunk(direction, xs_ref, cm_ref, bt_ref, dt_ref, dtT_ref, arow_ref, acol_ref, rexp_ref, s_ref):
    d0 = direction * HEADS
    dtc = dt_ref[:, d0:d0 + HEADS]
    dtr = dtT_ref[d0:d0 + HEADS, :]
    da_c = dtc * arow_ref[:, d0:d0 + HEADS]
    da_r = dtr * acol_ref[d0:d0 + HEADS, :]
    ii = lax.broadcasted_iota(I32, (Q, Q), 0)
    jj = lax.broadcasted_iota(I32, (Q, Q), 1)
    if direction == 0:
        lower = jj <= ii
        tot_idx = Q - 1
    else:
        lower = jj >= ii
        tot_idx = 0
    tri_c = jnp.where(lower, 1.0, 0.0).astype(BF16)
    upper = (ii <= jj) if direction == 0 else (ii >= jj)
    tri_r = jnp.where(upper, 1.0, 0.0).astype(BF16)
    c1, c2, c3 = _split3(da_c)
    cum_c = _dot(tri_c, c1) + _dot(tri_c, c2) + _dot(tri_c, c3)
    r1, r2, r3 = _split3(da_r)
    cum_r = _dot(r1, tri_r) + _dot(r2, tri_r) + _dot(r3, tri_r)
    tot_c = cum_c[tot_idx:tot_idx + 1, :]
    tot_r = cum_r[:, tot_idx:tot_idx + 1]
    rfac = dtr * jnp.exp(tot_r - cum_r)
    dec = jnp.exp(tot_c)
    dh = dec.astype(BF16)
    dl = (dec - dh.astype(F32)).astype(BF16)
    dec_x = (_dot(jnp.broadcast_to(dh, (8, HEADS)), rexp_ref[...])
             + _dot(jnp.broadcast_to(dl, (8, HEADS)), rexp_ref[...]))[0:1, :]
    assert Q == NSTATE and 2 * HEADDIM == 128
    odd_head = lax.broadcasted_iota(I32, (Q, 128), 1) >= HEADDIM
    ys = []
    for g in range(GROUPS):
        cg = cm_ref[:, g * NSTATE:(g + 1) * NSTATE]
        btg = bt_ref[g * NSTATE:(g + 1) * NSTATE, :]
        xg = xs_ref[:, g * 256:(g + 1) * 256]
        sg = s_ref[g]
        cb = _dot(cg, btg)
        sgb = sg.astype(BF16)
        cg32 = cg.astype(F32)
        btg32 = btg.astype(F32)
        yh = []
        uh = []
        for r in range(HPG):
            h = g * HPG + r
            hs = slice((r // 2) * 128, (r // 2 + 1) * 128)
            colb = cum_c[:, h:h + 1]
            rowb = cum_r[h:h + 1, :]
            decay = jnp.exp(jnp.where(lower, colb - rowb, NEG))
            m = (cb * decay * dtr[h:h + 1, :]).astype(BF16)
            cs = (cg32 * jnp.exp(colb)).astype(BF16)
            rhs = jnp.concatenate([xg[:, hs], sgb[:, hs]], axis=0)
            res = _dot(jnp.concatenate([m, cs], axis=1), rhs)
            bw = (btg32 * rfac[h:h + 1, :]).astype(BF16)
            upd = _dot(bw, xg[:, hs])
            if r % 2 == 0:
                yh.append(res)
                uh.append(upd)
            else:
                yh[-1] = jnp.where(odd_head, res, yh[-1])
                uh[-1] = jnp.where(odd_head, upd, uh[-1])
        s_ref[g] = sg * dec_x[:, g * 256:(g + 1) * 256] + jnp.concatenate(uh, axis=1)
        ys.append(jnp.concatenate(yh, axis=1))
    return ys


def _ssd_bwd_kernel(xs_ref, cm_ref, bt_ref, dt_ref, dtT_ref, arow_ref, acol_ref, rexp_ref,
                    y_ref, s_ref):
    @pl.when(pl.program_id(1) == 0)
    def _():
        s_ref[...] = jnp.zeros_like(s_ref)
    for i in range(NB):
        ys = _ssd_chunk(1, xs_ref.at[i], cm_ref.at[i], bt_ref.at[i], dt_ref.at[i], dtT_ref.at[i],
                        arow_ref, acol_ref, rexp_ref, s_ref.at[i])
        for g in range(GROUPS):
            y_ref[i, :, g * 256:(g + 1) * 256] = ys[g].astype(BF16)


def _ssd_fwd_kernel(xs_ref, cm_ref, bt_ref, dt_ref, dtT_ref, arow_ref, acol_ref, rexp_ref,
                    z_ref, yb_ref, dsk_ref, nw_ref, y_ref, s_ref):
    @pl.when(pl.program_id(1) == 0)
    def _():
        s_ref[...] = jnp.zeros_like(s_ref)
    for i in range(NB):
        ys = _ssd_chunk(0, xs_ref.at[i], cm_ref.at[i], bt_ref.at[i], dt_ref.at[i], dtT_ref.at[i],
                        arow_ref, acol_ref, rexp_ref, s_ref.at[i])
        for g in range(GROUPS):
            gs = slice(g * 256, (g + 1) * 256)
            y = ys[g] + yb_ref[i, :, gs].astype(F32) + xs_ref[i, :, gs].astype(F32) * dsk_ref[:, gs]
            y = y * _silu(z_ref[i, :, gs].astype(F32))
            y = y * lax.rsqrt(jnp.mean(y * y, axis=-1, keepdims=True) + EPS) * nw_ref[:, gs]
            y_ref[i, :, gs] = y.astype(BF16)


NB = 2


def _ssd_specs(cmap):
    gn = GROUPS * NSTATE
    full = lambda b, j: (0, 0)
    return [pl.BlockSpec((NB, Q, D_INNER), lambda b, j: (b, cmap(j), 0)),
            pl.BlockSpec((NB, Q, gn), lambda b, j: (b, cmap(j), 0)),
            pl.BlockSpec((NB, gn, Q), lambda b, j: (b, 0, cmap(j))),
            pl.BlockSpec((NB, Q, 2 * HEADS), lambda b, j: (b, cmap(j), 0)),
            pl.BlockSpec((NB, 2 * HEADS, Q), lambda b, j: (b, 0, cmap(j))),
            pl.BlockSpec((1, 2 * HEADS), full),
            pl.BlockSpec((2 * HEADS, 1), full),
            pl.BlockSpec((HEADS, D_INNER), full)]


def _ssd_bwd(xs, cm, bt, dt, dtT, arow, acol, rexp):
    cmap = lambda j: jnp.where(j < CTX_CHUNKS, CTX_CHUNKS - 1 - j, NCHUNK + CTX_CHUNKS - 1 - j)
    omap = lambda b, j: (b, NCHUNK - 1 - jnp.maximum(j, CTX_CHUNKS), 0)
    return pl.pallas_call(
        _ssd_bwd_kernel,
        grid=(BATCH // NB, NCHUNK),
        in_specs=_ssd_specs(cmap),
        out_specs=pl.BlockSpec((NB, Q, D_INNER), omap),
        out_shape=jax.ShapeDtypeStruct((BATCH, SEQ, D_INNER), BF16),
        scratch_shapes=[pltpu.VMEM((NB, GROUPS, NSTATE, HPG * HEADDIM), F32)],
        compiler_params=_cp(("arbitrary", "arbitrary")),
        name="ssd_bwd",
    )(xs, cm, bt, dt, dtT, arow, acol, rexp)


def _ssd_fwd(xs, cm, bt, dt, dtT, arow, acol, rexp, z, yb, dsk, nw):
    cmap = lambda j: j
    lat = lambda b, j: (b, jnp.maximum(j - CTX_CHUNKS, 0), 0)
    full = lambda b, j: (0, 0)
    return pl.pallas_call(
        _ssd_fwd_kernel,
        grid=(BATCH // NB, NCHUNK),
        in_specs=_ssd_specs(cmap) + [
            pl.BlockSpec((NB, Q, D_INNER), lambda b, j: (b, j, 0)),
            pl.BlockSpec((NB, Q, D_INNER), lat),
            pl.BlockSpec((1, D_INNER), full),
            pl.BlockSpec((1, D_INNER), full)],
        out_specs=pl.BlockSpec((NB, Q, D_INNER), lat),
        out_shape=jax.ShapeDtypeStruct((BATCH, SEQ, D_INNER), BF16),
        scratch_shapes=[pltpu.VMEM((NB, GROUPS, NSTATE, HPG * HEADDIM), F32)],
        compiler_params=_cp(("arbitrary", "arbitrary")),
        name="ssd_fwd",
    )(xs, cm, bt, dt, dtT, arow, acol, rexp, z, yb, dsk, nw)


def _route(h, rwT_ref, rb_ref, cnt_ref, eidx_ref, pos_ref, gate_ref):
    hh = h.astype(BF16)
    hl = (h - hh.astype(F32)).astype(BF16)
    w = rwT_ref[...]
    wh = w.astype(BF16)
    wl = (w - wh.astype(F32)).astype(BF16)
    logits = _dot_nt(wh, hh) + _dot_nt(wh, hl) + _dot_nt(wl, hh)
    scores = jax.nn.sigmoid(logits)
    sel = scores + rb_ref[...]
    per = E // NGRP
    sub = lax.broadcasted_iota(I32, (per, TM), 0)
    gscore = []
    for g in range(NGRP):
        blk = sel[g * per:(g + 1) * per, :]
        m1 = jnp.max(blk, axis=0, keepdims=True)
        first = jnp.min(jnp.where(blk == m1, sub, per), axis=0, keepdims=True)
        m2 = jnp.max(jnp.where(sub == first, -jnp.inf, blk), axis=0, keepdims=True)
        gscore.append(m1 + m2)
    masked = []
    for g in range(NGRP):
        rank = jnp.zeros((1, TM), F32)
        for o in range(NGRP):
            if o == g:
                continue
            ahead = (gscore[o] >= gscore[g]) if o < g else (gscore[o] > gscore[g])
            rank = rank + jnp.where(ahead, 1.0, 0.0)
        blk = sel[g * per:(g + 1) * per, :]
        masked.append(jnp.where(rank < TOPG, blk, -jnp.inf))
    v = jnp.concatenate(masked, axis=0)
    eio = lax.broadcasted_iota(I32, (E, TM), 0)
    kio = lax.broadcasted_iota(I32, (TOPK, TM), 0)
    eidx = jnp.zeros((TOPK, TM), I32)
    gsc = jnp.zeros((TOPK, TM), F32)
    hot = jnp.zeros((E, TM), F32)
    picks = []
    for k in range(TOPK):
        m = jnp.max(v, axis=0, keepdims=True)
        first = jnp.min(jnp.where(v == m, eio, E), axis=0, keepdims=True)
        pick = eio == first
        sc = jnp.sum(jnp.where(pick, scores, 0.0), axis=0, keepdims=True)
        eidx = jnp.where(kio == k, first, eidx)
        gsc = jnp.where(kio == k, sc, gsc)
        hot = jnp.where(pick, 1.0, hot)
        v = jnp.where(pick, -jnp.inf, v)
        picks.append(pick)
    gate = gsc / jnp.sum(gsc, axis=0, keepdims=True) * ROUTED_SCALE
    ti = lax.broadcasted_iota(I32, (TM, TM), 0)
    tj = lax.broadcasted_iota(I32, (TM, TM), 1)
    before = jnp.where(ti < tj, 1.0, 0.0).astype(BF16)
    posfull = _dot(hot.astype(BF16), before) + cnt_ref[...]
    pos = jnp.zeros((TOPK, TM), F32)
    for k in range(TOPK):
        pk = jnp.sum(jnp.where(picks[k], posfull, 0.0), axis=0, keepdims=True)
        pos = jnp.where(kio == k, pk, pos)
    cnt_ref[...] = cnt_ref[...] + jnp.sum(hot, axis=1, keepdims=True)
    eidx_ref[...] = eidx
    pos_ref[...] = pos.astype(I32)
    eye = jnp.where(ti == tj, 1.0, 0.0).astype(BF16)
    g1, g2, g3 = _split3(gate)
    gate_ref[...] = _dot_nt(eye, g1) + _dot_nt(eye, g2) + _dot_nt(eye, g3)


def _mix_epilogue(y, x_ref, mod_ref, gpost_ref, gpre_ref, rwT_ref, rb_ref,
                  x1_ref, h2_ref, eidx_ref, pos_ref, gate_ref, cnt_out_ref, cnt_ref):
    first = (pl.program_id(0) == 0) & (pl.program_id(1) == 0)

    @pl.when(first)
    def _():
        cnt_ref[...] = jnp.zeros_like(cnt_ref)
    x1 = x_ref[...] + mod_ref[2:3, :] * _rms(y, gpost_ref[...])
    x1_ref[...] = x1
    h2 = _rms(x1, gpre_ref[...]) * (1.0 + mod_ref[4:5, :]) + mod_ref[3:4, :]
    _store_packed(h2_ref, h2)
    _route(h2, rwT_ref, rb_ref, cnt_ref, eidx_ref, pos_ref, gate_ref)
    cnt_out_ref[...] = cnt_ref[...]


def _ssm_out_kernel(y_ref, w_ref, x_ref, mod_ref, gpost_ref, gpre_ref, rwT_ref, rb_ref,
                    x1_ref, h2_ref, eidx_ref, pos_ref, gate_ref, cnt_out_ref, cnt_ref):
    y = _dot(y_ref[...], w_ref[...])
    _mix_epilogue(y, x_ref, mod_ref, gpost_ref, gpre_ref, rwT_ref, rb_ref,
                  x1_ref, h2_ref, eidx_ref, pos_ref, gate_ref, cnt_out_ref, cnt_ref)


_CW = 256
_PAD = 16


def _conf_out_kernel(u_ref, dww_ref, dwb_ref, lng_ref, lnb_ref, w_ref, b_ref,
                     x_ref, mod_ref, gpost_ref, gpre_ref, rwT_ref, rb_ref,
                     x1_ref, h2_ref, eidx_ref, pos_ref, gate_ref, cnt_out_ref,
                     cnt_ref, v_ref):
    nrow = TM // GRID_W
    ext_rows = GRID_W + 2 * _PAD
    first = _PAD - CONF_K // 2
    span = GRID_W + 8 * ((CONF_K - 1) // 8)
    assert first + 7 + span <= ext_rows
    ri = lax.broadcasted_iota(I32, (span, ext_rows), 0)
    ji = lax.broadcasted_iota(I32, (span, ext_rows), 1)
    shift = [jnp.where(ji == ri + first + b, 1.0, 0.0).astype(BF16) for b in range(8)]
    zpad = jnp.zeros((_PAD, D), BF16)
    for r in range(nrow):
        ext = jnp.concatenate([zpad, u_ref[r * GRID_W:(r + 1) * GRID_W, :], zpad], axis=0)
        for c in range(D // _CW):
            cs = slice(c * _CW, (c + 1) * _CW)
            acc = jnp.broadcast_to(dwb_ref[:, cs], (GRID_W, _CW))
            for b in range(8):
                win = _dot(shift[b], ext[:, cs])
                for a in range(-(-CONF_K // 8)):
                    k = 8 * a + b
                    if k < CONF_K:
                        wk = jnp.concatenate([dww_ref[8 * k:8 * k + 8, cs]] * (GRID_W // 8), axis=0)
                        acc = acc + wk * win[8 * a:8 * a + GRID_W, :]
            v_ref[r * GRID_W:(r + 1) * GRID_W, cs] = acc
    v = v_ref[...]
    mu = jnp.mean(v, axis=-1, keepdims=True)
    vc = v - mu
    ln = vc * lax.rsqrt(jnp.mean(vc * vc, axis=-1, keepdims=True) + EPS) * lng_ref[...] + lnb_ref[...]
    y = _dot(_silu(ln).astype(BF16), w_ref[...]) + b_ref[...]
    _mix_epilogue(y, x_ref, mod_ref, gpost_ref, gpre_ref, rwT_ref, rb_ref,
                  x1_ref, h2_ref, eidx_ref, pos_ref, gate_ref, cnt_out_ref, cnt_ref)


def _mix_out_common_specs():
    full = lambda b, i: (0, 0)
    nt = SEQ // TM
    in_specs = [pl.BlockSpec((None, TM, D), lambda b, i: (b, i, 0)),
                pl.BlockSpec((None, 6, D), lambda b, i: (b, 0, 0)),
                pl.BlockSpec((1, D), full), pl.BlockSpec((1, D), full),
                pl.BlockSpec((E, D), full), pl.BlockSpec((E, 1), full)]
    out_specs = [pl.BlockSpec((None, TM, D), lambda b, i: (b, i, 0)),
                 pl.BlockSpec((TM * RW, 128), lambda b, i: (b * nt + i, 0)),
                 pl.BlockSpec((TOPK, TM), lambda b, i: (0, b * nt + i)),
                 pl.BlockSpec((TOPK, TM), lambda b, i: (0, b * nt + i)),
                 pl.BlockSpec((TM, TOPK), lambda b, i: (b * nt + i, 0)),
                 pl.BlockSpec((E, 1), full)]
    out_shape = [jax.ShapeDtypeStruct((BATCH, SEQ, D), F32),
                 jax.ShapeDtypeStruct((T * RW, 128), U32),
                 jax.ShapeDtypeStruct((TOPK, T), I32),
                 jax.ShapeDtypeStruct((TOPK, T), I32),
                 jax.ShapeDtypeStruct((T, TOPK), F32),
                 jax.ShapeDtypeStruct((E, 1), F32)]
    return in_specs, out_specs, out_shape


def _ssm_out(y, w, x, mod, gpost, gpre, rwT, rb):
    common_in, out_specs, out_shape = _mix_out_common_specs()
    return pl.pallas_call(
        _ssm_out_kernel,
        grid=(BATCH, SEQ // TM),
        in_specs=[pl.BlockSpec((None, TM, D_INNER), lambda b, i: (b, i, 0)),
                  pl.BlockSpec((D_INNER, D), lambda b, i: (0, 0))] + common_in,
        out_specs=out_specs, out_shape=out_shape,
        scratch_shapes=[pltpu.VMEM((E, 1), F32)],
        compiler_params=_cp(("arbitrary", "arbitrary")),
        name="ssm_out",
    )(y, w, x, mod, gpost, gpre, rwT, rb)


def _conf_out(u, dww, dwb, lng, lnb, w, b, x, mod, gpost, gpre, rwT, rb):
    common_in, out_specs, out_shape = _mix_out_common_specs()
    full = lambda b_, i: (0, 0)
    return pl.pallas_call(
        _conf_out_kernel,
        grid=(BATCH, SEQ // TM),
        in_specs=[pl.BlockSpec((None, TM, D), lambda b_, i: (b_, i, 0)),
                  pl.BlockSpec((CONF_K * 8, D), full), pl.BlockSpec((1, D), full),
                  pl.BlockSpec((1, D), full), pl.BlockSpec((1, D), full),
                  pl.BlockSpec((D, D), full), pl.BlockSpec((1, D), full)] + common_in,
        out_specs=out_specs, out_shape=out_shape,
        scratch_shapes=[pltpu.VMEM((E, 1), F32), pltpu.VMEM((TM, D), F32)],
        compiler_params=_cp(("arbitrary", "arbitrary")),
        name="conf_out",
    )(u, dww, dwb, lng, lnb, w, b, x, mod, gpost, gpre, rwT, rb)


def _conf_in_kernel(x_ref, mod_ref, g_ref, wa_ref, wg_ref, ba_ref, bg_ref, u_ref):
    h = _rms(x_ref[...], g_ref[...]) * (1.0 + mod_ref[1:2, :]) + mod_ref[0:1, :]
    hb = h.astype(BF16)
    a = _dot(hb, wa_ref[...]) + ba_ref[...]
    g = _dot(hb, wg_ref[...]) + bg_ref[...]
    u_ref[...] = (a * jax.nn.sigmoid(g)).astype(BF16)


def _conf_in(x, mod, gain, w, bias):
    full = lambda b, i: (0, 0)
    return pl.pallas_call(
        _conf_in_kernel,
        grid=(BATCH, SEQ // TM),
        in_specs=[pl.BlockSpec((None, TM, D), lambda b, i: (b, i, 0)),
                  pl.BlockSpec((None, 6, D), lambda b, i: (b, 0, 0)),
                  pl.BlockSpec((1, D), full),
                  pl.BlockSpec((D, D), full), pl.BlockSpec((D, D), lambda b, i: (0, 1)),
                  pl.BlockSpec((1, D), full), pl.BlockSpec((1, D), lambda b, i: (0, 1))],
        out_specs=pl.BlockSpec((None, TM, D), lambda b, i: (b, i, 0)),
        out_shape=jax.ShapeDtypeStruct((BATCH, SEQ, D), BF16),
        compiler_params=_cp(("arbitrary", "arbitrary")),
        name="conf_in",
    )(x, mod, gain, w, w, bias, bias)


def _row_copy(src, si, dst, di, sem):
    return pltpu.make_async_copy(src.at[pl.ds(pl.multiple_of(si * RW, RW), RW), :],
                                 dst.at[pl.ds(pl.multiple_of(di * RW, RW), RW), :], sem)


ZROWS = BLK + 8
_TU = 4


def _dispatch_kernel(zs_ref, na_ref, dest_ref, h_ref, xs_ref, zero_ref, zsem, sem):
    @pl.when(pl.program_id(0) == 0)
    def _():
        zero_ref[...] = jnp.zeros_like(zero_ref)

        def zcopy(e):
            start = pl.multiple_of(zs_ref[e] * RW, 8 * RW)
            return pltpu.make_async_copy(zero_ref, xs_ref.at[pl.ds(start, ZROWS * RW), :], zsem)

        def zstart(e, c):
            zcopy(e).start()
            return c

        def zwait(e, c):
            zcopy(e).wait()
            return c
        lax.fori_loop(0, E, zstart, 0)
        lax.fori_loop(0, E, zwait, 0)

        ntail = NBLK + 1 - na_ref[0]

        def tcopy(c):
            start = pl.multiple_of((na_ref[0] + c) * (BLK * RW), BLK * RW)
            return pltpu.make_async_copy(zero_ref.at[pl.ds(0, BLK * RW), :],
                                         xs_ref.at[pl.ds(start, BLK * RW), :], zsem)

        def tstart(c, carry):
            tcopy(c).start()
            return carry

        def twait(c, carry):
            tcopy(c).wait()
            return carry
        lax.fori_loop(0, ntail, tstart, 0)
        lax.fori_loop(0, ntail, twait, 0)
        last = pltpu.make_async_copy(zero_ref.at[pl.ds(0, 8 * RW), :],
                                     xs_ref.at[pl.ds((PROWS - 8) * RW, 8 * RW), :], zsem)
        last.start()
        last.wait()

    def issue(tb, c):
        for u in range(_TU):
            t = tb * _TU + u
            for k in range(TOPK):
                _row_copy(h_ref, t, xs_ref, dest_ref[t * TOPK + k], sem).start(priority=k % 2)
        return c

    lax.fori_loop(0, TM // _TU, issue, 0)
    for k in range(TOPK):
        pltpu.make_async_copy(h_ref, xs_ref.at[pl.ds(0, TM * RW), :], sem).wait()


def _dispatch(zstart, nact, dest, hp):
    grid_spec = pltpu.PrefetchScalarGridSpec(
        num_scalar_prefetch=2,
        grid=(T // TM,),
        in_specs=[pl.BlockSpec((TM * TOPK,), lambda i, zs, na: (i,), memory_space=pltpu.SMEM),
                  pl.BlockSpec((TM * RW, 128), lambda i, zs, na: (i, 0))],
        out_specs=pl.BlockSpec(memory_space=pl.ANY),
        scratch_shapes=[pltpu.VMEM((ZROWS * RW, 128), U32), pltpu.SemaphoreType.DMA(()),
                        pltpu.SemaphoreType.DMA(())])
    return pl.pallas_call(
        _dispatch_kernel,
        grid_spec=grid_spec,
        out_shape=jax.ShapeDtypeStruct((PROWS * RW, 128), U32),
        compiler_params=_cp(("arbitrary",)),
        name="moe_dispatch",
    )(zstart, nact, dest, hp)


XRING = 3


def _experts_kernel(be_ref, na_ref, x_hbm, wg_ref, wu_ref, wd_ref, y_ref, wgb_ref, wub_ref, wdb_ref,
                    xbuf_ref, xsem):
    i = pl.program_id(0)
    na = na_ref[0]

    def fetch(j):
        rows = BLK * RW
        src = x_hbm.at[pl.ds(pl.multiple_of(j * rows, rows), rows), :]
        return pltpu.make_async_copy(src, xbuf_ref.at[j % XRING], xsem.at[j % XRING])

    @pl.when(i == 0)
    def _():
        for j in range(XRING - 1):
            @pl.when(j < na)
            def _():
                fetch(j).start()

    @pl.when(i + XRING - 1 < na)
    def _():
        fetch(i + XRING - 1).start()

    @pl.when((i == 0) | (be_ref[i] != be_ref[jnp.maximum(i - 1, 0)]))
    def _():
        wgb_ref[...] = wg_ref[...].astype(BF16)
        wub_ref[...] = wu_ref[...].astype(BF16)
        wdb_ref[...] = wd_ref[...].astype(BF16)

    @pl.when(i < na)
    def _():
        fetch(i).wait()
        xb = _load_packed(xbuf_ref.at[i % XRING], BLK)
        g = _dot(xb, wgb_ref[...])
        u = _dot(xb, wub_ref[...])
        a = (_silu(g) * u).astype(BF16)
        _store_packed(y_ref, _dot(a, wdb_ref[...]))

    @pl.when(i >= na)
    def _():
        y_ref[...] = jnp.zeros_like(y_ref)


def _experts(layer, block_e, nact, xs, wg, wu, wd):
    wmap = lambda i, be, na: (layer, be[i], 0, 0)
    grid_spec = pltpu.PrefetchScalarGridSpec(
        num_scalar_prefetch=2,
        grid=(NBLK,),
        in_specs=[pl.BlockSpec(memory_space=pl.ANY),
                  pl.BlockSpec((None, None, D, DE), wmap),
                  pl.BlockSpec((None, None, D, DE), wmap),
                  pl.BlockSpec((None, None, DE, D), wmap)],
        out_specs=pl.BlockSpec((BLK * RW, 128), lambda i, be, na: (i, 0)),
        scratch_shapes=[pltpu.VMEM((D, DE), BF16), pltpu.VMEM((D, DE), BF16),
                        pltpu.VMEM((DE, D), BF16),
                        pltpu.VMEM((XRING, BLK * RW, 128), U32), pltpu.SemaphoreType.DMA((XRING,))])
    return pl.pallas_call(
        _experts_kernel,
        grid_spec=grid_spec,
        out_shape=jax.ShapeDtypeStruct((NBLK * BLK * RW, 128), U32),
        compiler_params=_cp(("arbitrary",)),
        name="moe_experts",
    )(block_e, nact, xs, wg, wu, wd)


def _combine_kernel(dcur_ref, dnxt_ref, gate_ref, h_ref, x1_ref, mod_ref, gpost_ref,
                    sg_ref, su_ref, sd_ref, ys_ref, o_ref, buf_ref, sems):
    i = pl.program_id(0)
    n = pl.num_programs(0)

    def gather(dest_ref, slot):
        def body(tb, c):
            for u in range(_TU):
                t = tb * _TU + u
                for k in range(TOPK):
                    _row_copy(ys_ref, dest_ref[t * TOPK + k], buf_ref.at[slot, k], t,
                              sems.at[slot]).start(priority=k % 2)
            return c
        lax.fori_loop(0, TM // _TU, body, 0)

    @pl.when(i == 0)
    def _():
        gather(dcur_ref, 0)

    for slot_next in range(2):
        @pl.when((i + 1 < n) & ((i + 1) % 2 == slot_next))
        def _():
            gather(dnxt_ref, slot_next)

    hb = _load_packed(h_ref, TM)
    a = (_silu(_dot(hb, sg_ref[...])) * _dot(hb, su_ref[...])).astype(BF16)
    f = _dot(a, sd_ref[...])

    slot = i % 2

    for k in range(TOPK):
        pltpu.make_async_copy(ys_ref.at[pl.ds(0, TM * RW), :], buf_ref.at[slot, k],
                              sems.at[slot]).wait()

    gate = gate_ref[...]
    gb = [jnp.broadcast_to(gate[:, k:k + 1], (TM, 128)) for k in range(TOPK)]
    los, his = [], []
    for s in range(RW):
        lo = f[:, s * 128:(s + 1) * 128]
        hi = f[:, D // 2 + s * 128:D // 2 + (s + 1) * 128]
        for k in range(TOPK):
            wl, wh = _unpack2(buf_ref[slot, k, pl.ds(s, TM, stride=RW), :])
            lo = lo + gb[k] * wl
            hi = hi + gb[k] * wh
        los.append(lo)
        his.append(hi)
    f = jnp.concatenate(los + his, axis=1)
    o_ref[...] = x1_ref[...] + mod_ref[5:6, :] * _rms(f, gpost_ref[...])


def _combine(dest, gate, hp, x1, mod, gpost, sg, su, sd, ys):
    nt = SEQ // TM
    n = T // TM
    full = lambda i: (0, 0)
    return pl.pallas_call(
        _combine_kernel,
        grid=(n,),
        in_specs=[pl.BlockSpec((TM * TOPK,), lambda i: (i,), memory_space=pltpu.SMEM),
                  pl.BlockSpec((TM * TOPK,), lambda i: (jnp.minimum(i + 1, n - 1),),
                               memory_space=pltpu.SMEM),
                  pl.BlockSpec((TM, TOPK), lambda i: (i, 0)),
                  pl.BlockSpec((TM * RW, 128), lambda i: (i, 0)),
                  pl.BlockSpec((TM, D), lambda i: (i, 0)),
                  pl.BlockSpec((None, 6, D), lambda i: (i // nt, 0, 0)),
                  pl.BlockSpec((1, D), full),
                  pl.BlockSpec((D, DE), full), pl.BlockSpec((D, DE), full), pl.BlockSpec((DE, D), full),
                  pl.BlockSpec(memory_space=pl.ANY)],
        out_specs=pl.BlockSpec((TM, D), lambda i: (i, 0)),
        out_shape=jax.ShapeDtypeStruct((T, D), F32),
        scratch_shapes=[pltpu.VMEM((2, TOPK, TM * RW, 128), U32), pltpu.SemaphoreType.DMA((2,))],
        compiler_params=_cp(("arbitrary",)),
        name="moe_combine",
    )(dest, dest, gate, hp, x1.reshape(T, D), mod, gpost, sg, su, sd, ys).reshape(BATCH, SEQ, D)


def _moe(layer, x1, hp, eidx, pos, gate, counts, mod, gpost, wg, wu, wd, sg, su, sd):
    cnt = counts.reshape(E).astype(I32)
    padded = (cnt + BLK - 1) // BLK * BLK
    pend = jnp.cumsum(padded)
    pstart = pend - padded
    ids = jnp.arange(E, dtype=I32)
    dest = pos + jnp.sum(jnp.where(eidx[:, :, None] == ids, pstart, 0), axis=-1)
    dest = dest.T.reshape(T * TOPK)
    nact = (pend[-1] // BLK).reshape(1).astype(I32)
    blk0 = jnp.arange(NBLK, dtype=I32) * BLK
    block_e = jnp.sum((pend[None, :] <= blk0[:, None]).astype(I32), axis=1)
    block_e = jnp.minimum(block_e, E - 1).astype(I32)
    xs = _dispatch(((pstart + cnt) // 8 * 8).astype(I32), nact, dest, hp)
    ys = _experts(layer, block_e, nact, xs, wg, wu, wd)
    return _combine(dest, gate, hp, x1, mod, gpost, sg.astype(BF16), su.astype(BF16),
                    sd.astype(BF16), ys)


def kernel(x, c, ctx, c_ctx, ada_w, ada_b, norm_mix_pre, norm_mix_post, norm_ffn_pre, norm_ffn_post, ssm_w_in, ssm_conv_w, ssm_conv_b, ssm_dt_bias, ssm_a_log, ssm_d, ssm_norm, ssm_w_out, cv_w_in, cv_b_in, cv_dw_w, cv_dw_b, cv_ln_g, cv_ln_b, cv_w_out, cv_b_out, router_w, router_b, exp_w_gate, exp_w_up, exp_w_down, sh_w_gate, sh_w_up, sh_w_down):
    row = lambda v: v.reshape(1, -1)
    cvec = jnp.concatenate([c, c_ctx[None, :], jnp.zeros((3, D), F32)], axis=0)
    mod = _ada(cvec, ada_w, ada_b).reshape(2, 8, 6, D)

    w_in = ssm_w_in[0].astype(BF16)
    dtb = ssm_dt_bias[0].reshape(1, 2 * HEADS)
    wdt = w_in[:, D_INNER + CONV_DIM:]
    z, xbc, dt, dtT = _ssm_in(ctx, x, mod[0], row(norm_mix_pre[0]), w_in, wdt, wdt.T,
                              dtb, dtb.reshape(2 * HEADS, 1))
    xs, bt, cm = _ssm_conv(xbc, jnp.repeat(ssm_conv_w[0], 8, axis=0), row(ssm_conv_b[0]))
    a = -jnp.exp(ssm_a_log[0].astype(F32)).reshape(1, 2 * HEADS)
    rexp = (jnp.arange(D_INNER)[None, :] // HEADDIM == jnp.arange(HEADS)[:, None]).astype(BF16)
    yb = _ssd_bwd(xs, cm, bt, dt, dtT, a, a.reshape(2 * HEADS, 1), rexp)
    dsk = jnp.repeat(ssm_d[0], HEADDIM).reshape(1, D_INNER)
    ygn = _ssd_fwd(xs, cm, bt, dt, dtT, a, a.reshape(2 * HEADS, 1), rexp, z, yb, dsk,
                   row(ssm_norm[0]))
    x1, h2, eidx, pos, gate, counts = _ssm_out(
        ygn, ssm_w_out[0].astype(BF16), x, mod[0], row(norm_mix_post[0]), row(norm_ffn_pre[0]),
        router_w[0].T, router_b[0].reshape(E, 1))
    x2 = _moe(0, x1, h2, eidx, pos, gate, counts, mod[0], row(norm_ffn_post[0]),
              exp_w_gate, exp_w_up, exp_w_down, sh_w_gate[0], sh_w_up[0], sh_w_down[0])

    u = _conf_in(x2, mod[1], row(norm_mix_pre[1]), cv_w_in[0].astype(BF16), row(cv_b_in[0]))
    x3, h4, eidx, pos, gate, counts = _conf_out(
        u, jnp.repeat(cv_dw_w[0], 8, axis=0), row(cv_dw_b[0]), row(cv_ln_g[0]), row(cv_ln_b[0]),
        cv_w_out[0].astype(BF16), row(cv_b_out[0]),
        x2, mod[1], row(norm_mix_post[1]), row(norm_ffn_pre[1]),
        router_w[1].T, router_b[1].reshape(E, 1))
    return _moe(1, x3, h4, eidx, pos, gate, counts, mod[1], row(norm_ffn_post[1]),
                exp_w_gate, exp_w_up, exp_w_down, sh_w_gate[1], sh_w_up[1], sh_w_down[1])
```

```python
import functools

import jax
import jax.numpy as jnp
from jax import lax
from jax.experimental import pallas as pl
from jax.experimental.pallas import tpu as pltpu

F32 = jnp.float32
BF16 = jnp.bfloat16
I32 = jnp.int32

D = 1024
BATCH = 4
SEQ = 4096
CTX = 256
LTOT = CTX + SEQ
GRID_W = 64

D_INNER = 2048
HEADS = 32
GROUPS = 8
HPG = 4
HEADDIM = 64
NSTATE = 128
Q = 128
NCHUNK = LTOT // Q
CTX_CHUNKS = CTX // Q
CONV_DIM = D_INNER + 2 * GROUPS * NSTATE
SSM_K = 5
CONF_K = 31

E = 64
TOPK = 8
NGRP = 8
TOPG = 4
DE = 256
ROUTED_SCALE = 2.5
EPS = 1e-6

T = BATCH * SEQ
TM = 512
BLK = 512
NBLK = -(-(T * TOPK + E * (BLK - 1)) // BLK)
PROWS = (NBLK + 1) * BLK + 8

VMEM_LIMIT = 56 * 1024 * 1024
NEG = -1e30


def _cp(sem):
    return pltpu.CompilerParams(dimension_semantics=sem, vmem_limit_bytes=VMEM_LIMIT)


def _silu(v):
    return v * jax.nn.sigmoid(v)


def _rms(v, g):
    return v * lax.rsqrt(jnp.mean(v * v, axis=-1, keepdims=True) + EPS) * g


def _split3(v):
    a = v.astype(BF16)
    r = v - a.astype(F32)
    b = r.astype(BF16)
    c = (r - b.astype(F32)).astype(BF16)
    return a, b, c


def _dot(a, b):
    return jnp.dot(a, b, preferred_element_type=F32)


def _dot_nt(a, b):
    return lax.dot_general(a, b, (((1,), (1,)), ((), ())), preferred_element_type=F32)


U32 = jnp.uint32
RW = D // 2 // 128
_HI = 0xFFFF0000


def _pack2(lo, hi):
    ul = pltpu.bitcast(lo.astype(BF16).astype(F32), U32)
    uh = pltpu.bitcast(hi.astype(BF16).astype(F32), U32)
    return (ul >> 16) | (uh & U32(_HI))


def _unpack2(w):
    return pltpu.bitcast(w << 16, F32), pltpu.bitcast(w & U32(_HI), F32)


def _store_packed(ref, v):
    m = v.shape[0]
    for s in range(RW):
        lo = v[:, s * 128:(s + 1) * 128]
        hi = v[:, D // 2 + s * 128:D // 2 + (s + 1) * 128]
        ref[pl.ds(s, m, stride=RW), :] = _pack2(lo, hi)


def _load_packed(ref, m):
    los, his = [], []
    for s in range(RW):
        lo, hi = _unpack2(ref[pl.ds(s, m, stride=RW), :])
        los.append(lo.astype(BF16))
        his.append(hi.astype(BF16))
    return jnp.concatenate(los + his, axis=1)


def _ada_kernel(c_ref, w_ref, b_ref, o_ref):
    s = _silu(c_ref[...])
    o_ref[...] = jnp.dot(s, w_ref[...], preferred_element_type=F32,
                         precision=lax.Precision.HIGHEST) + b_ref[...]


def _ada(cvec, ada_w, ada_b):
    depth = ada_w.shape[0]
    tn = 1536
    return pl.pallas_call(
        _ada_kernel,
        grid=(depth, 6 * D // tn),
        in_specs=[pl.BlockSpec((8, D), lambda l, j: (0, 0)),
                  pl.BlockSpec((None, D, tn), lambda l, j: (l, 0, j)),
                  pl.BlockSpec((None, 1, tn), lambda l, j: (l, 0, j))],
        out_specs=pl.BlockSpec((None, 8, tn), lambda l, j: (l, 0, j)),
        out_shape=jax.ShapeDtypeStruct((depth, 8, 6 * D), F32),
        compiler_params=_cp(("arbitrary", "arbitrary")),
        name="ada",
    )(cvec, ada_w, ada_b.reshape(depth, 1, 6 * D))


def _ssm_in_kernel(c_ref, x_ref, mod_ref, g_ref, wz_ref, wx0_ref, wx1_ref, wdt_ref, wdtT_ref, dtb_ref,
                   dtbT_ref, z_ref, xbc_ref, dt_ref, dtT_ref):
    xin = jnp.where(pl.program_id(1) == 0, c_ref[...], x_ref[...])
    h = _rms(xin, g_ref[...]) * (1.0 + mod_ref[1:2, :]) + mod_ref[0:1, :]
    hb = h.astype(BF16)
    z_ref[...] = _dot(hb, wz_ref[...]).astype(BF16)
    xbc_ref[:, :D_INNER] = _dot(hb, wx0_ref[...]).astype(BF16)
    xbc_ref[:, D_INNER:] = _dot(hb, wx1_ref[...]).astype(BF16)
    dt_ref[...] = jax.nn.softplus(_dot(hb, wdt_ref[...]) + dtb_ref[...])
    dtT_ref[...] = jax.nn.softplus(_dot_nt(wdtT_ref[...], hb) + dtbT_ref[...])


def _ssm_in(ctx, x, mod, gain, w_in, wdt, wdtT, dtb, dtbT):
    ts = CTX
    nt = LTOT // ts
    full = lambda b, i: (0, 0)
    assert CONV_DIM == 2 * D_INNER and SEQ % ts == 0
    return pl.pallas_call(
        _ssm_in_kernel,
        grid=(BATCH, nt),
        in_specs=[pl.BlockSpec((None, ts, D), lambda b, i: (b, 0, 0)),
                  pl.BlockSpec((None, ts, D), lambda b, i: (b, jnp.maximum(i - 1, 0), 0)),
                  pl.BlockSpec((None, 6, D), lambda b, i: (jnp.where(i == 0, BATCH, b), 0, 0)),
                  pl.BlockSpec((1, D), full),
                  pl.BlockSpec((D, D_INNER), lambda b, i: (0, 0)),
                  pl.BlockSpec((D, D_INNER), lambda b, i: (0, 1)),
                  pl.BlockSpec((D, D_INNER), lambda b, i: (0, 2)),
                  pl.BlockSpec((D, 2 * HEADS), full),
                  pl.BlockSpec((2 * HEADS, D), full),
                  pl.BlockSpec((1, 2 * HEADS), full),
                  pl.BlockSpec((2 * HEADS, 1), full)],
        out_specs=[pl.BlockSpec((None, ts, D_INNER), lambda b, i: (b, i, 0)),
                   pl.BlockSpec((None, ts, CONV_DIM), lambda b, i: (b, i, 0)),
                   pl.BlockSpec((None, ts, 2 * HEADS), lambda b, i: (b, i, 0)),
                   pl.BlockSpec((None, 2 * HEADS, ts), lambda b, i: (b, 0, i))],
        out_shape=[jax.ShapeDtypeStruct((BATCH, LTOT, D_INNER), BF16),
                   jax.ShapeDtypeStruct((BATCH, LTOT, CONV_DIM), BF16),
                   jax.ShapeDtypeStruct((BATCH, LTOT, 2 * HEADS), F32),
                   jax.ShapeDtypeStruct((BATCH, 2 * HEADS, LTOT), F32)],
        compiler_params=_cp(("arbitrary", "arbitrary")),
        name="ssm_in",
    )(ctx, x, mod, gain, w_in, w_in, w_in, wdt, wdtT, dtb, dtbT)


_HALO = 16
_CC = 512


def _ssm_conv_kernel(x_ref, xp_ref, xn_ref, w_ref, b_ref, xs_ref, bt_ref, cm_ref):
    i = pl.program_id(1)
    first_lat = CTX // Q
    pvalid = (i != 0) & (i != first_lat)
    nvalid = (i != first_lat - 1) & (i != NCHUNK - 1)
    li = lax.broadcasted_iota(I32, (Q, Q + 2 * _HALO), 0)
    ji = lax.broadcasted_iota(I32, (Q, Q + 2 * _HALO), 1)
    shift = {k: jnp.where(ji == li + _HALO + k - SSM_K // 2, 1.0, 0.0).astype(BF16)
             for k in range(SSM_K) if k != SSM_K // 2}
    for c in range(CONV_DIM // _CC):
        cs = slice(c * _CC, (c + 1) * _CC)
        cur = x_ref[:, cs]
        prev = jnp.where(pvalid, xp_ref[:, cs], jnp.zeros((_HALO, _CC), BF16))
        nxt = jnp.where(nvalid, xn_ref[:, cs], jnp.zeros((_HALO, _CC), BF16))
        ext = jnp.concatenate([prev, cur, nxt], axis=0)
        acc = jnp.broadcast_to(b_ref[:, cs], (Q, _CC))
        for k in range(SSM_K):
            wk = jnp.concatenate([w_ref[8 * k:8 * k + 8, cs]] * (Q // 8), axis=0)
            tap = cur.astype(F32) if k == SSM_K // 2 else _dot(shift[k], ext)
            acc = acc + wk * tap
        y = _silu(acc)
        lo = c * _CC
        if lo < D_INNER:
            xs_ref[:, cs] = y.astype(BF16)
        elif lo < D_INNER + GROUPS * NSTATE:
            o = lo - D_INNER
            bt_ref[o:o + _CC, :] = y.T.astype(BF16)
        else:
            o = lo - D_INNER - GROUPS * NSTATE
            cm_ref[:, o:o + _CC] = y.astype(BF16)


def _ssm_conv(xbc, w, b):
    nh = Q // _HALO
    last = LTOT // _HALO - 1
    gn = GROUPS * NSTATE
    return pl.pallas_call(
        _ssm_conv_kernel,
        grid=(BATCH, NCHUNK),
        in_specs=[pl.BlockSpec((None, Q, CONV_DIM), lambda b, i: (b, i, 0)),
                  pl.BlockSpec((None, _HALO, CONV_DIM), lambda b, i: (b, jnp.maximum(i * nh - 1, 0), 0)),
                  pl.BlockSpec((None, _HALO, CONV_DIM), lambda b, i: (b, jnp.minimum(i * nh + nh, last), 0)),
                  pl.BlockSpec((SSM_K * 8, CONV_DIM), lambda b, i: (0, 0)),
                  pl.BlockSpec((1, CONV_DIM), lambda b, i: (0, 0))],
        out_specs=[pl.BlockSpec((None, Q, D_INNER), lambda b, i: (b, i, 0)),
                   pl.BlockSpec((None, gn, Q), lambda b, i: (b, 0, i)),
                   pl.BlockSpec((None, Q, gn), lambda b, i: (b, i, 0))],
        out_shape=[jax.ShapeDtypeStruct((BATCH, LTOT, D_INNER), BF16),
                   jax.ShapeDtypeStruct((BATCH, gn, LTOT), BF16),
                   jax.ShapeDtypeStruct((BATCH, LTOT, gn), BF16)],
        compiler_params=_cp(("arbitrary", "arbitrary")),
        name="ssm_conv",
    )(xbc, xbc, xbc, w, b)


def _ssd_chunk(direction, xs_ref, cm_ref, bt_ref, dt_ref, dtT_ref, arow_ref, acol_ref, rexp_ref, s_ref):
    d0 = direction * HEADS
    dtc = dt_ref[:, d0:d0 + HEADS]
    dtr = dtT_ref[d0:d0 + HEADS, :]
    da_c = dtc * arow_ref[:, d0:d0 + HEADS]
    da_r = dtr * acol_ref[d0:d0 + HEADS, :]
    ii = lax.broadcasted_iota(I32, (Q, Q), 0)
    jj = lax.broadcasted_iota(I32, (Q, Q), 1)
    if direction == 0:
        lower = jj <= ii
        tot_idx = Q - 1
    else:
        lower = jj >= ii
        tot_idx = 0
    tri_c = jnp.where(lower, 1.0, 0.0).astype(BF16)
    upper = (ii <= jj) if direction == 0 else (ii >= jj)
    tri_r = jnp.where(upper, 1.0, 0.0).astype(BF16)
    c1, c2, c3 = _split3(da_c)
    cum_c = _dot(tri_c, c1) + _dot(tri_c, c2) + _dot(tri_c, c3)
    r1, r2, r3 = _split3(da_r)
    cum_r = _dot(r1, tri_r) + _dot(r2, tri_r) + _dot(r3, tri_r)
    tot_c = cum_c[tot_idx:tot_idx + 1, :]
    tot_r = cum_r[:, tot_idx:tot_idx + 1]
    rfac = dtr * jnp.exp(tot_r - cum_r)
    dec = jnp.exp(tot_c)
    dh = dec.astype(BF16)
    dl = (dec - dh.astype(F32)).astype(BF16)
    dec_x = (_dot(jnp.broadcast_to(dh, (8, HEADS)), rexp_ref[...])
             + _dot(jnp.broadcast_to(dl, (8, HEADS)), rexp_ref[...]))[0:1, :]
    assert Q == NSTATE and 2 * HEADDIM == 128
    odd_head = lax.broadcasted_iota(I32, (Q, 128), 1) >= HEADDIM
    ys = []
    for g in range(GROUPS):
        cg = cm_ref[:, g * NSTATE:(g + 1) * NSTATE]
        btg = bt_ref[g * NSTATE:(g + 1) * NSTATE, :]
        xg = xs_ref[:, g * 256:(g + 1) * 256]
        sg = s_ref[g]
        cb = _dot(cg, btg)
        sgb = sg.astype(BF16)
        cg32 = cg.astype(F32)
        btg32 = btg.astype(F32)
        yh = []
        uh = []
        for r in range(HPG):
            h = g * HPG + r
            hs = slice((r // 2) * 128, (r // 2 + 1) * 128)
            colb = cum_c[:, h:h + 1]
            rowb = cum_r[h:h + 1, :]
            decay = jnp.exp(jnp.where(lower, colb - rowb, NEG))
            m = (cb * decay * dtr[h:h + 1, :]).astype(BF16)
            cs = (cg32 * jnp.exp(colb)).astype(BF16)
            rhs = jnp.concatenate([xg[:, hs], sgb[:, hs]], axis=0)
            res = _dot(jnp.concatenate([m, cs], axis=1), rhs)
            bw = (btg32 * rfac[h:h + 1, :]).astype(BF16)
            upd = _dot(bw, xg[:, hs])
            if r % 2 == 0:
                yh.append(res)
                uh.append(upd)
            else:
                yh[-1] = jnp.where(odd_head, res, yh[-1])
                uh[-1] = jnp.where(odd_head, upd, uh[-1])
        s_ref[g] = sg * dec_x[:, g * 256:(g + 1) * 256] + jnp.concatenate(uh, axis=1)
        ys.append(jnp.concatenate(yh, axis=1))
    return ys


def _ssd_bwd_kernel(xs_ref, cm_ref, bt_ref, dt_ref, dtT_ref, arow_ref, acol_ref, rexp_ref,
                    y_ref, s_ref):
    @pl.when(pl.program_id(1) == 0)
    def _():
        s_ref[...] = jnp.zeros_like(s_ref)
    for i in range(NB):
        ys = _ssd_chunk(1, xs_ref.at[i], cm_ref.at[i], bt_ref.at[i], dt_ref.at[i], dtT_ref.at[i],
                        arow_ref, acol_ref, rexp_ref, s_ref.at[i])
        for g in range(GROUPS):
            y_ref[i, :, g * 256:(g + 1) * 256] = ys[g].astype(BF16)


def _ssd_fwd_kernel(xs_ref, cm_ref, bt_ref, dt_ref, dtT_ref, arow_ref, acol_ref, rexp_ref,
                    z_ref, yb_ref, dsk_ref, nw_ref, y_ref, s_ref):
    @pl.when(pl.program_id(1) == 0)
    def _():
        s_ref[...] = jnp.zeros_like(s_ref)
    for i in range(NB):
        ys = _ssd_chunk(0, xs_ref.at[i], cm_ref.at[i], bt_ref.at[i], dt_ref.at[i], dtT_ref.at[i],
                        arow_ref, acol_ref, rexp_ref, s_ref.at[i])
        for g in range(GROUPS):
            gs = slice(g * 256, (g + 1) * 256)
            y = ys[g] + yb_ref[i, :, gs].astype(F32) + xs_ref[i, :, gs].astype(F32) * dsk_ref[:, gs]
            y = y * _silu(z_ref[i, :, gs].astype(F32))
            y = y * lax.rsqrt(jnp.mean(y * y, axis=-1, keepdims=True) + EPS) * nw_ref[:, gs]
            y_ref[i, :, gs] = y.astype(BF16)


NB = 4


def _ssd_specs(cmap):
    gn = GROUPS * NSTATE
    full = lambda b, j: (0, 0)
    return [pl.BlockSpec((NB, Q, D_INNER), lambda b, j: (b, cmap(j), 0)),
            pl.BlockSpec((NB, Q, gn), lambda b, j: (b, cmap(j), 0)),
            pl.BlockSpec((NB, gn, Q), lambda b, j: (b, 0, cmap(j))),
            pl.BlockSpec((NB, Q, 2 * HEADS), lambda b, j: (b, cmap(j), 0)),
            pl.BlockSpec((NB, 2 * HEADS, Q), lambda b, j: (b, 0, cmap(j))),
            pl.BlockSpec((1, 2 * HEADS), full),
            pl.BlockSpec((2 * HEADS, 1), full),
            pl.BlockSpec((HEADS, D_INNER), full)]


def _ssd_bwd(xs, cm, bt, dt, dtT, arow, acol, rexp):
    cmap = lambda j: jnp.where(j < CTX_CHUNKS, CTX_CHUNKS - 1 - j, NCHUNK + CTX_CHUNKS - 1 - j)
    omap = lambda b, j: (b, NCHUNK - 1 - jnp.maximum(j, CTX_CHUNKS), 0)
    return pl.pallas_call(
        _ssd_bwd_kernel,
        grid=(BATCH // NB, NCHUNK),
        in_specs=_ssd_specs(cmap),
        out_specs=pl.BlockSpec((NB, Q, D_INNER), omap),
        out_shape=jax.ShapeDtypeStruct((BATCH, SEQ, D_INNER), BF16),
        scratch_shapes=[pltpu.VMEM((NB, GROUPS, NSTATE, HPG * HEADDIM), F32)],
        compiler_params=_cp(("arbitrary", "arbitrary")),
        name="ssd_bwd",
    )(xs, cm, bt, dt, dtT, arow, acol, rexp)


def _ssd_fwd(xs, cm, bt, dt, dtT, arow, acol, rexp, z, yb, dsk, nw):
    cmap = lambda j: j
    lat = lambda b, j: (b, jnp.maximum(j - CTX_CHUNKS, 0), 0)
    full = lambda b, j: (0, 0)
    return pl.pallas_call(
        _ssd_fwd_kernel,
        grid=(BATCH // NB, NCHUNK),
        in_specs=_ssd_specs(cmap) + [
            pl.BlockSpec((NB, Q, D_INNER), lambda b, j: (b, j, 0)),
            pl.BlockSpec((NB, Q, D_INNER), lat),
            pl.BlockSpec((1, D_INNER), full),
            pl.BlockSpec((1, D_INNER), full)],
        out_specs=pl.BlockSpec((NB, Q, D_INNER), lat),
        out_shape=jax.ShapeDtypeStruct((BATCH, SEQ, D_INNER), BF16),
        scratch_shapes=[pltpu.VMEM((NB, GROUPS, NSTATE, HPG * HEADDIM), F32)],
        compiler_params=_cp(("arbitrary", "arbitrary")),
        name="ssd_fwd",
    )(xs, cm, bt, dt, dtT, arow, acol, rexp, z, yb, dsk, nw)


def _route(h, rwT_ref, rb_ref, cnt_ref, eidx_ref, pos_ref, gate_ref):
    hh = h.astype(BF16)
    hl = (h - hh.astype(F32)).astype(BF16)
    w = rwT_ref[...]
    wh = w.astype(BF16)
    wl = (w - wh.astype(F32)).astype(BF16)
    logits = _dot_nt(wh, hh) + _dot_nt(wh, hl) + _dot_nt(wl, hh)
    scores = jax.nn.sigmoid(logits)
    sel = scores + rb_ref[...]
    per = E // NGRP
    sub = lax.broadcasted_iota(I32, (per, TM), 0)
    gscore = []
    for g in range(NGRP):
        blk = sel[g * per:(g + 1) * per, :]
        m1 = jnp.max(blk, axis=0, keepdims=True)
        first = jnp.min(jnp.where(blk == m1, sub, per), axis=0, keepdims=True)
        m2 = jnp.max(jnp.where(sub == first, -jnp.inf, blk), axis=0, keepdims=True)
        gscore.append(m1 + m2)
    masked = []
    for g in range(NGRP):
        rank = jnp.zeros((1, TM), F32)
        for o in range(NGRP):
            if o == g:
                continue
            ahead = (gscore[o] >= gscore[g]) if o < g else (gscore[o] > gscore[g])
            rank = rank + jnp.where(ahead, 1.0, 0.0)
        blk = sel[g * per:(g + 1) * per, :]
        masked.append(jnp.where(rank < TOPG, blk, -jnp.inf))
    v = jnp.concatenate(masked, axis=0)
    eio = lax.broadcasted_iota(I32, (E, TM), 0)
    kio = lax.broadcasted_iota(I32, (TOPK, TM), 0)
    eidx = jnp.zeros((TOPK, TM), I32)
    gsc = jnp.zeros((TOPK, TM), F32)
    hot = jnp.zeros((E, TM), F32)
    picks = []
    for k in range(TOPK):
        m = jnp.max(v, axis=0, keepdims=True)
        first = jnp.min(jnp.where(v == m, eio, E), axis=0, keepdims=True)
        pick = eio == first
        sc = jnp.sum(jnp.where(pick, scores, 0.0), axis=0, keepdims=True)
        eidx = jnp.where(kio == k, first, eidx)
        gsc = jnp.where(kio == k, sc, gsc)
        hot = jnp.where(pick, 1.0, hot)
        v = jnp.where(pick, -jnp.inf, v)
        picks.append(pick)
    gate = gsc / jnp.sum(gsc, axis=0, keepdims=True) * ROUTED_SCALE
    ti = lax.broadcasted_iota(I32, (TM, TM), 0)
    tj = lax.broadcasted_iota(I32, (TM, TM), 1)
    before = jnp.where(ti < tj, 1.0, 0.0).astype(BF16)
    posfull = _dot(hot.astype(BF16), before) + cnt_ref[...]
    pos = jnp.zeros((TOPK, TM), F32)
    for k in range(TOPK):
        pk = jnp.sum(jnp.where(picks[k], posfull, 0.0), axis=0, keepdims=True)
        pos = jnp.where(kio == k, pk, pos)
    cnt_ref[...] = cnt_ref[...] + jnp.sum(hot, axis=1, keepdims=True)
    eidx_ref[...] = eidx
    pos_ref[...] = pos.astype(I32)
    eye = jnp.where(ti == tj, 1.0, 0.0).astype(BF16)
    g1, g2, g3 = _split3(gate)
    gate_ref[...] = _dot_nt(eye, g1) + _dot_nt(eye, g2) + _dot_nt(eye, g3)


def _mix_epilogue(y, x_ref, mod_ref, gpost_ref, gpre_ref, rwT_ref, rb_ref,
                  x1_ref, h2_ref, eidx_ref, pos_ref, gate_ref, cnt_out_ref, cnt_ref):
    first = (pl.program_id(0) == 0) & (pl.program_id(1) == 0)

    @pl.when(first)
    def _():
        cnt_ref[...] = jnp.zeros_like(cnt_ref)
    x1 = x_ref[...] + mod_ref[2:3, :] * _rms(y, gpost_ref[...])
    x1_ref[...] = x1
    h2 = _rms(x1, gpre_ref[...]) * (1.0 + mod_ref[4:5, :]) + mod_ref[3:4, :]
    _store_packed(h2_ref, h2)
    _route(h2, rwT_ref, rb_ref, cnt_ref, eidx_ref, pos_ref, gate_ref)
    cnt_out_ref[...] = cnt_ref[...]


def _ssm_out_kernel(y_ref, w_ref, x_ref, mod_ref, gpost_ref, gpre_ref, rwT_ref, rb_ref,
                    x1_ref, h2_ref, eidx_ref, pos_ref, gate_ref, cnt_out_ref, cnt_ref):
    y = _dot(y_ref[...], w_ref[...])
    _mix_epilogue(y, x_ref, mod_ref, gpost_ref, gpre_ref, rwT_ref, rb_ref,
                  x1_ref, h2_ref, eidx_ref, pos_ref, gate_ref, cnt_out_ref, cnt_ref)


_CW = 256
_PAD = 16


def _conf_out_kernel(u_ref, dww_ref, dwb_ref, lng_ref, lnb_ref, w_ref, b_ref,
                     x_ref, mod_ref, gpost_ref, gpre_ref, rwT_ref, rb_ref,
                     x1_ref, h2_ref, eidx_ref, pos_ref, gate_ref, cnt_out_ref,
                     cnt_ref, v_ref):
    nrow = TM // GRID_W
    ext_rows = GRID_W + 2 * _PAD
    first = _PAD - CONF_K // 2
    span = GRID_W + 8 * ((CONF_K - 1) // 8)
    assert first + 7 + span <= ext_rows
    ri = lax.broadcasted_iota(I32, (span, ext_rows), 0)
    ji = lax.broadcasted_iota(I32, (span, ext_rows), 1)
    shift = [jnp.where(ji == ri + first + b, 1.0, 0.0).astype(BF16) for b in range(8)]
    zpad = jnp.zeros((_PAD, D), BF16)
    for r in range(nrow):
        ext = jnp.concatenate([zpad, u_ref[r * GRID_W:(r + 1) * GRID_W, :], zpad], axis=0)
        for c in range(D // _CW):
            cs = slice(c * _CW, (c + 1) * _CW)
            acc = jnp.broadcast_to(dwb_ref[:, cs], (GRID_W, _CW))
            for b in range(8):
                win = _dot(shift[b], ext[:, cs])
                for a in range(-(-CONF_K // 8)):
                    k = 8 * a + b
                    if k < CONF_K:
                        wk = jnp.concatenate([dww_ref[8 * k:8 * k + 8, cs]] * (GRID_W // 8), axis=0)
                        acc = acc + wk * win[8 * a:8 * a + GRID_W, :]
            v_ref[r * GRID_W:(r + 1) * GRID_W, cs] = acc
    v = v_ref[...]
    mu = jnp.mean(v, axis=-1, keepdims=True)
    vc = v - mu
    ln = vc * lax.rsqrt(jnp.mean(vc * vc, axis=-1, keepdims=True) + EPS) * lng_ref[...] + lnb_ref[...]
    y = _dot(_silu(ln).astype(BF16), w_ref[...]) + b_ref[...]
    _mix_epilogue(y, x_ref, mod_ref, gpost_ref, gpre_ref, rwT_ref, rb_ref,
                  x1_ref, h2_ref, eidx_ref, pos_ref, gate_ref, cnt_out_ref, cnt_ref)


def _mix_out_common_specs():
    full = lambda b, i: (0, 0)
    nt = SEQ // TM
    in_specs = [pl.BlockSpec((None, TM, D), lambda b, i: (b, i, 0)),
                pl.BlockSpec((None, 6, D), lambda b, i: (b, 0, 0)),
                pl.BlockSpec((1, D), full), pl.BlockSpec((1, D), full),
                pl.BlockSpec((E, D), full), pl.BlockSpec((E, 1), full)]
    out_specs = [pl.BlockSpec((None, TM, D), lambda b, i: (b, i, 0)),
                 pl.BlockSpec((TM * RW, 128), lambda b, i: (b * nt + i, 0)),
                 pl.BlockSpec((TOPK, TM), lambda b, i: (0, b * nt + i)),
                 pl.BlockSpec((TOPK, TM), lambda b, i: (0, b * nt + i)),
                 pl.BlockSpec((TM, TOPK), lambda b, i: (b * nt + i, 0)),
                 pl.BlockSpec((E, 1), full)]
    out_shape = [jax.ShapeDtypeStruct((BATCH, SEQ, D), F32),
                 jax.ShapeDtypeStruct((T * RW, 128), U32),
                 jax.ShapeDtypeStruct((TOPK, T), I32),
                 jax.ShapeDtypeStruct((TOPK, T), I32),
                 jax.ShapeDtypeStruct((T, TOPK), F32),
                 jax.ShapeDtypeStruct((E, 1), F32)]
    return in_specs, out_specs, out_shape


def _ssm_out(y, w, x, mod, gpost, gpre, rwT, rb):
    common_in, out_specs, out_shape = _mix_out_common_specs()
    return pl.pallas_call(
        _ssm_out_kernel,
        grid=(BATCH, SEQ // TM),
        in_specs=[pl.BlockSpec((None, TM, D_INNER), lambda b, i: (b, i, 0)),
                  pl.BlockSpec((D_INNER, D), lambda b, i: (0, 0))] + common_in,
        out_specs=out_specs, out_shape=out_shape,
        scratch_shapes=[pltpu.VMEM((E, 1), F32)],
        compiler_params=_cp(("arbitrary", "arbitrary")),
        name="ssm_out",
    )(y, w, x, mod, gpost, gpre, rwT, rb)


def _conf_out(u, dww, dwb, lng, lnb, w, b, x, mod, gpost, gpre, rwT, rb):
    common_in, out_specs, out_shape = _mix_out_common_specs()
    full = lambda b_, i: (0, 0)
    return pl.pallas_call(
        _conf_out_kernel,
        grid=(BATCH, SEQ // TM),
        in_specs=[pl.BlockSpec((None, TM, D), lambda b_, i: (b_, i, 0)),
                  pl.BlockSpec((CONF_K * 8, D), full), pl.BlockSpec((1, D), full),
                  pl.BlockSpec((1, D), full), pl.BlockSpec((1, D), full),
                  pl.BlockSpec((D, D), full), pl.BlockSpec((1, D), full)] + common_in,
        out_specs=out_specs, out_shape=out_shape,
        scratch_shapes=[pltpu.VMEM((E, 1), F32), pltpu.VMEM((TM, D), F32)],
        compiler_params=_cp(("arbitrary", "arbitrary")),
        name="conf_out",
    )(u, dww, dwb, lng, lnb, w, b, x, mod, gpost, gpre, rwT, rb)


def _conf_in_kernel(x_ref, mod_ref, g_ref, wa_ref, wg_ref, ba_ref, bg_ref, u_ref):
    h = _rms(x_ref[...], g_ref[...]) * (1.0 + mod_ref[1:2, :]) + mod_ref[0:1, :]
    hb = h.astype(BF16)
    a = _dot(hb, wa_ref[...]) + ba_ref[...]
    g = _dot(hb, wg_ref[...]) + bg_ref[...]
    u_ref[...] = (a * jax.nn.sigmoid(g)).astype(BF16)


def _conf_in(x, mod, gain, w, bias):
    full = lambda b, i: (0, 0)
    return pl.pallas_call(
        _conf_in_kernel,
        grid=(BATCH, SEQ // TM),
        in_specs=[pl.BlockSpec((None, TM, D), lambda b, i: (b, i, 0)),
                  pl.BlockSpec((None, 6, D), lambda b, i: (b, 0, 0)),
                  pl.BlockSpec((1, D), full),
                  pl.BlockSpec((D, D), full), pl.BlockSpec((D, D), lambda b, i: (0, 1)),
                  pl.BlockSpec((1, D), full), pl.BlockSpec((1, D), lambda b, i: (0, 1))],
        out_specs=pl.BlockSpec((None, TM, D), lambda b, i: (b, i, 0)),
        out_shape=jax.ShapeDtypeStruct((BATCH, SEQ, D), BF16),
        compiler_params=_cp(("arbitrary", "arbitrary")),
        name="conf_in",
    )(x, mod, gain, w, w, bias, bias)


def _row_copy(src, si, dst, di, sem):
    return pltpu.make_async_copy(src.at[pl.ds(pl.multiple_of(si * RW, RW), RW), :],
                                 dst.at[pl.ds(pl.multiple_of(di * RW, RW), RW), :], sem)


ZROWS = BLK + 8
_TU = 8


def _dispatch_kernel(zs_ref, na_ref, dest_ref, h_ref, xs_ref, zero_ref, zsem, sem):
    @pl.when(pl.program_id(0) == 0)
    def _():
        zero_ref[...] = jnp.zeros_like(zero_ref)

        def zcopy(e):
            start = pl.multiple_of(zs_ref[e] * RW, 8 * RW)
            return pltpu.make_async_copy(zero_ref, xs_ref.at[pl.ds(start, ZROWS * RW), :], zsem)

        def zstart(e, c):
            zcopy(e).start()
            return c

        def zwait(e, c):
            zcopy(e).wait()
            return c
        lax.fori_loop(0, E, zstart, 0)
        lax.fori_loop(0, E, zwait, 0)

        ntail = NBLK + 1 - na_ref[0]

        def tcopy(c):
            start = pl.multiple_of((na_ref[0] + c) * (BLK * RW), BLK * RW)
            return pltpu.make_async_copy(zero_ref.at[pl.ds(0, BLK * RW), :],
                                         xs_ref.at[pl.ds(start, BLK * RW), :], zsem)

        def tstart(c, carry):
            tcopy(c).start()
            return carry

        def twait(c, carry):
            tcopy(c).wait()
            return carry
        lax.fori_loop(0, ntail, tstart, 0)
        lax.fori_loop(0, ntail, twait, 0)
        last = pltpu.make_async_copy(zero_ref.at[pl.ds(0, 8 * RW), :],
                                     xs_ref.at[pl.ds((PROWS - 8) * RW, 8 * RW), :], zsem)
        last.start()
        last.wait()

    def issue(tb, c):
        for u in range(_TU):
            t = tb * _TU + u
            for k in range(TOPK):
                _row_copy(h_ref, t, xs_ref, dest_ref[t * TOPK + k], sem).start(priority=k % 2)
        return c

    lax.fori_loop(0, TM // _TU, issue, 0)
    for k in range(TOPK):
        pltpu.make_async_copy(h_ref, xs_ref.at[pl.ds(0, TM * RW), :], sem).wait()


def _dispatch(zstart, nact, dest, hp):
    grid_spec = pltpu.PrefetchScalarGridSpec(
        num_scalar_prefetch=2,
        grid=(T // TM,),
        in_specs=[pl.BlockSpec((TM * TOPK,), lambda i, zs, na: (i,), memory_space=pltpu.SMEM),
                  pl.BlockSpec((TM * RW, 128), lambda i, zs, na: (i, 0))],
        out_specs=pl.BlockSpec(memory_space=pl.ANY),
        scratch_shapes=[pltpu.VMEM((ZROWS * RW, 128), U32), pltpu.SemaphoreType.DMA(()),
                        pltpu.SemaphoreType.DMA(())])
    return pl.pallas_call(
        _dispatch_kernel,
        grid_spec=grid_spec,
        out_shape=jax.ShapeDtypeStruct((PROWS * RW, 128), U32),
        compiler_params=_cp(("arbitrary",)),
        name="moe_dispatch",
    )(zstart, nact, dest, hp)


XRING = 3


def _experts_kernel(be_ref, na_ref, x_hbm, wg_ref, wu_ref, wd_ref, y_ref, wgb_ref, wub_ref, wdb_ref,
                    xbuf_ref, xsem):
    i = pl.program_id(0)
    na = na_ref[0]

    def fetch(j):
        rows = BLK * RW
        src = x_hbm.at[pl.ds(pl.multiple_of(j * rows, rows), rows), :]
        return pltpu.make_async_copy(src, xbuf_ref.at[j % XRING], xsem.at[j % XRING])

    @pl.when(i == 0)
    def _():
        for j in range(XRING - 1):
            @pl.when(j < na)
            def _():
                fetch(j).start()

    @pl.when(i + XRING - 1 < na)
    def _():
        fetch(i + XRING - 1).start()

    @pl.when((i == 0) | (be_ref[i] != be_ref[jnp.maximum(i - 1, 0)]))
    def _():
        wgb_ref[...] = wg_ref[...].astype(BF16)
        wub_ref[...] = wu_ref[...].astype(BF16)
        wdb_ref[...] = wd_ref[...].astype(BF16)

    @pl.when(i < na)
    def _():
        fetch(i).wait()
        xb = _load_packed(xbuf_ref.at[i % XRING], BLK)
        g = _dot(xb, wgb_ref[...])
        u = _dot(xb, wub_ref[...])
        a = (_silu(g) * u).astype(BF16)
        _store_packed(y_ref, _dot(a, wdb_ref[...]))

    @pl.when(i >= na)
    def _():
        y_ref[...] = jnp.zeros_like(y_ref)


def _experts(layer, block_e, nact, xs, wg, wu, wd):
    wmap = lambda i, be, na: (layer, be[i], 0, 0)
    grid_spec = pltpu.PrefetchScalarGridSpec(
        num_scalar_prefetch=2,
        grid=(NBLK,),
        in_specs=[pl.BlockSpec(memory_space=pl.ANY),
                  pl.BlockSpec((None, None, D, DE), wmap),
                  pl.BlockSpec((None, None, D, DE), wmap),
                  pl.BlockSpec((None, None, DE, D), wmap)],
        out_specs=pl.BlockSpec((BLK * RW, 128), lambda i, be, na: (i, 0)),
        scratch_shapes=[pltpu.VMEM((D, DE), BF16), pltpu.VMEM((D, DE), BF16),
                        pltpu.VMEM((DE, D), BF16),
                        pltpu.VMEM((XRING, BLK * RW, 128), U32), pltpu.SemaphoreType.DMA((XRING,))])
    return pl.pallas_call(
        _experts_kernel,
        grid_spec=grid_spec,
        out_shape=jax.ShapeDtypeStruct((NBLK * BLK * RW, 128), U32),
        compiler_params=_cp(("arbitrary",)),
        name="moe_experts",
    )(block_e, nact, xs, wg, wu, wd)


def _combine_kernel(dcur_ref, dnxt_ref, gate_ref, h_ref, x1_ref, mod_ref, gpost_ref,
                    sg_ref, su_ref, sd_ref, ys_ref, o_ref, buf_ref, sems):
    i = pl.program_id(0)
    n = pl.num_programs(0)

    def gather(dest_ref, slot):
        def body(tb, c):
            for u in range(_TU):
                t = tb * _TU + u
                for k in range(TOPK):
                    _row_copy(ys_ref, dest_ref[t * TOPK + k], buf_ref.at[slot, k], t,
                              sems.at[slot]).start(priority=k % 2)
            return c
        lax.fori_loop(0, TM // _TU, body, 0)

    @pl.when(i == 0)
    def _():
        gather(dcur_ref, 0)

    for slot_next in range(2):
        @pl.when((i + 1 < n) & ((i + 1) % 2 == slot_next))
        def _():
            gather(dnxt_ref, slot_next)

    hb = _load_packed(h_ref, TM)
    a = (_silu(_dot(hb, sg_ref[...])) * _dot(hb, su_ref[...])).astype(BF16)
    f = _dot(a, sd_ref[...])

    slot = i % 2

    for k in range(TOPK):
        pltpu.make_async_copy(ys_ref.at[pl.ds(0, TM * RW), :], buf_ref.at[slot, k],
                              sems.at[slot]).wait()

    gate = gate_ref[...]
    gb = [jnp.broadcast_to(gate[:, k:k + 1], (TM, 128)) for k in range(TOPK)]
    los, his = [], []
    for s in range(RW):
        lo = f[:, s * 128:(s + 1) * 128]
        hi = f[:, D // 2 + s * 128:D // 2 + (s + 1) * 128]
        for k in range(TOPK):
            wl, wh = _unpack2(buf_ref[slot, k, pl.ds(s, TM, stride=RW), :])
            lo = lo + gb[k] * wl
            hi = hi + gb[k] * wh
        los.append(lo)
        his.append(hi)
    f = jnp.concatenate(los + his, axis=1)
    o_ref[...] = x1_ref[...] + mod_ref[5:6, :] * _rms(f, gpost_ref[...])


def _combine(dest, gate, hp, x1, mod, gpost, sg, su, sd, ys):
    nt = SEQ // TM
    n = T // TM
    full = lambda i: (0, 0)
    return pl.pallas_call(
        _combine_kernel,
        grid=(n,),
        in_specs=[pl.BlockSpec((TM * TOPK,), lambda i: (i,), memory_space=pltpu.SMEM),
                  pl.BlockSpec((TM * TOPK,), lambda i: (jnp.minimum(i + 1, n - 1),),
                               memory_space=pltpu.SMEM),
                  pl.BlockSpec((TM, TOPK), lambda i: (i, 0)),
                  pl.BlockSpec((TM * RW, 128), lambda i: (i, 0)),
                  pl.BlockSpec((TM, D), lambda i: (i, 0)),
                  pl.BlockSpec((None, 6, D), lambda i: (i // nt, 0, 0)),
                  pl.BlockSpec((1, D), full),
                  pl.BlockSpec((D, DE), full), pl.BlockSpec((D, DE), full), pl.BlockSpec((DE, D), full),
                  pl.BlockSpec(memory_space=pl.ANY)],
        out_specs=pl.BlockSpec((TM, D), lambda i: (i, 0)),
        out_shape=jax.ShapeDtypeStruct((T, D), F32),
        scratch_shapes=[pltpu.VMEM((2, TOPK, TM * RW, 128), U32), pltpu.SemaphoreType.DMA((2,))],
        compiler_params=_cp(("arbitrary",)),
        name="moe_combine",
    )(dest, dest, gate, hp, x1.reshape(T, D), mod, gpost, sg, su, sd, ys).reshape(BATCH, SEQ, D)


def _moe(layer, x1, hp, eidx, pos, gate, counts, mod, gpost, wg, wu, wd, sg, su, sd):
    cnt = counts.reshape(E).astype(I32)
    padded = (cnt + BLK - 1) // BLK * BLK
    pend = jnp.cumsum(padded)
    pstart = pend - padded
    ids = jnp.arange(E, dtype=I32)
    dest = pos + jnp.sum(jnp.where(eidx[:, :, None] == ids, pstart, 0), axis=-1)
    dest = dest.T.reshape(T * TOPK)
    nact = (pend[-1] // BLK).reshape(1).astype(I32)
    blk0 = jnp.arange(NBLK, dtype=I32) * BLK
    block_e = jnp.sum((pend[None, :] <= blk0[:, None]).astype(I32), axis=1)
    block_e = jnp.minimum(block_e, E - 1).astype(I32)
    xs = _dispatch(((pstart + cnt) // 8 * 8).astype(I32), nact, dest, hp)
    ys = _experts(layer, block_e, nact, xs, wg, wu, wd)
    return _combine(dest, gate, hp, x1, mod, gpost, sg.astype(BF16), su.astype(BF16),
                    sd.astype(BF16), ys)


def kernel(x, c, ctx, c_ctx, ada_w, ada_b, norm_mix_pre, norm_mix_post, norm_ffn_pre, norm_ffn_post, ssm_w_in, ssm_conv_w, ssm_conv_b, ssm_dt_bias, ssm_a_log, ssm_d, ssm_norm, ssm_w_out, cv_w_in, cv_b_in, cv_dw_w, cv_dw_b, cv_ln_g, cv_ln_b, cv_w_out, cv_b_out, router_w, router_b, exp_w_gate, exp_w_up, exp_w_down, sh_w_gate, sh_w_up, sh_w_down):
    row = lambda v: v.reshape(1, -1)
    cvec = jnp.concatenate([c, c_ctx[None, :], jnp.zeros((3, D), F32)], axis=0)
    mod = _ada(cvec, ada_w, ada_b).reshape(2, 8, 6, D)

    w_in = ssm_w_in[0].astype(BF16)
    dtb = ssm_dt_bias[0].reshape(1, 2 * HEADS)
    wdt = w_in[:, D_INNER + CONV_DIM:]
    z, xbc, dt, dtT = _ssm_in(ctx, x, mod[0], row(norm_mix_pre[0]), w_in, wdt, wdt.T,
                              dtb, dtb.reshape(2 * HEADS, 1))
    xs, bt, cm = _ssm_conv(xbc, jnp.repeat(ssm_conv_w[0], 8, axis=0), row(ssm_conv_b[0]))
    a = -jnp.exp(ssm_a_log[0].astype(F32)).reshape(1, 2 * HEADS)
    rexp = (jnp.arange(D_INNER)[None, :] // HEADDIM == jnp.arange(HEADS)[:, None]).astype(BF16)
    yb = _ssd_bwd(xs, cm, bt, dt, dtT, a, a.reshape(2 * HEADS, 1), rexp)
    dsk = jnp.repeat(ssm_d[0], HEADDIM).reshape(1, D_INNER)
    ygn = _ssd_fwd(xs, cm, bt, dt, dtT, a, a.reshape(2 * HEADS, 1), rexp, z, yb, dsk,
                   row(ssm_norm[0]))
    x1, h2, eidx, pos, gate, counts = _ssm_out(
        ygn, ssm_w_out[0].astype(BF16), x, mod[0], row(norm_mix_post[0]), row(norm_ffn_pre[0]),
        router_w[0].T, router_b[0].reshape(E, 1))
    x2 = _moe(0, x1, h2, eidx, pos, gate, counts, mod[0], row(norm_ffn_post[0]),
              exp_w_gate, exp_w_up, exp_w_down, sh_w_gate[0], sh_w_up[0], sh_w_down[0])

    u = _conf_in(x2, mod[1], row(norm_mix_pre[1]), cv_w_in[0].astype(BF16), row(cv_b_in[0]))
    x3, h4, eidx, pos, gate, counts = _conf_out(
        u, jnp.repeat(cv_dw_w[0], 8, axis=0), row(cv_dw_b[0]), row(cv_ln_g[0]), row(cv_ln_b[0]),
        cv_w_out[0].astype(BF16), row(cv_b_out[0]),
        x2, mod[1], row(norm_mix_post[1]), row(norm_ffn_pre[1]),
        router_w[1].T, router_b[1].reshape(E, 1))
    return _moe(1, x3, h4, eidx, pos, gate, counts, mod[1], row(norm_ffn_post[1]),
                exp_w_gate, exp_w_up, exp_w_down, sh_w_gate[1], sh_w_up[1], sh_w_down[1])
```

```python
import functools

import jax
import jax.numpy as jnp
from jax import lax
from jax.experimental import pallas as pl
from jax.experimental.pallas import tpu as pltpu

F32 = jnp.float32
BF16 = jnp.bfloat16
I32 = jnp.int32

D = 1024
BATCH = 4
SEQ = 4096
CTX = 256
LTOT = CTX + SEQ
GRID_W = 64

D_INNER = 2048
HEADS = 32
GROUPS = 8
HPG = 4
HEADDIM = 64
NSTATE = 128
Q = 128
NCHUNK = LTOT // Q
CTX_CHUNKS = CTX // Q
CONV_DIM = D_INNER + 2 * GROUPS * NSTATE
SSM_K = 5
CONF_K = 31

E = 64
TOPK = 8
NGRP = 8
TOPG = 4
DE = 256
ROUTED_SCALE = 2.5
EPS = 1e-6

T = BATCH * SEQ
TM = 512
BLK = 512
NBLK = -(-(T * TOPK + E * (BLK - 1)) // BLK)
PROWS = (NBLK + 1) * BLK + 8

VMEM_LIMIT = 56 * 1024 * 1024
NEG = -1e30


def _cp(sem):
    return pltpu.CompilerParams(dimension_semantics=sem, vmem_limit_bytes=VMEM_LIMIT)


def _silu(v):
    return v * jax.nn.sigmoid(v)


def _rms(v, g):
    return v * lax.rsqrt(jnp.mean(v * v, axis=-1, keepdims=True) + EPS) * g


def _split3(v):
    a = v.astype(BF16)
    r = v - a.astype(F32)
    b = r.astype(BF16)
    c = (r - b.astype(F32)).astype(BF16)
    return a, b, c


def _dot(a, b):
    return jnp.dot(a, b, preferred_element_type=F32)


def _dot_nt(a, b):
    return lax.dot_general(a, b, (((1,), (1,)), ((), ())), preferred_element_type=F32)


U32 = jnp.uint32
RW = D // 2 // 128
_HI = 0xFFFF0000


def _pack2(lo, hi):
    ul = pltpu.bitcast(lo.astype(BF16).astype(F32), U32)
    uh = pltpu.bitcast(hi.astype(BF16).astype(F32), U32)
    return (ul >> 16) | (uh & U32(_HI))


def _unpack2(w):
    return pltpu.bitcast(w << 16, F32), pltpu.bitcast(w & U32(_HI), F32)


def _store_packed(ref, v):
    m = v.shape[0]
    for s in range(RW):
        lo = v[:, s * 128:(s + 1) * 128]
        hi = v[:, D // 2 + s * 128:D // 2 + (s + 1) * 128]
        ref[pl.ds(s, m, stride=RW), :] = _pack2(lo, hi)


def _load_packed(ref, m):
    los, his = [], []
    for s in range(RW):
        lo, hi = _unpack2(ref[pl.ds(s, m, stride=RW), :])
        los.append(lo.astype(BF16))
        his.append(hi.astype(BF16))
    return jnp.concatenate(los + his, axis=1)


def _ada_kernel(c_ref, w_ref, b_ref, o_ref):
    s = _silu(c_ref[...])
    o_ref[...] = jnp.dot(s, w_ref[...], preferred_element_type=F32,
                         precision=lax.Precision.HIGHEST) + b_ref[...]


def _ada(cvec, ada_w, ada_b):
    depth = ada_w.shape[0]
    tn = 1536
    return pl.pallas_call(
        _ada_kernel,
        grid=(depth, 6 * D // tn),
        in_specs=[pl.BlockSpec((8, D), lambda l, j: (0, 0)),
                  pl.BlockSpec((None, D, tn), lambda l, j: (l, 0, j)),
                  pl.BlockSpec((None, 1, tn), lambda l, j: (l, 0, j))],
        out_specs=pl.BlockSpec((None, 8, tn), lambda l, j: (l, 0, j)),
        out_shape=jax.ShapeDtypeStruct((depth, 8, 6 * D), F32),
        compiler_params=_cp(("arbitrary", "arbitrary")),
        name="ada",
    )(cvec, ada_w, ada_b.reshape(depth, 1, 6 * D))


def _ssm_in_kernel(c_ref, x_ref, mod_ref, g_ref, wz_ref, wx0_ref, wx1_ref, wdt_ref, wdtT_ref, dtb_ref,
                   dtbT_ref, z_ref, xbc_ref, dt_ref, dtT_ref):
    xin = jnp.where(pl.program_id(1) == 0, c_ref[...], x_ref[...])
    h = _rms(xin, g_ref[...]) * (1.0 + mod_ref[1:2, :]) + mod_ref[0:1, :]
    hb = h.astype(BF16)
    z_ref[...] = _dot(hb, wz_ref[...]).astype(BF16)
    xbc_ref[:, :D_INNER] = _dot(hb, wx0_ref[...]).astype(BF16)
    xbc_ref[:, D_INNER:] = _dot(hb, wx1_ref[...]).astype(BF16)
    dt_ref[...] = jax.nn.softplus(_dot(hb, wdt_ref[...]) + dtb_ref[...])
    dtT_ref[...] = jax.nn.softplus(_dot_nt(wdtT_ref[...], hb) + dtbT_ref[...])


def _ssm_in(ctx, x, mod, gain, w_in, wdt, wdtT, dtb, dtbT):
    ts = CTX
    nt = LTOT // ts
    full = lambda b, i: (0, 0)
    assert CONV_DIM == 2 * D_INNER and SEQ % ts == 0
    return pl.pallas_call(
        _ssm_in_kernel,
        grid=(BATCH, nt),
        in_specs=[pl.BlockSpec((None, ts, D), lambda b, i: (b, 0, 0)),
                  pl.BlockSpec((None, ts, D), lambda b, i: (b, jnp.maximum(i - 1, 0), 0)),
                  pl.BlockSpec((None, 6, D), lambda b, i: (jnp.where(i == 0, BATCH, b), 0, 0)),
                  pl.BlockSpec((1, D), full),
                  pl.BlockSpec((D, D_INNER), lambda b, i: (0, 0)),
                  pl.BlockSpec((D, D_INNER), lambda b, i: (0, 1)),
                  pl.BlockSpec((D, D_INNER), lambda b, i: (0, 2)),
                  pl.BlockSpec((D, 2 * HEADS), full),
                  pl.BlockSpec((2 * HEADS, D), full),
                  pl.BlockSpec((1, 2 * HEADS), full),
                  pl.BlockSpec((2 * HEADS, 1), full)],
        out_specs=[pl.BlockSpec((None, ts, D_INNER), lambda b, i: (b, i, 0)),
                   pl.BlockSpec((None, ts, CONV_DIM), lambda b, i: (b, i, 0)),
                   pl.BlockSpec((None, ts, 2 * HEADS), lambda b, i: (b, i, 0)),
                   pl.BlockSpec((None, 2 * HEADS, ts), lambda b, i: (b, 0, i))],
        out_shape=[jax.ShapeDtypeStruct((BATCH, LTOT, D_INNER), BF16),
                   jax.ShapeDtypeStruct((BATCH, LTOT, CONV_DIM), BF16),
                   jax.ShapeDtypeStruct((BATCH, LTOT, 2 * HEADS), F32),
                   jax.ShapeDtypeStruct((BATCH, 2 * HEADS, LTOT), F32)],
        compiler_params=_cp(("arbitrary", "arbitrary")),
        name="ssm_in",
    )(ctx, x, mod, gain, w_in, w_in, w_in, wdt, wdtT, dtb, dtbT)


_HALO = 16
_CC = 512


def _ssm_conv_kernel(x_ref, xp_ref, xn_ref, w_ref, b_ref, xs_ref, bt_ref, cm_ref):
    i = pl.program_id(1)
    first_lat = CTX // Q
    pvalid = (i != 0) & (i != first_lat)
    nvalid = (i != first_lat - 1) & (i != NCHUNK - 1)
    li = lax.broadcasted_iota(I32, (Q, Q + 2 * _HALO), 0)
    ji = lax.broadcasted_iota(I32, (Q, Q + 2 * _HALO), 1)
    shift = {k: jnp.where(ji == li + _HALO + k - SSM_K // 2, 1.0, 0.0).astype(BF16)
             for k in range(SSM_K) if k != SSM_K // 2}
    for c in range(CONV_DIM // _CC):
        cs = slice(c * _CC, (c + 1) * _CC)
        cur = x_ref[:, cs]
        prev = jnp.where(pvalid, xp_ref[:, cs], jnp.zeros((_HALO, _CC), BF16))
        nxt = jnp.where(nvalid, xn_ref[:, cs], jnp.zeros((_HALO, _CC), BF16))
        ext = jnp.concatenate([prev, cur, nxt], axis=0)
        acc = jnp.broadcast_to(b_ref[:, cs], (Q, _CC))
        for k in range(SSM_K):
            wk = jnp.concatenate([w_ref[8 * k:8 * k + 8, cs]] * (Q // 8), axis=0)
            tap = cur.astype(F32) if k == SSM_K // 2 else _dot(shift[k], ext)
            acc = acc + wk * tap
        y = _silu(acc)
        lo = c * _CC
        if lo < D_INNER:
            xs_ref[:, cs] = y.astype(BF16)
        elif lo < D_INNER + GROUPS * NSTATE:
            o = lo - D_INNER
            bt_ref[o:o + _CC, :] = y.T.astype(BF16)
        else:
            o = lo - D_INNER - GROUPS * NSTATE
            cm_ref[:, o:o + _CC] = y.astype(BF16)


def _ssm_conv(xbc, w, b):
    nh = Q // _HALO
    last = LTOT // _HALO - 1
    gn = GROUPS * NSTATE
    return pl.pallas_call(
        _ssm_conv_kernel,
        grid=(BATCH, NCHUNK),
        in_specs=[pl.BlockSpec((None, Q, CONV_DIM), lambda b, i: (b, i, 0)),
                  pl.BlockSpec((None, _HALO, CONV_DIM), lambda b, i: (b, jnp.maximum(i * nh - 1, 0), 0)),
                  pl.BlockSpec((None, _HALO, CONV_DIM), lambda b, i: (b, jnp.minimum(i * nh + nh, last), 0)),
                  pl.BlockSpec((SSM_K * 8, CONV_DIM), lambda b, i: (0, 0)),
                  pl.BlockSpec((1, CONV_DIM), lambda b, i: (0, 0))],
        out_specs=[pl.BlockSpec((None, Q, D_INNER), lambda b, i: (b, i, 0)),
                   pl.BlockSpec((None, gn, Q), lambda b, i: (b, 0, i)),
                   pl.BlockSpec((None, Q, gn), lambda b, i: (b, i, 0))],
        out_shape=[jax.ShapeDtypeStruct((BATCH, LTOT, D_INNER), BF16),
                   jax.ShapeDtypeStruct((BATCH, gn, LTOT), BF16),
                   jax.ShapeDtypeStruct((BATCH, LTOT, gn), BF16)],
        compiler_params=_cp(("arbitrary", "arbitrary")),
        name="ssm_conv",
    )(xbc, xbc, xbc, w, b)


def _ssd_chunk(direction, xs_ref, cm_ref, bt_ref, dt_ref, dtT_ref, arow_ref, acol_ref, rexp_ref, s_ref):
    d0 = direction * HEADS
    dtc = dt_ref[:, d0:d0 + HEADS]
    dtr = dtT_ref[d0:d0 + HEADS, :]
    da_c = dtc * arow_ref[:, d0:d0 + HEADS]
    da_r = dtr * acol_ref[d0:d0 + HEADS, :]
    ii = lax.broadcasted_iota(I32, (Q, Q), 0)
    jj = lax.broadcasted_iota(I32, (Q, Q), 1)
    if direction == 0:
        lower = jj <= ii
        tot_idx = Q - 1
    else:
        lower = jj >= ii
        tot_idx = 0
    tri_c = jnp.where(lower, 1.0, 0.0).astype(BF16)
    upper = (ii <= jj) if direction == 0 else (ii >= jj)
    tri_r = jnp.where(upper, 1.0, 0.0).astype(BF16)
    c1, c2, c3 = _split3(da_c)
    cum_c = _dot(tri_c, c1) + _dot(tri_c, c2) + _dot(tri_c, c3)
    r1, r2, r3 = _split3(da_r)
    cum_r = _dot(r1, tri_r) + _dot(r2, tri_r) + _dot(r3, tri_r)
    tot_c = cum_c[tot_idx:tot_idx + 1, :]
    tot_r = cum_r[:, tot_idx:tot_idx + 1]
    rfac = dtr * jnp.exp(tot_r - cum_r)
    dec = jnp.exp(tot_c)
    dh = dec.astype(BF16)
    dl = (dec - dh.astype(F32)).astype(BF16)
    dec_x = (_dot(jnp.broadcast_to(dh, (8, HEADS)), rexp_ref[...])
             + _dot(jnp.broadcast_to(dl, (8, HEADS)), rexp_ref[...]))[0:1, :]
    assert Q == NSTATE and 2 * HEADDIM == 128
    odd_head = lax.broadcasted_iota(I32, (Q, 128), 1) >= HEADDIM
    ys = []
    for g in range(GROUPS):
        cg = cm_ref[:, g * NSTATE:(g + 1) * NSTATE]
        btg = bt_ref[g * NSTATE:(g + 1) * NSTATE, :]
        xg = xs_ref[:, g * 256:(g + 1) * 256]
        sg = s_ref[g]
        cb = _dot(cg, btg)
        sgb = sg.astype(BF16)
        cg32 = cg.astype(F32)
        btg32 = btg.astype(F32)
        yh = []
        uh = []
        for r in range(HPG):
            h = g * HPG + r
            hs = slice((r // 2) * 128, (r // 2 + 1) * 128)
            colb = cum_c[:, h:h + 1]
            rowb = cum_r[h:h + 1, :]
            decay = jnp.exp(jnp.where(lower, colb - rowb, NEG))
            m = (cb * decay * dtr[h:h + 1, :]).astype(BF16)
            cs = (cg32 * jnp.exp(colb)).astype(BF16)
            rhs = jnp.concatenate([xg[:, hs], sgb[:, hs]], axis=0)
            res = _dot(jnp.concatenate([m, cs], axis=1), rhs)
            bw = (btg32 * rfac[h:h + 1, :]).astype(BF16)
            upd = _dot(bw, xg[:, hs])
            if r % 2 == 0:
                yh.append(res)
                uh.append(upd)
            else:
                yh[-1] = jnp.where(odd_head, res, yh[-1])
                uh[-1] = jnp.where(odd_head, upd, uh[-1])
        s_ref[g] = sg * dec_x[:, g * 256:(g + 1) * 256] + jnp.concatenate(uh, axis=1)
        ys.append(jnp.concatenate(yh, axis=1))
    return ys


def _ssd_bwd_kernel(xs_ref, cm_ref, bt_ref, dt_ref, dtT_ref, arow_ref, acol_ref, rexp_ref,
                    y_ref, s_ref):
    @pl.when(pl.program_id(1) == 0)
    def _():
        s_ref[...] = jnp.zeros_like(s_ref)
    for i in range(NB):
        ys = _ssd_chunk(1, xs_ref.at[i], cm_ref.at[i], bt_ref.at[i], dt_ref.at[i], dtT_ref.at[i],
                        arow_ref, acol_ref, rexp_ref, s_ref.at[i])
        for g in range(GROUPS):
            y_ref[i, :, g * 256:(g + 1) * 256] = ys[g].astype(BF16)


def _ssd_fwd_kernel(xs_ref, cm_ref, bt_ref, dt_ref, dtT_ref, arow_ref, acol_ref, rexp_ref,
                    z_ref, yb_ref, dsk_ref, nw_ref, y_ref, s_ref):
    @pl.when(pl.program_id(1) == 0)
    def _():
        s_ref[...] = jnp.zeros_like(s_ref)
    for i in range(NB):
        ys = _ssd_chunk(0, xs_ref.at[i], cm_ref.at[i], bt_ref.at[i], dt_ref.at[i], dtT_ref.at[i],
                        arow_ref, acol_ref, rexp_ref, s_ref.at[i])
        for g in range(GROUPS):
            gs = slice(g * 256, (g + 1) * 256)
            y = ys[g] + yb_ref[i, :, gs].astype(F32) + xs_ref[i, :, gs].astype(F32) * dsk_ref[:, gs]
            y = y * _silu(z_ref[i, :, gs].astype(F32))
            y = y * lax.rsqrt(jnp.mean(y * y, axis=-1, keepdims=True) + EPS) * nw_ref[:, gs]
            y_ref[i, :, gs] = y.astype(BF16)


NB = 4


def _ssd_specs(cmap):
    gn = GROUPS * NSTATE
    full = lambda b, j: (0, 0)
    return [pl.BlockSpec((NB, Q, D_INNER), lambda b, j: (b, cmap(j), 0)),
            pl.BlockSpec((NB, Q, gn), lambda b, j: (b, cmap(j), 0)),
            pl.BlockSpec((NB, gn, Q), lambda b, j: (b, 0, cmap(j))),
            pl.BlockSpec((NB, Q, 2 * HEADS), lambda b, j: (b, cmap(j), 0)),
            pl.BlockSpec((NB, 2 * HEADS, Q), lambda b, j: (b, 0, cmap(j))),
            pl.BlockSpec((1, 2 * HEADS), full),
            pl.BlockSpec((2 * HEADS, 1), full),
            pl.BlockSpec((HEADS, D_INNER), full)]


def _ssd_bwd(xs, cm, bt, dt, dtT, arow, acol, rexp):
    cmap = lambda j: jnp.where(j < CTX_CHUNKS, CTX_CHUNKS - 1 - j, NCHUNK + CTX_CHUNKS - 1 - j)
    omap = lambda b, j: (b, NCHUNK - 1 - jnp.maximum(j, CTX_CHUNKS), 0)
    return pl.pallas_call(
        _ssd_bwd_kernel,
        grid=(BATCH // NB, NCHUNK),
        in_specs=_ssd_specs(cmap),
        out_specs=pl.BlockSpec((NB, Q, D_INNER), omap),
        out_shape=jax.ShapeDtypeStruct((BATCH, SEQ, D_INNER), BF16),
        scratch_shapes=[pltpu.VMEM((NB, GROUPS, NSTATE, HPG * HEADDIM), F32)],
        compiler_params=_cp(("arbitrary", "arbitrary")),
        name="ssd_bwd",
    )(xs, cm, bt, dt, dtT, arow, acol, rexp)


def _ssd_fwd(xs, cm, bt, dt, dtT, arow, acol, rexp, z, yb, dsk, nw):
    cmap = lambda j: j
    lat = lambda b, j: (b, jnp.maximum(j - CTX_CHUNKS, 0), 0)
    full = lambda b, j: (0, 0)
    return pl.pallas_call(
        _ssd_fwd_kernel,
        grid=(BATCH // NB, NCHUNK),
        in_specs=_ssd_specs(cmap) + [
            pl.BlockSpec((NB, Q, D_INNER), lambda b, j: (b, j, 0)),
            pl.BlockSpec((NB, Q, D_INNER), lat),
            pl.BlockSpec((1, D_INNER), full),
            pl.BlockSpec((1, D_INNER), full)],
        out_specs=pl.BlockSpec((NB, Q, D_INNER), lat),
        out_shape=jax.ShapeDtypeStruct((BATCH, SEQ, D_INNER), BF16),
        scratch_shapes=[pltpu.VMEM((NB, GROUPS, NSTATE, HPG * HEADDIM), F32)],
        compiler_params=_cp(("arbitrary", "arbitrary")),
        name="ssd_fwd",
    )(xs, cm, bt, dt, dtT, arow, acol, rexp, z, yb, dsk, nw)


def _route(h, rwT_ref, rb_ref, cnt_ref, eidx_ref, pos_ref, gate_ref):
    hh = h.astype(BF16)
    hl = (h - hh.astype(F32)).astype(BF16)
    w = rwT_ref[...]
    wh = w.astype(BF16)
    wl = (w - wh.astype(F32)).astype(BF16)
    logits = _dot_nt(wh, hh) + _dot_nt(wh, hl) + _dot_nt(wl, hh)
    scores = jax.nn.sigmoid(logits)
    sel = scores + rb_ref[...]
    per = E // NGRP
    sub = lax.broadcasted_iota(I32, (per, TM), 0)
    gscore = []
    for g in range(NGRP):
        blk = sel[g * per:(g + 1) * per, :]
        m1 = jnp.max(blk, axis=0, keepdims=True)
        first = jnp.min(jnp.where(blk == m1, sub, per), axis=0, keepdims=True)
        m2 = jnp.max(jnp.where(sub == first, -jnp.inf, blk), axis=0, keepdims=True)
        gscore.append(m1 + m2)
    masked = []
    for g in range(NGRP):
        rank = jnp.zeros((1, TM), F32)
        for o in range(NGRP):
            if o == g:
                continue
            ahead = (gscore[o] >= gscore[g]) if o < g else (gscore[o] > gscore[g])
            rank = rank + jnp.where(ahead, 1.0, 0.0)
        blk = sel[g * per:(g + 1) * per, :]
        masked.append(jnp.where(rank < TOPG, blk, -jnp.inf))
    v = jnp.concatenate(masked, axis=0)
    eio = lax.broadcasted_iota(I32, (E, TM), 0)
    kio = lax.broadcasted_iota(I32, (TOPK, TM), 0)
    eidx = jnp.zeros((TOPK, TM), I32)
    gsc = jnp.zeros((TOPK, TM), F32)
    hot = jnp.zeros((E, TM), F32)
    picks = []
    for k in range(TOPK):
        m = jnp.max(v, axis=0, keepdims=True)
        first = jnp.min(jnp.where(v == m, eio, E), axis=0, keepdims=True)
        pick = eio == first
        sc = jnp.sum(jnp.where(pick, scores, 0.0), axis=0, keepdims=True)
        eidx = jnp.where(kio == k, first, eidx)
        gsc = jnp.where(kio == k, sc, gsc)
        hot = jnp.where(pick, 1.0, hot)
        v = jnp.where(pick, -jnp.inf, v)
        picks.append(pick)
    gate = gsc / jnp.sum(gsc, axis=0, keepdims=True) * ROUTED_SCALE
    ti = lax.broadcasted_iota(I32, (TM, TM), 0)
    tj = lax.broadcasted_iota(I32, (TM, TM), 1)
    before = jnp.where(ti < tj, 1.0, 0.0).astype(BF16)
    posfull = _dot(hot.astype(BF16), before) + cnt_ref[...]
    pos = jnp.zeros((TOPK, TM), F32)
    for k in range(TOPK):
        pk = jnp.sum(jnp.where(picks[k], posfull, 0.0), axis=0, keepdims=True)
        pos = jnp.where(kio == k, pk, pos)
    cnt_ref[...] = cnt_ref[...] + jnp.sum(hot, axis=1, keepdims=True)
    eidx_ref[...] = eidx
    pos_ref[...] = pos.astype(I32)
    eye = jnp.where(ti == tj, 1.0, 0.0).astype(BF16)
    g1, g2, g3 = _split3(gate)
    gate_ref[...] = _dot_nt(eye, g1) + _dot_nt(eye, g2) + _dot_nt(eye, g3)


def _mix_epilogue(y, x_ref, mod_ref, gpost_ref, gpre_ref, rwT_ref, rb_ref,
                  x1_ref, h2_ref, eidx_ref, pos_ref, gate_ref, cnt_out_ref, cnt_ref):
    first = (pl.program_id(0) == 0) & (pl.program_id(1) == 0)

    @pl.when(first)
    def _():
        cnt_ref[...] = jnp.zeros_like(cnt_ref)
    x1 = x_ref[...] + mod_ref[2:3, :] * _rms(y, gpost_ref[...])
    x1_ref[...] = x1
    h2 = _rms(x1, gpre_ref[...]) * (1.0 + mod_ref[4:5, :]) + mod_ref[3:4, :]
    _store_packed(h2_ref, h2)
    _route(h2, rwT_ref, rb_ref, cnt_ref, eidx_ref, pos_ref, gate_ref)
    cnt_out_ref[...] = cnt_ref[...]


def _ssm_out_kernel(y_ref, w_ref, x_ref, mod_ref, gpost_ref, gpre_ref, rwT_ref, rb_ref,
                    x1_ref, h2_ref, eidx_ref, pos_ref, gate_ref, cnt_out_ref, cnt_ref):
    y = _dot(y_ref[...], w_ref[...])
    _mix_epilogue(y, x_ref, mod_ref, gpost_ref, gpre_ref, rwT_ref, rb_ref,
                  x1_ref, h2_ref, eidx_ref, pos_ref, gate_ref, cnt_out_ref, cnt_ref)


_CW = 256
_PAD = 16


def _conf_out_kernel(gin_ref, wa_ref, wg_ref, ba_ref, bg_ref, dww_ref, dwb_ref, lng_ref, lnb_ref,
                     w_ref, b_ref, x_ref, mod_ref, gpost_ref, gpre_ref, rwT_ref, rb_ref,
                     x1_ref, h2_ref, eidx_ref, pos_ref, gate_ref, cnt_out_ref,
                     cnt_ref, v_ref):
    hin = _rms(x_ref[...], gin_ref[...]) * (1.0 + mod_ref[1:2, :]) + mod_ref[0:1, :]
    hinb = hin.astype(BF16)
    u = ((_dot(hinb, wa_ref[...]) + ba_ref[...])
         * jax.nn.sigmoid(_dot(hinb, wg_ref[...]) + bg_ref[...])).astype(BF16)
    nrow = TM // GRID_W
    ext_rows = GRID_W + 2 * _PAD
    first = _PAD - CONF_K // 2
    span = GRID_W + 8 * ((CONF_K - 1) // 8)
    assert first + 7 + span <= ext_rows
    ri = lax.broadcasted_iota(I32, (span, ext_rows), 0)
    ji = lax.broadcasted_iota(I32, (span, ext_rows), 1)
    shift = [jnp.where(ji == ri + first + b, 1.0, 0.0).astype(BF16) for b in range(8)]
    zpad = jnp.zeros((_PAD, D), BF16)
    for r in range(nrow):
        ext = jnp.concatenate([zpad, u[r * GRID_W:(r + 1) * GRID_W, :], zpad], axis=0)
        for c in range(D // _CW):
            cs = slice(c * _CW, (c + 1) * _CW)
            acc = jnp.broadcast_to(dwb_ref[:, cs], (GRID_W, _CW))
            for b in range(8):
                win = _dot(shift[b], ext[:, cs])
                for a in range(-(-CONF_K // 8)):
                    k = 8 * a + b
                    if k < CONF_K:
                        wk = jnp.concatenate([dww_ref[8 * k:8 * k + 8, cs]] * (GRID_W // 8), axis=0)
                        acc = acc + wk * win[8 * a:8 * a + GRID_W, :]
            v_ref[r * GRID_W:(r + 1) * GRID_W, cs] = acc
    v = v_ref[...]
    mu = jnp.mean(v, axis=-1, keepdims=True)
    vc = v - mu
    ln = vc * lax.rsqrt(jnp.mean(vc * vc, axis=-1, keepdims=True) + EPS) * lng_ref[...] + lnb_ref[...]
    y = _dot(_silu(ln).astype(BF16), w_ref[...]) + b_ref[...]
    _mix_epilogue(y, x_ref, mod_ref, gpost_ref, gpre_ref, rwT_ref, rb_ref,
                  x1_ref, h2_ref, eidx_ref, pos_ref, gate_ref, cnt_out_ref, cnt_ref)


def _mix_out_common_specs():
    full = lambda b, i: (0, 0)
    nt = SEQ // TM
    in_specs = [pl.BlockSpec((None, TM, D), lambda b, i: (b, i, 0)),
                pl.BlockSpec((None, 6, D), lambda b, i: (b, 0, 0)),
                pl.BlockSpec((1, D), full), pl.BlockSpec((1, D), full),
                pl.BlockSpec((E, D), full), pl.BlockSpec((E, 1), full)]
    out_specs = [pl.BlockSpec((None, TM, D), lambda b, i: (b, i, 0)),
                 pl.BlockSpec((TM * RW, 128), lambda b, i: (b * nt + i, 0)),
                 pl.BlockSpec((TOPK, TM), lambda b, i: (0, b * nt + i)),
                 pl.BlockSpec((TOPK, TM), lambda b, i: (0, b * nt + i)),
                 pl.BlockSpec((TM, TOPK), lambda b, i: (b * nt + i, 0)),
                 pl.BlockSpec((E, 1), full)]
    out_shape = [jax.ShapeDtypeStruct((BATCH, SEQ, D), F32),
                 jax.ShapeDtypeStruct((T * RW, 128), U32),
                 jax.ShapeDtypeStruct((TOPK, T), I32),
                 jax.ShapeDtypeStruct((TOPK, T), I32),
                 jax.ShapeDtypeStruct((T, TOPK), F32),
                 jax.ShapeDtypeStruct((E, 1), F32)]
    return in_specs, out_specs, out_shape


def _ssm_out(y, w, x, mod, gpost, gpre, rwT, rb):
    common_in, out_specs, out_shape = _mix_out_common_specs()
    return pl.pallas_call(
        _ssm_out_kernel,
        grid=(BATCH, SEQ // TM),
        in_specs=[pl.BlockSpec((None, TM, D_INNER), lambda b, i: (b, i, 0)),
                  pl.BlockSpec((D_INNER, D), lambda b, i: (0, 0))] + common_in,
        out_specs=out_specs, out_shape=out_shape,
        scratch_shapes=[pltpu.VMEM((E, 1), F32)],
        compiler_params=_cp(("arbitrary", "arbitrary")),
        name="ssm_out",
    )(y, w, x, mod, gpost, gpre, rwT, rb)


def _conf_out(gin, w_in, b_in, dww, dwb, lng, lnb, w, b, x, mod, gpost, gpre, rwT, rb):
    common_in, out_specs, out_shape = _mix_out_common_specs()
    full = lambda b_, i: (0, 0)
    half1 = lambda b_, i: (0, 1)
    return pl.pallas_call(
        _conf_out_kernel,
        grid=(BATCH, SEQ // TM),
        in_specs=[pl.BlockSpec((1, D), full),
                  pl.BlockSpec((D, D), full), pl.BlockSpec((D, D), half1),
                  pl.BlockSpec((1, D), full), pl.BlockSpec((1, D), half1),
                  pl.BlockSpec((CONF_K * 8, D), full), pl.BlockSpec((1, D), full),
                  pl.BlockSpec((1, D), full), pl.BlockSpec((1, D), full),
                  pl.BlockSpec((D, D), full), pl.BlockSpec((1, D), full)] + common_in,
        out_specs=out_specs, out_shape=out_shape,
        scratch_shapes=[pltpu.VMEM((E, 1), F32), pltpu.VMEM((TM, D), F32)],
        compiler_params=_cp(("arbitrary", "arbitrary")),
        name="conf_out",
    )(gin, w_in, w_in, b_in, b_in, dww, dwb, lng, lnb, w, b, x, mod, gpost, gpre, rwT, rb)


def _conf_in_kernel(x_ref, mod_ref, g_ref, wa_ref, wg_ref, ba_ref, bg_ref, u_ref):
    h = _rms(x_ref[...], g_ref[...]) * (1.0 + mod_ref[1:2, :]) + mod_ref[0:1, :]
    hb = h.astype(BF16)
    a = _dot(hb, wa_ref[...]) + ba_ref[...]
    g = _dot(hb, wg_ref[...]) + bg_ref[...]
    u_ref[...] = (a * jax.nn.sigmoid(g)).astype(BF16)


def _conf_in(x, mod, gain, w, bias):
    full = lambda b, i: (0, 0)
    return pl.pallas_call(
        _conf_in_kernel,
        grid=(BATCH, SEQ // TM),
        in_specs=[pl.BlockSpec((None, TM, D), lambda b, i: (b, i, 0)),
                  pl.BlockSpec((None, 6, D), lambda b, i: (b, 0, 0)),
                  pl.BlockSpec((1, D), full),
                  pl.BlockSpec((D, D), full), pl.BlockSpec((D, D), lambda b, i: (0, 1)),
                  pl.BlockSpec((1, D), full), pl.BlockSpec((1, D), lambda b, i: (0, 1))],
        out_specs=pl.BlockSpec((None, TM, D), lambda b, i: (b, i, 0)),
        out_shape=jax.ShapeDtypeStruct((BATCH, SEQ, D), BF16),
        compiler_params=_cp(("arbitrary", "arbitrary")),
        name="conf_in",
    )(x, mod, gain, w, w, bias, bias)


def _row_copy(src, si, dst, di, sem):
    return pltpu.make_async_copy(src.at[pl.ds(pl.multiple_of(si * RW, RW), RW), :],
                                 dst.at[pl.ds(pl.multiple_of(di * RW, RW), RW), :], sem)


ZROWS = BLK + 8
_TU = 8


def _dispatch_kernel(zs_ref, na_ref, dest_ref, h_ref, xs_ref, zero_ref, zsem, sem):
    @pl.when(pl.program_id(0) == 0)
    def _():
        zero_ref[...] = jnp.zeros_like(zero_ref)

        def zcopy(e):
            start = pl.multiple_of(zs_ref[e] * RW, 8 * RW)
            return pltpu.make_async_copy(zero_ref, xs_ref.at[pl.ds(start, ZROWS * RW), :], zsem)

        def zstart(e, c):
            zcopy(e).start()
            return c

        def zwait(e, c):
            zcopy(e).wait()
            return c
        lax.fori_loop(0, E, zstart, 0)
        lax.fori_loop(0, E, zwait, 0)

        ntail = NBLK + 1 - na_ref[0]

        def tcopy(c):
            start = pl.multiple_of((na_ref[0] + c) * (BLK * RW), BLK * RW)
            return pltpu.make_async_copy(zero_ref.at[pl.ds(0, BLK * RW), :],
                                         xs_ref.at[pl.ds(start, BLK * RW), :], zsem)

        def tstart(c, carry):
            tcopy(c).start()
            return carry

        def twait(c, carry):
            tcopy(c).wait()
            return carry
        lax.fori_loop(0, ntail, tstart, 0)
        lax.fori_loop(0, ntail, twait, 0)
        last = pltpu.make_async_copy(zero_ref.at[pl.ds(0, 8 * RW), :],
                                     xs_ref.at[pl.ds((PROWS - 8) * RW, 8 * RW), :], zsem)
        last.start()
        last.wait()

    def issue(tb, c):
        for u in range(_TU):
            t = tb * _TU + u
            for k in range(TOPK):
                _row_copy(h_ref, t, xs_ref, dest_ref[t * TOPK + k], sem).start(priority=k % 2)
        return c

    lax.fori_loop(0, TM // _TU, issue, 0)
    for k in range(TOPK):
        pltpu.make_async_copy(h_ref, xs_ref.at[pl.ds(0, TM * RW), :], sem).wait()


def _dispatch(zstart, nact, dest, hp):
    grid_spec = pltpu.PrefetchScalarGridSpec(
        num_scalar_prefetch=2,
        grid=(T // TM,),
        in_specs=[pl.BlockSpec((TM * TOPK,), lambda i, zs, na: (i,), memory_space=pltpu.SMEM),
                  pl.BlockSpec((TM * RW, 128), lambda i, zs, na: (i, 0))],
        out_specs=pl.BlockSpec(memory_space=pl.ANY),
        scratch_shapes=[pltpu.VMEM((ZROWS * RW, 128), U32), pltpu.SemaphoreType.DMA(()),
                        pltpu.SemaphoreType.DMA(())])
    return pl.pallas_call(
        _dispatch_kernel,
        grid_spec=grid_spec,
        out_shape=jax.ShapeDtypeStruct((PROWS * RW, 128), U32),
        compiler_params=_cp(("arbitrary",)),
        name="moe_dispatch",
    )(zstart, nact, dest, hp)


XRING = 3


def _experts_kernel(be_ref, na_ref, x_hbm, wg_ref, wu_ref, wd_ref, y_ref, wgb_ref, wub_ref, wdb_ref,
                    xbuf_ref, xsem):
    i = pl.program_id(0)
    na = na_ref[0]

    def fetch(j):
        rows = BLK * RW
        src = x_hbm.at[pl.ds(pl.multiple_of(j * rows, rows), rows), :]
        return pltpu.make_async_copy(src, xbuf_ref.at[j % XRING], xsem.at[j % XRING])

    @pl.when(i == 0)
    def _():
        for j in range(XRING - 1):
            @pl.when(j < na)
            def _():
                fetch(j).start()

    @pl.when(i + XRING - 1 < na)
    def _():
        fetch(i + XRING - 1).start()

    @pl.when((i == 0) | (be_ref[i] != be_ref[jnp.maximum(i - 1, 0)]))
    def _():
        wgb_ref[...] = wg_ref[...].astype(BF16)
        wub_ref[...] = wu_ref[...].astype(BF16)
        wdb_ref[...] = wd_ref[...].astype(BF16)

    @pl.when(i < na)
    def _():
        fetch(i).wait()
        xb = _load_packed(xbuf_ref.at[i % XRING], BLK)
        g = _dot(xb, wgb_ref[...])
        u = _dot(xb, wub_ref[...])
        a = (_silu(g) * u).astype(BF16)
        _store_packed(y_ref, _dot(a, wdb_ref[...]))

    @pl.when(i >= na)
    def _():
        y_ref[...] = jnp.zeros_like(y_ref)


def _experts(layer, block_e, nact, xs, wg, wu, wd):
    wmap = lambda i, be, na: (layer, be[i], 0, 0)
    grid_spec = pltpu.PrefetchScalarGridSpec(
        num_scalar_prefetch=2,
        grid=(NBLK,),
        in_specs=[pl.BlockSpec(memory_space=pl.ANY),
                  pl.BlockSpec((None, None, D, DE), wmap),
                  pl.BlockSpec((None, None, D, DE), wmap),
                  pl.BlockSpec((None, None, DE, D), wmap)],
        out_specs=pl.BlockSpec((BLK * RW, 128), lambda i, be, na: (i, 0)),
        scratch_shapes=[pltpu.VMEM((D, DE), BF16), pltpu.VMEM((D, DE), BF16),
                        pltpu.VMEM((DE, D), BF16),
                        pltpu.VMEM((XRING, BLK * RW, 128), U32), pltpu.SemaphoreType.DMA((XRING,))])
    return pl.pallas_call(
        _experts_kernel,
        grid_spec=grid_spec,
        out_shape=jax.ShapeDtypeStruct((NBLK * BLK * RW, 128), U32),
        compiler_params=_cp(("arbitrary",)),
        name="moe_experts",
    )(block_e, nact, xs, wg, wu, wd)


def _combine_kernel(dcur_ref, dnxt_ref, gate_ref, h_ref, x1_ref, mod_ref, gpost_ref,
                    sg_ref, su_ref, sd_ref, ys_ref, o_ref, buf_ref, sems):
    i = pl.program_id(0)
    n = pl.num_programs(0)

    def gather(dest_ref, slot):
        def body(tb, c):
            for u in range(_TU):
                t = tb * _TU + u
                for k in range(TOPK):
                    _row_copy(ys_ref, dest_ref[t * TOPK + k], buf_ref.at[slot, k], t,
                              sems.at[slot]).start(priority=k % 2)
            return c
        lax.fori_loop(0, TM // _TU, body, 0)

    @pl.when(i == 0)
    def _():
        gather(dcur_ref, 0)

    for slot_next in range(2):
        @pl.when((i + 1 < n) & ((i + 1) % 2 == slot_next))
        def _():
            gather(dnxt_ref, slot_next)

    hb = _load_packed(h_ref, TM)
    a = (_silu(_dot(hb, sg_ref[...])) * _dot(hb, su_ref[...])).astype(BF16)
    f = _dot(a, sd_ref[...])

    slot = i % 2

    for k in range(TOPK):
        pltpu.make_async_copy(ys_ref.at[pl.ds(0, TM * RW), :], buf_ref.at[slot, k],
                              sems.at[slot]).wait()

    gate = gate_ref[...]
    gb = [jnp.broadcast_to(gate[:, k:k + 1], (TM, 128)) for k in range(TOPK)]
    los, his = [], []
    for s in range(RW):
        lo = f[:, s * 128:(s + 1) * 128]
        hi = f[:, D // 2 + s * 128:D // 2 + (s + 1) * 128]
        for k in range(TOPK):
            wl, wh = _unpack2(buf_ref[slot, k, pl.ds(s, TM, stride=RW), :])
            lo = lo + gb[k] * wl
            hi = hi + gb[k] * wh
        los.append(lo)
        his.append(hi)
    f = jnp.concatenate(los + his, axis=1)
    o_ref[...] = x1_ref[...] + mod_ref[5:6, :] * _rms(f, gpost_ref[...])


def _combine(dest, gate, hp, x1, mod, gpost, sg, su, sd, ys):
    nt = SEQ // TM
    n = T // TM
    full = lambda i: (0, 0)
    return pl.pallas_call(
        _combine_kernel,
        grid=(n,),
        in_specs=[pl.BlockSpec((TM * TOPK,), lambda i: (i,), memory_space=pltpu.SMEM),
                  pl.BlockSpec((TM * TOPK,), lambda i: (jnp.minimum(i + 1, n - 1),),
                               memory_space=pltpu.SMEM),
                  pl.BlockSpec((TM, TOPK), lambda i: (i, 0)),
                  pl.BlockSpec((TM * RW, 128), lambda i: (i, 0)),
                  pl.BlockSpec((TM, D), lambda i: (i, 0)),
                  pl.BlockSpec((None, 6, D), lambda i: (i // nt, 0, 0)),
                  pl.BlockSpec((1, D), full),
                  pl.BlockSpec((D, DE), full), pl.BlockSpec((D, DE), full), pl.BlockSpec((DE, D), full),
                  pl.BlockSpec(memory_space=pl.ANY)],
        out_specs=pl.BlockSpec((TM, D), lambda i: (i, 0)),
        out_shape=jax.ShapeDtypeStruct((T, D), F32),
        scratch_shapes=[pltpu.VMEM((2, TOPK, TM * RW, 128), U32), pltpu.SemaphoreType.DMA((2,))],
        compiler_params=_cp(("arbitrary",)),
        name="moe_combine",
    )(dest, dest, gate, hp, x1.reshape(T, D), mod, gpost, sg, su, sd, ys).reshape(BATCH, SEQ, D)


def _moe(layer, x1, hp, eidx, pos, gate, counts, mod, gpost, wg, wu, wd, sg, su, sd):
    cnt = counts.reshape(E).astype(I32)
    padded = (cnt + BLK - 1) // BLK * BLK
    pend = jnp.cumsum(padded)
    pstart = pend - padded
    ids = jnp.arange(E, dtype=I32)
    dest = pos + jnp.sum(jnp.where(eidx[:, :, None] == ids, pstart, 0), axis=-1)
    dest = dest.T.reshape(T * TOPK)
    nact = (pend[-1] // BLK).reshape(1).astype(I32)
    blk0 = jnp.arange(NBLK, dtype=I32) * BLK
    block_e = jnp.sum((pend[None, :] <= blk0[:, None]).astype(I32), axis=1)
    block_e = jnp.minimum(block_e, E - 1).astype(I32)
    xs = _dispatch(((pstart + cnt) // 8 * 8).astype(I32), nact, dest, hp)
    ys = _experts(layer, block_e, nact, xs, wg, wu, wd)
    return _combine(dest, gate, hp, x1, mod, gpost, sg.astype(BF16), su.astype(BF16),
                    sd.astype(BF16), ys)


def kernel(x, c, ctx, c_ctx, ada_w, ada_b, norm_mix_pre, norm_mix_post, norm_ffn_pre, norm_ffn_post, ssm_w_in, ssm_conv_w, ssm_conv_b, ssm_dt_bias, ssm_a_log, ssm_d, ssm_norm, ssm_w_out, cv_w_in, cv_b_in, cv_dw_w, cv_dw_b, cv_ln_g, cv_ln_b, cv_w_out, cv_b_out, router_w, router_b, exp_w_gate, exp_w_up, exp_w_down, sh_w_gate, sh_w_up, sh_w_down):
    row = lambda v: v.reshape(1, -1)
    cvec = jnp.concatenate([c, c_ctx[None, :], jnp.zeros((3, D), F32)], axis=0)
    mod = _ada(cvec, ada_w, ada_b).reshape(2, 8, 6, D)

    w_in = ssm_w_in[0].astype(BF16)
    dtb = ssm_dt_bias[0].reshape(1, 2 * HEADS)
    wdt = w_in[:, D_INNER + CONV_DIM:]
    z, xbc, dt, dtT = _ssm_in(ctx, x, mod[0], row(norm_mix_pre[0]), w_in, wdt, wdt.T,
                              dtb, dtb.reshape(2 * HEADS, 1))
    xs, bt, cm = _ssm_conv(xbc, jnp.repeat(ssm_conv_w[0], 8, axis=0), row(ssm_conv_b[0]))
    a = -jnp.exp(ssm_a_log[0].astype(F32)).reshape(1, 2 * HEADS)
    rexp = (jnp.arange(D_INNER)[None, :] // HEADDIM == jnp.arange(HEADS)[:, None]).astype(BF16)
    yb = _ssd_bwd(xs, cm, bt, dt, dtT, a, a.reshape(2 * HEADS, 1), rexp)
    dsk = jnp.repeat(ssm_d[0], HEADDIM).reshape(1, D_INNER)
    ygn = _ssd_fwd(xs, cm, bt, dt, dtT, a, a.reshape(2 * HEADS, 1), rexp, z, yb, dsk,
                   row(ssm_norm[0]))
    x1, h2, eidx, pos, gate, counts = _ssm_out(
        ygn, ssm_w_out[0].astype(BF16), x, mod[0], row(norm_mix_post[0]), row(norm_ffn_pre[0]),
        router_w[0].T, router_b[0].reshape(E, 1))
    x2 = _moe(0, x1, h2, eidx, pos, gate, counts, mod[0], row(norm_ffn_post[0]),
              exp_w_gate, exp_w_up, exp_w_down, sh_w_gate[0], sh_w_up[0], sh_w_down[0])

    x3, h4, eidx, pos, gate, counts = _conf_out(
        row(norm_mix_pre[1]), cv_w_in[0].astype(BF16), row(cv_b_in[0]),
        jnp.repeat(cv_dw_w[0], 8, axis=0), row(cv_dw_b[0]), row(cv_ln_g[0]), row(cv_ln_b[0]),
        cv_w_out[0].astype(BF16), row(cv_b_out[0]),
        x2, mod[1], row(norm_mix_post[1]), row(norm_ffn_pre[1]),
        router_w[1].T, router_b[1].reshape(E, 1))
    return _moe(1, x3, h4, eidx, pos, gate, counts, mod[1], row(norm_ffn_post[1]),
                exp_w_gate, exp_w_up, exp_w_down, sh_w_gate[1], sh_w_up[1], sh_w_down[1])
```
